```python
import math
import jax, jax.numpy as jnp
from jax import lax
import numpy as np

D_MODEL = 1024
BATCH = 4
SEQ = 4096
DEPTH = 4
DEC_BATCH = 128
DEC_SEQ = 1
PAST_LEN = 2048
PAGE_SIZE = 128

N_MIXERS = 3
N_A_LAYERS = (DEPTH + 2) // 3
N_B_LAYERS = (DEPTH + 1) // 3
N_C_LAYERS = DEPTH // 3

NSA_HEADS = 16
NSA_KV_HEADS = 4
NSA_HPG = NSA_HEADS // NSA_KV_HEADS
HEAD_DIM = D_MODEL // NSA_HEADS
NSA_BLOCK = 64
NSA_TOPK = 16
NSA_WINDOW = 512
NSA_QCHUNK = 64
NSA_N_PAGED = 4
NSA_IN = NSA_HEADS * HEAD_DIM + 6 * NSA_KV_HEADS * HEAD_DIM + 3 * NSA_HEADS
FORCED_SCORE = 1.0e4

REL_BUCKETS = 32
REL_MAX_DIST = 128

CONV_W = 3

POOL_WINDOWS = (2, 4, 8, 16)
POOL_GROUPS = 4
POOL_GW = D_MODEL // POOL_GROUPS
POOL_HIST = max(POOL_WINDOWS) - 1

MEM_LEN = 256
CA_HEADS = 4
CA_HEAD_DIM = D_MODEL // CA_HEADS

D_FF = 2816

NORM_EPS = 1e-6
NEG_INF = -1e30

kernel_name = 'hybrid_nsa_conv_pool_decoder_step'

f32 = jnp.float32


def rmsnorm(x, g):
    xf = x.astype(f32)
    y = xf * lax.rsqrt(jnp.mean(xf * xf, axis=-1, keepdims=True) + NORM_EPS)
    return (y * g.astype(f32)).astype(x.dtype)


def swiglu(x, w_in, w_out):
    g, u = jnp.split(x @ w_in, 2, axis=-1)
    return (jax.nn.silu(g) * u) @ w_out


def masked_softmax(logits, mask):
    l = jnp.where(mask, logits.astype(f32), NEG_INF)
    return jax.nn.softmax(l, axis=-1) * mask


def t5_bucket(rel):
    n = jnp.maximum(rel, 0)
    max_exact = REL_BUCKETS // 2
    nf = jnp.maximum(n, 1).astype(f32)
    large = max_exact + (jnp.log(nf / max_exact) / math.log(REL_MAX_DIST / max_exact)
                         * (REL_BUCKETS - max_exact)).astype(jnp.int32)
    large = jnp.minimum(large, REL_BUCKETS - 1)
    return jnp.where(n < max_exact, n, large)


def nsa_project(xn, w_in):
    b, s, _ = xn.shape
    p = xn @ w_in
    nq = NSA_HEADS * HEAD_DIM
    nkv = 6 * NSA_KV_HEADS * HEAD_DIM
    q = p[..., :nq].reshape(b, s, NSA_HEADS, HEAD_DIM)
    kv = p[..., nq:nq + nkv].reshape(b, s, 6, NSA_KV_HEADS, HEAD_DIM)
    gates = jax.nn.sigmoid(p[..., nq + nkv:].astype(f32)).astype(xn.dtype).reshape(b, s, NSA_HEADS, 3)
    return q, kv, gates


def compress_blocks(k, w):
    b, l, g, d = k.shape
    kb = k.reshape(b, l // NSA_BLOCK, NSA_BLOCK, g, d)
    return jnp.einsum('bnigd,id->bngd', kb, w)


def nsa_attend(q, gates, qpos, kc, vc, ks_blk, vs_blk, kw, vw, wpos, rel_table):
    b, nq, h, d = q.shape
    g, hpg = NSA_KV_HEADS, NSA_HPG
    nb = kc.shape[1]
    scale = HEAD_DIM ** -0.5
    qg = q.reshape(b, nq, g, hpg, d)
    table_g = rel_table.reshape(REL_BUCKETS, g, hpg)
    blk = jnp.arange(nb, dtype=jnp.int32)

    rel_c = qpos[:, None] - ((blk + 1) * NSA_BLOCK - 1)[None, :]
    mask_c = rel_c >= 0
    bias_c = jnp.moveaxis(table_g[t5_bucket(rel_c)], 1, -1)
    s_c = jnp.einsum('bqghd,bngd->bqghn', qg, kc) * scale + bias_c
    p_c = masked_softmax(s_c, mask_c[:, None, None, :])
    o_c = jnp.einsum('bqghn,bngd->bqghd', p_c.astype(vc.dtype), vc)

    cur = qpos // NSA_BLOCK
    importance = p_c.sum(axis=3)
    forced = (blk[None, :] == 0) | (blk[None, :] == cur[:, None]) | (blk[None, :] == cur[:, None] - 1)
    valid = blk[None, :] <= cur[:, None]
    sel_score = jnp.where(forced[None, :, None, :], FORCED_SCORE, importance)
    sel_score = jnp.where(valid[None, :, None, :], sel_score, -1.0)
    _, idx = lax.top_k(sel_score, min(NSA_TOPK, nb))
    b_ix = jnp.arange(b)[:, None, None, None]
    g_ix = jnp.arange(g)[None, None, :, None]
    k_sel = ks_blk[b_ix, idx, :, g_ix, :]
    v_sel = vs_blk[b_ix, idx, :, g_ix, :]
    kpos = idx[..., None] * NSA_BLOCK + jnp.arange(NSA_BLOCK, dtype=jnp.int32)
    rel_s = qpos[None, :, None, None, None] - kpos
    mask_s = rel_s >= 0
    bias_s = table_g[t5_bucket(rel_s), jnp.arange(g)[None, None, :, None, None]]
    bias_s = jnp.moveaxis(bias_s, -1, 3)
    s_s = jnp.einsum('bqghd,bqgknd->bqghkn', qg, k_sel) * scale + bias_s
    n_sel = s_s.shape[-2] * s_s.shape[-1]
    p_s = masked_softmax(s_s.reshape(b, nq, g, hpg, n_sel), mask_s.reshape(b, nq, g, 1, n_sel))
    o_s = jnp.einsum('bqghm,bqgmd->bqghd', p_s.astype(v_sel.dtype), v_sel.reshape(b, nq, g, n_sel, d))

    rel_w = qpos[:, None] - wpos[None, :]
    mask_w = (rel_w >= 0) & (rel_w <= NSA_WINDOW) & (wpos[None, :] >= 0)
    bias_w = jnp.moveaxis(table_g[t5_bucket(rel_w)], 1, -1)
    s_w = jnp.einsum('bqghd,blgd->bqghl', qg, kw) * scale + bias_w
    p_w = masked_softmax(s_w, mask_w[:, None, None, :])
    o_w = jnp.einsum('bqghl,blgd->bqghd', p_w.astype(vw.dtype), vw)

    gg = gates.reshape(b, nq, g, hpg, 3)
    o = gg[..., 0:1] * o_c + gg[..., 1:2] * o_s + gg[..., 2:3] * o_w
    return o.reshape(b, nq, h, d)


def nsa_prompt(xn, w_in, w_cmp, w_out, rel_table):
    b, s, _ = xn.shape
    q, kv, gates = nsa_project(xn, w_in)
    kc = compress_blocks(kv[:, :, 0], w_cmp[0])
    vc = compress_blocks(kv[:, :, 1], w_cmp[1])
    ks_blk = kv[:, :, 2].reshape(b, s // NSA_BLOCK, NSA_BLOCK, NSA_KV_HEADS, HEAD_DIM)
    vs_blk = kv[:, :, 3].reshape(b, s // NSA_BLOCK, NSA_BLOCK, NSA_KV_HEADS, HEAD_DIM)
    pad = ((0, 0), (NSA_WINDOW, 0), (0, 0), (0, 0))
    kw_pad = jnp.pad(kv[:, :, 4], pad)
    vw_pad = jnp.pad(kv[:, :, 5], pad)
    n_win = NSA_WINDOW + NSA_QCHUNK

    def one_block(c):
        t0 = c * NSA_QCHUNK
        qc = lax.dynamic_slice_in_dim(q, t0, NSA_QCHUNK, axis=1)
        gc = lax.dynamic_slice_in_dim(gates, t0, NSA_QCHUNK, axis=1)
        kwc = lax.dynamic_slice_in_dim(kw_pad, t0, n_win, axis=1)
        vwc = lax.dynamic_slice_in_dim(vw_pad, t0, n_win, axis=1)
        qpos = t0 + jnp.arange(NSA_QCHUNK, dtype=jnp.int32)
        wpos = t0 - NSA_WINDOW + jnp.arange(n_win, dtype=jnp.int32)
        return nsa_attend(qc, gc, qpos, kc, vc, ks_blk, vs_blk, kwc, vwc, wpos, rel_table)

    o = lax.map(one_block, jnp.arange(s // NSA_QCHUNK, dtype=jnp.int32))
    o = jnp.moveaxis(o, 0, 1).reshape(b, s, NSA_HEADS * HEAD_DIM)
    n_keep = min(NSA_WINDOW, s)
    pages = kv[:, :, :NSA_N_PAGED].reshape(b * s // PAGE_SIZE, PAGE_SIZE, NSA_N_PAGED, NSA_KV_HEADS, HEAD_DIM)
    return o @ w_out, pages, kv[:, s - n_keep:, NSA_N_PAGED:]


def nsa_sample(xn, cache_kv, layer, page_table, win, w_in, w_cmp, w_out, rel_table):
    b, s, _ = xn.shape
    q, kv, gates = nsa_project(xn, w_in)
    past = cache_kv[layer, page_table]
    past = past.reshape(b, -1, NSA_N_PAGED, NSA_KV_HEADS, HEAD_DIM)
    p_len = past.shape[1]
    total = p_len + s
    padded = -(-total // NSA_BLOCK) * NSA_BLOCK
    full = jnp.concatenate([past, kv[:, :, :NSA_N_PAGED]], axis=1)
    full = jnp.pad(full, ((0, 0), (0, padded - total), (0, 0), (0, 0), (0, 0)))
    kc = compress_blocks(full[:, :, 0], w_cmp[0])
    vc = compress_blocks(full[:, :, 1], w_cmp[1])
    ks_blk = full[:, :, 2].reshape(b, padded // NSA_BLOCK, NSA_BLOCK, NSA_KV_HEADS, HEAD_DIM)
    vs_blk = full[:, :, 3].reshape(b, padded // NSA_BLOCK, NSA_BLOCK, NSA_KV_HEADS, HEAD_DIM)
    n_buf = win.shape[1]
    win_all = jnp.concatenate([win, kv[:, :, NSA_N_PAGED:]], axis=1)
    qpos = p_len + jnp.arange(s, dtype=jnp.int32)
    wpos = p_len - n_buf + jnp.arange(n_buf + s, dtype=jnp.int32)
    o = nsa_attend(q, gates, qpos, kc, vc, ks_blk, vs_blk, win_all[:, :, 0], win_all[:, :, 1], wpos, rel_table)
    o = o.reshape(b, s, NSA_HEADS * HEAD_DIM)
    return o @ w_out, kv[:, :, :NSA_N_PAGED], win_all[:, s:]


def conv_mix(xn, hist, w_in, conv_w, w_out):
    h, c, bg = jnp.split(xn @ w_in, 3, axis=-1)
    u = c * h
    s = u.shape[1]
    ucat = jnp.concatenate([hist, u], axis=1)
    v = conv_w[0] * ucat[:, 0:s] + conv_w[1] * ucat[:, 1:s + 1] + conv_w[2] * ucat[:, 2:s + 2]
    return (bg * v) @ w_out, ucat[:, -(CONV_W - 1):]


def pool_mix(xn, hist, pos, w_grp, scale):
    b, s, d = xn.shape
    xcat = jnp.concatenate([hist, xn], axis=1)
    cs0 = jnp.pad(jnp.cumsum(xcat.astype(f32), axis=1), ((0, 0), (1, 0), (0, 0)))
    outs = []
    for gi, w in enumerate(POOL_WINDOWS):
        sl = slice(gi * POOL_GW, (gi + 1) * POOL_GW)
        win_sum = cs0[:, POOL_HIST + 1:POOL_HIST + 1 + s, sl] - cs0[:, POOL_HIST + 1 - w:POOL_HIST + 1 - w + s, sl]
        cnt = jnp.minimum(pos + 1, w).astype(f32)[None, :, None]
        outs.append(win_sum / cnt)
    pooled = jnp.concatenate(outs, axis=-1).astype(xn.dtype)
    diff = (pooled - xn).reshape(b, s, POOL_GROUPS, POOL_GW)
    y = jnp.einsum('bsgc,gce->bsge', diff, w_grp).reshape(b, s, d) * scale
    return y, xcat[:, -POOL_HIST:]


def memory_kv(mem, g, w_kv):
    b, m, _ = mem.shape
    return (rmsnorm(mem, g) @ w_kv).reshape(b, m, 2, CA_HEADS, CA_HEAD_DIM)


def cross_attn(xn, mem_kv, w_q, w_out):
    b, s, _ = xn.shape
    q = (xn @ w_q).reshape(b, s, CA_HEADS, CA_HEAD_DIM)
    sc = jnp.einsum('bshd,bmhd->bhsm', q, mem_kv[:, :, 0]) * (CA_HEAD_DIM ** -0.5)
    p = jax.nn.softmax(sc.astype(f32), axis=-1).astype(xn.dtype)
    o = jnp.einsum('bhsm,bmhd->bshd', p, mem_kv[:, :, 1])
    return o.reshape(b, s, CA_HEADS * CA_HEAD_DIM) @ w_out


def setup_inputs(seed: int = 0) -> dict:
    key = jax.random.key(seed)
    ks = jax.random.split(key, 32)
    n_pages = PAST_LEN // PAGE_SIZE
    n_used = DEC_BATCH * n_pages
    n_phys = n_used + n_used // 4
    perm = jax.random.permutation(ks[0], n_phys)
    page_table = perm[:n_used].reshape(DEC_BATCH, n_pages).astype(jnp.int32)
    n_buf = min(NSA_WINDOW, PAST_LEN)

    def nrm(k, shape, s=1.0):
        return jax.random.normal(k, shape, f32) * s

    return {
        'x_prompt': nrm(ks[1], (BATCH, SEQ, D_MODEL)),
        'x_sample': nrm(ks[2], (DEC_BATCH, DEC_SEQ, D_MODEL)),
        'cache_nsa_kv': nrm(ks[3], (N_A_LAYERS, n_phys, PAGE_SIZE, NSA_N_PAGED, NSA_KV_HEADS, HEAD_DIM)),
        'cache_nsa_win': nrm(ks[4], (N_A_LAYERS, DEC_BATCH, n_buf, 2, NSA_KV_HEADS, HEAD_DIM)),
        'state_conv': nrm(ks[5], (N_B_LAYERS, DEC_BATCH, CONV_W - 1, D_MODEL)),
        'state_pool': nrm(ks[6], (N_C_LAYERS, DEC_BATCH, POOL_HIST, D_MODEL)),
        'cache_mem_kv': nrm(ks[7], (DEPTH, DEC_BATCH, MEM_LEN, 2, CA_HEADS, CA_HEAD_DIM)),
        'page_table': page_table,
        'mem_prompt': nrm(ks[8], (BATCH, MEM_LEN, D_MODEL)),
        'rel_bias': nrm(ks[9], (REL_BUCKETS, NSA_HEADS), 0.5),
        'norm_g': 1.0 + nrm(ks[10], (DEPTH, 4, D_MODEL), 0.1),
        'norm_mem_g': 1.0 + nrm(ks[11], (DEPTH, D_MODEL), 0.1),
        'final_g': 1.0 + nrm(ks[12], (D_MODEL,), 0.1),
        'w_ffn_in': nrm(ks[13], (DEPTH, 2, D_MODEL, 2 * D_FF), D_MODEL ** -0.5),
        'w_ffn_out': nrm(ks[14], (DEPTH, 2, D_FF, D_MODEL), D_FF ** -0.5),
        'w_nsa_in': nrm(ks[15], (N_A_LAYERS, D_MODEL, NSA_IN), D_MODEL ** -0.5),
        'w_nsa_cmp': (1.0 + nrm(ks[16], (N_A_LAYERS, 2, NSA_BLOCK, HEAD_DIM), 0.1)) * (NSA_BLOCK ** -0.5),
        'w_nsa_out': nrm(ks[17], (N_A_LAYERS, NSA_HEADS * HEAD_DIM, D_MODEL), (NSA_HEADS * HEAD_DIM) ** -0.5),
        'w_conv_in': nrm(ks[18], (N_B_LAYERS, D_MODEL, 3 * D_MODEL), D_MODEL ** -0.5),
        'conv_w': nrm(ks[19], (N_B_LAYERS, CONV_W, D_MODEL), CONV_W ** -0.5),
        'w_conv_out': nrm(ks[20], (N_B_LAYERS, D_MODEL, D_MODEL), D_MODEL ** -0.5),
        'w_pool': nrm(ks[21], (N_C_LAYERS, POOL_GROUPS, POOL_GW, POOL_GW), POOL_GW ** -0.5),
        'pool_scale': 1.0 + nrm(ks[22], (N_C_LAYERS, D_MODEL), 0.1),
        'w_ca_q': nrm(ks[23], (DEPTH, D_MODEL, CA_HEADS * CA_HEAD_DIM), D_MODEL ** -0.5),
        'w_ca_kv': nrm(ks[24], (DEPTH, D_MODEL, 2 * CA_HEADS * CA_HEAD_DIM), D_MODEL ** -0.5),
        'w_ca_out': nrm(ks[25], (DEPTH, CA_HEADS * CA_HEAD_DIM, D_MODEL), (CA_HEADS * CA_HEAD_DIM) ** -0.5),
    }


def reference(x_prompt, x_sample, cache_nsa_kv, cache_nsa_win, state_conv, state_pool, cache_mem_kv,
              page_table, mem_prompt, rel_bias, norm_g, norm_mem_g, final_g, w_ffn_in, w_ffn_out,
              w_nsa_in, w_nsa_cmp, w_nsa_out, w_conv_in, conv_w, w_conv_out, w_pool, pool_scale,
              w_ca_q, w_ca_kv, w_ca_out):
    xp, xs = x_prompt, x_sample
    bp, sp, _ = xp.shape
    sd = xs.shape[1]
    pos_p = jnp.arange(sp, dtype=jnp.int32)
    pos_s = PAST_LEN + jnp.arange(sd, dtype=jnp.int32)
    nsa_kv_p, nsa_win_p, conv_p, pool_p, mem_p = [], [], [], [], []
    nsa_kv_s, nsa_win_s, conv_s, pool_s = [], [], [], []
    for i in range(DEPTH):
        kind, j = i % N_MIXERS, i // N_MIXERS
        xp = xp + 0.5 * swiglu(rmsnorm(xp, norm_g[i, 0]), w_ffn_in[i, 0], w_ffn_out[i, 0])
        xs = xs + 0.5 * swiglu(rmsnorm(xs, norm_g[i, 0]), w_ffn_in[i, 0], w_ffn_out[i, 0])
        hp = rmsnorm(xp, norm_g[i, 1])
        hs = rmsnorm(xs, norm_g[i, 1])
        if kind == 0:
            yp, kvp, winp = nsa_prompt(hp, w_nsa_in[j], w_nsa_cmp[j], w_nsa_out[j], rel_bias)
            ys, kvs, wins = nsa_sample(hs, cache_nsa_kv, j, page_table, cache_nsa_win[j],
                                       w_nsa_in[j], w_nsa_cmp[j], w_nsa_out[j], rel_bias)
            nsa_kv_p.append(kvp); nsa_win_p.append(winp)
            nsa_kv_s.append(kvs); nsa_win_s.append(wins)
        elif kind == 1:
            yp, cpn = conv_mix(hp, jnp.zeros((bp, CONV_W - 1, D_MODEL), hp.dtype), w_conv_in[j], conv_w[j], w_conv_out[j])
            ys, csn = conv_mix(hs, state_conv[j], w_conv_in[j], conv_w[j], w_conv_out[j])
            conv_p.append(cpn); conv_s.append(csn)
        else:
            yp, ppn = pool_mix(hp, jnp.zeros((bp, POOL_HIST, D_MODEL), hp.dtype), pos_p, w_pool[j], pool_scale[j])
            ys, psn = pool_mix(hs, state_pool[j], pos_s, w_pool[j], pool_scale[j])
            pool_p.append(ppn); pool_s.append(psn)
        xp = xp + yp
        xs = xs + ys
        mkv = memory_kv(mem_prompt, norm_mem_g[i], w_ca_kv[i])
        mem_p.append(mkv)
        xp = xp + cross_attn(rmsnorm(xp, norm_g[i, 2]), mkv, w_ca_q[i], w_ca_out[i])
        xs = xs + cross_attn(rmsnorm(xs, norm_g[i, 2]), cache_mem_kv[i], w_ca_q[i], w_ca_out[i])
        xp = xp + 0.5 * swiglu(rmsnorm(xp, norm_g[i, 3]), w_ffn_in[i, 1], w_ffn_out[i, 1])
        xs = xs + 0.5 * swiglu(rmsnorm(xs, norm_g[i, 3]), w_ffn_in[i, 1], w_ffn_out[i, 1])
    y_prompt = rmsnorm(xp, final_g)
    y_sample = rmsnorm(xs, final_g)
    return (y_prompt, y_sample,
            jnp.stack(nsa_kv_p), jnp.stack(nsa_win_p), jnp.stack(conv_p), jnp.stack(pool_p), jnp.stack(mem_p),
            jnp.stack(nsa_kv_s), jnp.stack(nsa_win_s), jnp.stack(conv_s), jnp.stack(pool_s))
```

```python
import functools
import math

import jax
import jax.numpy as jnp
from jax import lax
from jax.experimental import pallas as pl
from jax.experimental.pallas import tpu as pltpu

f32 = jnp.float32
bf16 = jnp.bfloat16
i32 = jnp.int32

NORM_EPS = 1e-6
NEG_INF = -1e30

NSA_HEADS = 16
NSA_KV_HEADS = 4
NSA_HPG = NSA_HEADS // NSA_KV_HEADS
HEAD_DIM = 64
NSA_BLOCK = 64
NSA_TOPK = 16
NSA_WINDOW = 512
REL_BUCKETS = 32
REL_MAX_DIST = 128
CONV_W = 3
POOL_WINDOWS = (2, 4, 8, 16)
POOL_HIST = max(POOL_WINDOWS) - 1
CA_HEADS = 4

LANES = 128
VMEM_LIMIT_BYTES = 56 * 1024 * 1024

N_NEAR = 3
FAR_CHUNK_BLOCKS = 8
WIN_FAR_BLOCKS = NSA_WINDOW // NSA_BLOCK + 1 - N_NEAR


def _cparams(*sem):
    return pltpu.CompilerParams(dimension_semantics=sem, vmem_limit_bytes=VMEM_LIMIT_BYTES)


def _rms(x, g):
    return x * lax.rsqrt(jnp.mean(x * x, axis=-1, keepdims=True) + NORM_EPS) * g


def _dot(a, b):
    return jnp.dot(a, b, preferred_element_type=f32)


def _dot_nt(a, b):
    return lax.dot_general(a, b, (((1,), (1,)), ((), ())), preferred_element_type=f32)


def _row_tile(m, target):
    t = min(m, target)
    while m % t:
        t //= 2
    return t


def _ffn_kernel(x_ref, g_ref, wg_ref, wu_ref, wo_ref, *rest, n_f, final_norm):
    if final_norm:
        gf_ref, o_ref, xn_ref, acc_ref = rest
    else:
        o_ref, xn_ref, acc_ref = rest
    j = pl.program_id(1)

    @pl.when(j == 0)
    def _():
        xn_ref[...] = _rms(x_ref[...], g_ref[...]).astype(bf16)
        acc_ref[...] = jnp.zeros_like(acc_ref)

    xn = xn_ref[...]
    gate = _dot(xn, wg_ref[...])
    up = _dot(xn, wu_ref[...])
    act = (gate * jax.nn.sigmoid(gate) * up).astype(bf16)
    acc_ref[...] += _dot(act, wo_ref[...])

    @pl.when(j == n_f - 1)
    def _():
        y = x_ref[...] + 0.5 * acc_ref[...]
        if final_norm:
            y = _rms(y, gf_ref[...])
        o_ref[...] = y


def _ffn(x, g, w_in, w_out, final_g=None, tm_target=512):
    m, d = x.shape
    f = w_out.shape[0]
    tm = _row_tile(m, tm_target)
    tf = f
    for cand in (1408, 1024, 512, 256, 128):
        if f % cand == 0:
            tf = cand
            break
    n_f = f // tf
    in_specs = [
        pl.BlockSpec((tm, d), lambda i, j: (i, 0)),
        pl.BlockSpec((1, d), lambda i, j: (0, 0)),
        pl.BlockSpec((d, tf), lambda i, j: (0, j)),
        pl.BlockSpec((d, tf), lambda i, j: (0, n_f + j)),
        pl.BlockSpec((tf, d), lambda i, j: (j, 0)),
    ]
    args = [x, g.reshape(1, d), w_in, w_in, w_out]
    if final_g is not None:
        in_specs.append(pl.BlockSpec((1, d), lambda i, j: (0, 0)))
        args.append(final_g.reshape(1, d))
    return pl.pallas_call(
        functools.partial(_ffn_kernel, n_f=n_f, final_norm=final_g is not None),
        grid=(m // tm, n_f),
        in_specs=in_specs,
        out_specs=pl.BlockSpec((tm, d), lambda i, j: (i, 0)),
        out_shape=jax.ShapeDtypeStruct((m, d), f32),
        scratch_shapes=[pltpu.VMEM((tm, d), bf16), pltpu.VMEM((tm, d), f32)],
        compiler_params=_cparams("parallel", "arbitrary"),
        name="ffn",
    )(*args)


def _normproj_kernel(x_ref, g_ref, w_ref, o_ref):
    xn = _rms(x_ref[...], g_ref[...]).astype(bf16)
    o_ref[...] = _dot(xn, w_ref[...])


def _normproj(x, g, w, tm_target=512):
    m, d = x.shape
    n = w.shape[1]
    tm = _row_tile(m, tm_target)
    return pl.pallas_call(
        _normproj_kernel,
        grid=(m // tm,),
        in_specs=[pl.BlockSpec((tm, d), lambda i: (i, 0)),
                  pl.BlockSpec((1, d), lambda i: (0, 0)),
                  pl.BlockSpec((d, n), lambda i: (0, 0))],
        out_specs=pl.BlockSpec((tm, n), lambda i: (i, 0)),
        out_shape=jax.ShapeDtypeStruct((m, n), f32),
        compiler_params=_cparams("parallel"),
        name="normproj",
    )(x, g.reshape(1, d), w)


def _proj_res_kernel(a_ref, w_ref, x_ref, o_ref):
    o_ref[...] = x_ref[...] + _dot(a_ref[...].astype(bf16), w_ref[...])


def _proj_res(a, w, x, tm_target=512):
    m, k = a.shape
    d = w.shape[1]
    tm = _row_tile(m, tm_target)
    return pl.pallas_call(
        _proj_res_kernel,
        grid=(m // tm,),
        in_specs=[pl.BlockSpec((tm, k), lambda i: (i, 0)),
                  pl.BlockSpec((k, d), lambda i: (0, 0)),
                  pl.BlockSpec((tm, d), lambda i: (i, 0))],
        out_specs=pl.BlockSpec((tm, d), lambda i: (i, 0)),
        out_shape=jax.ShapeDtypeStruct((m, d), f32),
        compiler_params=_cparams("parallel"),
        name="proj_res",
    )(a, w, x)


def _nsa_proj_kernel(x_ref, g_ref, wq_ref, wkv_ref, wg_ref, q_ref, kv_ref, gt_ref):
    xn = _rms(x_ref[...], g_ref[...]).astype(bf16)
    q_ref[...] = _dot(xn, wq_ref[...])
    kv = _dot(xn, wkv_ref[...])
    for s in range(kv_ref.shape[0]):
        kv_ref[s] = kv[:, s * LANES:(s + 1) * LANES]
    gt = jax.nn.sigmoid(_dot(xn, wg_ref[...]))
    for s in range(gt_ref.shape[0]):
        gt_ref[s] = gt[:, s * LANES:(s + 1) * LANES]


def _nsa_proj(x, g, wq, wkv, wg, tm_target=512):
    m, d = x.shape
    nq, nkv, ng = wq.shape[1], wkv.shape[1], wg.shape[1]
    n_slab, n_gs = nkv // LANES, ng // LANES
    tm = _row_tile(m, tm_target)
    return pl.pallas_call(
        _nsa_proj_kernel,
        grid=(m // tm,),
        in_specs=[pl.BlockSpec((tm, d), lambda i: (i, 0)),
                  pl.BlockSpec((1, d), lambda i: (0, 0)),
                  pl.BlockSpec((d, nq), lambda i: (0, 0)),
                  pl.BlockSpec((d, nkv), lambda i: (0, 0)),
                  pl.BlockSpec((d, ng), lambda i: (0, 0))],
        out_specs=[pl.BlockSpec((tm, nq), lambda i: (i, 0)),
                   pl.BlockSpec((n_slab, tm, LANES), lambda i: (0, i, 0)),
                   pl.BlockSpec((n_gs, tm, LANES), lambda i: (0, i, 0))],
        out_shape=[jax.ShapeDtypeStruct((m, nq), f32),
                   jax.ShapeDtypeStruct((n_slab, m, LANES), f32),
                   jax.ShapeDtypeStruct((n_gs, m, LANES), f32)],
        compiler_params=_cparams("parallel"),
        name="nsa_proj",
    )(x, g.reshape(1, d), wq, wkv, wg)


def _compress_kernel(kv_ref, w_ref, o_ref):
    tm = kv_ref.shape[1]
    nb = tm // NSA_BLOCK
    x = kv_ref[0].reshape(nb, NSA_BLOCK, LANES)
    o_ref[0] = jnp.sum(x * w_ref[...][None], axis=1)


def _compress(kvp, w_pack, n_groups, tm_target=512):
    m = kvp.shape[1]
    tm = _row_tile(m, tm_target)
    nb = tm // NSA_BLOCK
    return pl.pallas_call(
        _compress_kernel,
        grid=(n_groups, m // tm),
        in_specs=[pl.BlockSpec((1, tm, LANES), lambda g, i: (g, i, 0)),
                  pl.BlockSpec((NSA_BLOCK, LANES), lambda g, i: (0, 0))],
        out_specs=pl.BlockSpec((1, nb, LANES), lambda g, i: (g, i, 0)),
        out_shape=jax.ShapeDtypeStruct((n_groups, m // NSA_BLOCK, LANES), f32),
        compiler_params=_cparams("parallel", "parallel"),
        name="nsa_compress",
    )(kvp, w_pack)


def _t5_bucket(rel):
    n = jnp.maximum(rel, 0)
    max_exact = REL_BUCKETS // 2
    nf = jnp.maximum(n, 1).astype(f32)
    large = max_exact + (jnp.log(nf / max_exact) / math.log(REL_MAX_DIST / max_exact)
                         * (REL_BUCKETS - max_exact)).astype(i32)
    large = jnp.minimum(large, REL_BUCKETS - 1)
    return jnp.where(n < max_exact, n, large)


def _bias_lookup(table_ref, bucket, head):
    out = jnp.zeros(bucket.shape, f32)
    for k in range(REL_BUCKETS):
        out = jnp.where(bucket == k, table_ref[k, head], out)
    return out


def _t5_prompt_kernel(table_ref, tile_ref, cmpq_ref, cmpt_ref):
    g = pl.program_id(0)
    n_bc = cmpq_ref.shape[3]
    qi = lax.broadcasted_iota(i32, (NSA_BLOCK, NSA_BLOCK), 0)
    kj = lax.broadcasted_iota(i32, (NSA_BLOCK, NSA_BLOCK), 1)
    qi_b = lax.broadcasted_iota(i32, (NSA_BLOCK, n_bc), 0)
    shp_t = (8, NSA_BLOCK)
    bucket_t = _t5_bucket(lax.broadcasted_iota(i32, shp_t, 1) - (NSA_BLOCK - 1)
                          + NSA_BLOCK * lax.broadcasted_iota(i32, shp_t, 0))
    for hh in range(NSA_HPG):
        head = g * NSA_HPG + hh
        far = table_ref[REL_BUCKETS - 1, head]
        rows = slice(hh * NSA_BLOCK, (hh + 1) * NSA_BLOCK)
        for d in range(N_NEAR):
            tile_ref[0, d, rows, :] = _bias_lookup(table_ref, _t5_bucket(qi - kj + NSA_BLOCK * d), head) - far
            cmpq_ref[0, d, rows, :] = _bias_lookup(
                table_ref, _t5_bucket(qi_b - (NSA_BLOCK - 1) + NSA_BLOCK * d), head) - far
        cmpt_ref[0, :, rows] = _bias_lookup(table_ref, bucket_t, head) - far


def _t5_prompt_tables(rel_bias, n_blk):
    rows = NSA_HPG * NSA_BLOCK
    return pl.pallas_call(
        _t5_prompt_kernel,
        grid=(NSA_KV_HEADS,),
        in_specs=[pl.BlockSpec(memory_space=pltpu.SMEM)],
        out_specs=[pl.BlockSpec((1, N_NEAR, rows, NSA_BLOCK), lambda g: (g, 0, 0, 0)),
                   pl.BlockSpec((1, N_NEAR, rows, n_blk), lambda g: (g, 0, 0, 0)),
                   pl.BlockSpec((1, 8, rows), lambda g: (g, 0, 0))],
        out_shape=[jax.ShapeDtypeStruct((NSA_KV_HEADS, N_NEAR, rows, NSA_BLOCK), f32),
                   jax.ShapeDtypeStruct((NSA_KV_HEADS, N_NEAR, rows, n_blk), f32),
                   jax.ShapeDtypeStruct((NSA_KV_HEADS, 8, rows), f32)],
        compiler_params=_cparams("parallel"),
        name="t5_prompt_tables",
    )(rel_bias)


def _t5_sample_kernel(table_ref, key_ref, blk_ref, *, qpos):
    kpos = lax.broadcasted_iota(i32, (1, key_ref.shape[1]), 1)
    bucket_k = _t5_bucket(qpos - kpos)
    blk = lax.broadcasted_iota(i32, (1, blk_ref.shape[1]), 1)
    bucket_b = _t5_bucket(qpos - (blk * NSA_BLOCK + NSA_BLOCK - 1))
    for r in range(NSA_HEADS):
        head = (r % NSA_KV_HEADS) * NSA_HPG + r // NSA_KV_HEADS
        key_ref[r:r + 1, :] = _bias_lookup(table_ref, bucket_k, head)
        blk_ref[r:r + 1, :] = _bias_lookup(table_ref, bucket_b, head)


def _t5_sample_tables(rel_bias, qpos, n_key, n_blk):
    return pl.pallas_call(
        functools.partial(_t5_sample_kernel, qpos=qpos),
        in_specs=[pl.BlockSpec(memory_space=pltpu.SMEM)],
        out_shape=[jax.ShapeDtypeStruct((NSA_HEADS, n_key), f32),
                   jax.ShapeDtypeStruct((NSA_HEADS, n_blk), f32)],
        name="t5_sample_tables",
    )(rel_bias)


def _flash_update(carry, s, msk, kv):
    m, l, acc = carry
    s = jnp.where(msk, s, NEG_INF)
    m_new = jnp.maximum(m, jnp.max(s, axis=1, keepdims=True))
    alpha = jnp.exp(m - m_new)
    p = jnp.where(msk, jnp.exp(s - m_new), 0.0)
    l = alpha * l + jnp.sum(p, axis=1, keepdims=True)
    acc = alpha * acc + _dot(p.astype(bf16), kv)
    return m_new, l, acc


def _flash_finish(carry):
    _, l, acc = carry
    return acc / jnp.where(l > 0.0, l, 1.0)


def _tile_rows(x, n):
    return jnp.concatenate([x] * n, axis=0)


def _select_blocks(imp_t, c, n_blk):
    blk = lax.broadcasted_iota(i32, imp_t.shape, 0)
    valid = blk <= c
    forced = (blk == 0) | (blk == c) | (blk == c - 1)

    def all_valid():
        return valid.astype(f32)

    def top_k():
        blk_f = blk.astype(f32)
        work0 = jnp.where(valid & jnp.logical_not(forced), imp_t, -1.0)
        sel0 = forced.astype(f32)

        def body(_, wc):
            work, sel = wc
            mx = jnp.max(work, axis=0, keepdims=True)
            first = jnp.min(jnp.where(work == mx, blk_f, float(n_blk)), axis=0, keepdims=True)
            pick = blk_f == first
            return jnp.where(pick, -2.0, work), jnp.where(pick, 1.0, sel)

        _, sel = lax.fori_loop(0, NSA_TOPK - 3, body, (work0, sel0), unroll=True)
        return sel

    return lax.cond(c + 1 <= NSA_TOPK, all_valid, top_k)


def _nsa_prompt_kernel(q_ref, gt_ref, kvc_ref, kvs_ref, kvw_ref, tile_ref, cmpq_ref, cmpt_ref, o_ref, *,
                       n_blk):
    c = pl.program_id(2)
    rows = NSA_HPG * NSA_BLOCK
    scale = HEAD_DIM ** -0.5
    lower = lax.broadcasted_iota(i32, (NSA_BLOCK, LANES), 1) < HEAD_DIM

    q_parts = []
    for h in range(NSA_HPG):
        x = q_ref[:, (h // 2) * LANES:(h // 2 + 1) * LANES] * scale
        if h % 2:
            x = pltpu.roll(x, HEAD_DIM, 1)
        q_parts.append(jnp.where(lower, x, 0.0))
    q_pad = jnp.concatenate(q_parts, axis=0).astype(bf16)

    row_qi = lax.broadcasted_iota(i32, (rows, NSA_BLOCK), 0) & (NSA_BLOCK - 1)
    lane_kj = lax.broadcasted_iota(i32, (rows, NSA_BLOCK), 1)
    causal = lane_kj <= row_qi

    kvc = kvc_ref[0].astype(bf16)
    blk = lax.broadcasted_iota(i32, (rows, n_blk), 1)
    qi = lax.broadcasted_iota(i32, (rows, n_blk), 0) & (NSA_BLOCK - 1)
    mask_c = (blk < c) | ((blk == c) & (qi == NSA_BLOCK - 1))
    bias_c = jnp.where(blk == c, cmpq_ref[0, 0],
                       jnp.where(blk == c - 1, cmpq_ref[0, 1],
                                 jnp.where(blk == c - 2, cmpq_ref[0, 2], 0.0)))
    s_c = jnp.where(mask_c, _dot_nt(q_pad, kvc) + bias_c, NEG_INF)
    m_c = jnp.max(s_c, axis=1, keepdims=True)
    e_c = jnp.where(mask_c, jnp.exp(s_c - m_c), 0.0)
    l_c = jnp.sum(e_c, axis=1, keepdims=True)
    p_c = e_c / jnp.where(l_c > 0.0, l_c, 1.0)
    o_c = _dot(p_c.astype(bf16), kvc)

    blk_t = lax.broadcasted_iota(i32, (n_blk, rows), 0)
    qi_t = lax.broadcasted_iota(i32, (n_blk, rows), 1) & (NSA_BLOCK - 1)
    mask_t = (blk_t < c) | ((blk_t == c) & (qi_t == NSA_BLOCK - 1))
    cols = cmpt_ref[0]
    bias_t = jnp.where(blk_t == c, cols[0:1, :],
                       jnp.where(blk_t == c - 1, cols[1:2, :],
                                 jnp.where(blk_t == c - 2, cols[2:3, :], 0.0)))
    s_t = jnp.where(mask_t, _dot_nt(kvc, q_pad) + bias_t, NEG_INF)
    m_t = jnp.max(s_t, axis=0, keepdims=True)
    e_t = jnp.where(mask_t, jnp.exp(s_t - m_t), 0.0)
    l_t = jnp.sum(e_t, axis=0, keepdims=True)
    p_t = e_t / jnp.where(l_t > 0.0, l_t, 1.0)
    imp_t = p_t[:, 0:LANES]
    for pr in range(1, NSA_HPG // 2):
        imp_t = imp_t + p_t[:, pr * LANES:(pr + 1) * LANES]
    imp_t = imp_t + pltpu.roll(imp_t, NSA_BLOCK, 1)
    sel_t = _select_blocks(imp_t, c, n_blk)
    eye = (lax.broadcasted_iota(i32, (NSA_BLOCK, LANES), 0)
           == lax.broadcasted_iota(i32, (NSA_BLOCK, LANES), 1)).astype(bf16)
    sel = _dot_nt(eye, sel_t.astype(bf16))
    blk_q = lax.broadcasted_iota(i32, (NSA_BLOCK, n_blk), 1)

    init = (jnp.full((rows, 1), NEG_INF, f32), jnp.zeros((rows, 1), f32), jnp.zeros((rows, LANES), f32))

    def near_update(carry, kv_ref, masks):
        for i in range(N_NEAR):
            start = pl.multiple_of(jnp.maximum(c - i, 0) * NSA_BLOCK, NSA_BLOCK)
            kv_i = kv_ref[0, pl.ds(start, NSA_BLOCK), :].astype(bf16)
            carry = _flash_update(carry, _dot_nt(q_pad, kv_i) + tile_ref[0, i], masks[i], kv_i)
        return carry

    n_far = jnp.maximum(c - (N_NEAR - 1), 0)
    sel_far = jnp.where(blk_q < c - (N_NEAR - 1), sel, 0.0).astype(bf16)
    tk = FAR_CHUNK_BLOCKS * NSA_BLOCK

    def far_body(j, carry):
        kv = kvs_ref[0, pl.ds(pl.multiple_of(j * tk, tk), tk), :].astype(bf16)
        s = _dot_nt(q_pad, kv)
        e_j = (lax.broadcasted_iota(i32, (n_blk, tk), 0)
               == j * FAR_CHUNK_BLOCKS + (lax.broadcasted_iota(i32, (n_blk, tk), 1) >> 6)).astype(bf16)
        msk = _tile_rows(_dot(sel_far, e_j), NSA_HPG) > 0.5
        return _flash_update(carry, s, msk, kv)

    carry = lax.fori_loop(0, (n_far + FAR_CHUNK_BLOCKS - 1) // FAR_CHUNK_BLOCKS, far_body, init)
    sel_d2 = jnp.sum(jnp.where(blk_q == c - 2, sel, 0.0), axis=1, keepdims=True)
    sel_d2 = jnp.broadcast_to(_tile_rows(sel_d2, NSA_HPG), (rows, NSA_BLOCK)) > 0.5
    has_d1 = jnp.broadcast_to(c >= 1, (rows, NSA_BLOCK))
    has_d2 = jnp.broadcast_to(c >= 2, (rows, NSA_BLOCK))
    o_s = _flash_finish(near_update(carry, kvs_ref, [causal, has_d1, sel_d2 & has_d2]))

    n_wf = WIN_FAR_BLOCKS * NSA_BLOCK
    start_w = pl.multiple_of(jnp.maximum(c - (NSA_WINDOW // NSA_BLOCK), 0) * NSA_BLOCK, NSA_BLOCK)
    kv_w = kvw_ref[0, pl.ds(start_w, n_wf), :].astype(bf16)
    kpos = start_w + lax.broadcasted_iota(i32, (rows, n_wf), 1)
    rel = c * NSA_BLOCK + (lax.broadcasted_iota(i32, (rows, n_wf), 0) & (NSA_BLOCK - 1)) - kpos
    msk_w = (kpos < (c - (N_NEAR - 1)) * NSA_BLOCK) & (rel <= NSA_WINDOW)
    carry = _flash_update(init, _dot_nt(q_pad, kv_w), msk_w, kv_w)
    o_w = _flash_finish(near_update(carry, kvw_ref, [causal, has_d1, has_d2]))

    gt = gt_ref[0]
    lane_g = lax.broadcasted_iota(i32, (NSA_BLOCK, LANES), 1)

    def gate(h, branch):
        return jnp.sum(jnp.where(lane_g == h * 3 + branch, gt, 0.0), axis=1, keepdims=True)

    heads = []
    for h in range(NSA_HPG):
        r = slice(h * NSA_BLOCK, (h + 1) * NSA_BLOCK)
        heads.append(gate(h, 0) * o_c[r] + gate(h, 1) * o_s[r] + gate(h, 2) * o_w[r])
    for pr in range(NSA_HPG // 2):
        o_ref[:, pr * LANES:(pr + 1) * LANES] = jnp.where(
            lower, pltpu.roll(heads[2 * pr], HEAD_DIM, 1), heads[2 * pr + 1])


def _nsa_prompt_attn(q, gates, kvc, kvp, tiles, cmpq, cmpt, n_batch, seq):
    m = q.shape[0]
    n_blk = seq // NSA_BLOCK
    g_ = NSA_KV_HEADS
    wq = NSA_HPG * HEAD_DIM
    return pl.pallas_call(
        functools.partial(_nsa_prompt_kernel, n_blk=n_blk),
        grid=(n_batch, g_, n_blk),
        in_specs=[
            pl.BlockSpec((NSA_BLOCK, wq), lambda b, g, c: (b * n_blk + c, g)),
            pl.BlockSpec((1, NSA_BLOCK, LANES), lambda b, g, c: (g, b * n_blk + c, 0)),
            pl.BlockSpec((1, n_blk, LANES), lambda b, g, c: (g, b, 0)),
            pl.BlockSpec((1, seq, LANES), lambda b, g, c: (g_ + g, b, 0)),
            pl.BlockSpec((1, seq, LANES), lambda b, g, c: (2 * g_ + g, b, 0)),
            pl.BlockSpec((1,) + tiles.shape[1:], lambda b, g, c: (g, 0, 0, 0)),
            pl.BlockSpec((1,) + cmpq.shape[1:], lambda b, g, c: (g, 0, 0, 0)),
            pl.BlockSpec((1,) + cmpt.shape[1:], lambda b, g, c: (g, 0, 0)),
        ],
        out_specs=pl.BlockSpec((NSA_BLOCK, wq), lambda b, g, c: (b * n_blk + c, g)),
        out_shape=jax.ShapeDtypeStruct((m, NSA_HEADS * HEAD_DIM), f32),
        compiler_params=_cparams("parallel", "parallel", "arbitrary"),
        name="nsa_prompt_attn",
    )(q, gates, kvc, kvp, kvp, tiles, cmpq, cmpt)


def _group_lane_mask(shape):
    r = lax.broadcasted_iota(i32, shape, 0)
    ln = lax.broadcasted_iota(i32, shape, 1)
    return (ln >> 6) == (r & (NSA_KV_HEADS - 1))


def _fold_groups(x):
    x = jnp.where(_group_lane_mask(x.shape), x, 0.0)
    out = x[:, 0:LANES]
    for pr in range(1, x.shape[1] // LANES):
        out = out + x[:, pr * LANES:(pr + 1) * LANES]
    return out + pltpu.roll(out, HEAD_DIM, 1)


def _nsa_sample_kernel(pt_ref, page_ref, win_ref, q_ref, kvn_ref, gt_ref, wc_ref, bkey_ref, bblk_ref,
                       o_ref, nwin_ref, kcv_scr, pages_scr, *, n_pages, page, past_len):
    del pt_ref
    b = pl.program_id(0)
    p = pl.program_id(1)
    gd = NSA_KV_HEADS * HEAD_DIM
    n_past_blk = past_len // NSA_BLOCK
    n_blk_pad = kcv_scr.shape[0]

    pages_scr[pl.ds(pl.multiple_of(p * page, page), page), :] = page_ref[0, 0]

    @pl.when(p == n_pages - 1)
    def _():
        scale = HEAD_DIM ** -0.5
        kvn = kvn_ref[pl.ds(b, 1), :]
        qm = jnp.where(_group_lane_mask((NSA_HEADS, gd)), q_ref[0] * scale, 0.0)
        qm_b = qm.astype(bf16)

        cmp_part = pages_scr[:, 0:2 * gd].reshape(n_past_blk, NSA_BLOCK, 2 * gd)
        kcv_scr[0:n_past_blk, :] = jnp.sum(cmp_part * wc_ref[...][None], axis=1)
        row8 = lax.broadcasted_iota(i32, (8, 2 * gd), 0)
        kcv_scr[n_past_blk:n_past_blk + 8, :] = jnp.where(row8 == 0, kvn[:, 0:2 * gd] * wc_ref[0:1, :], 0.0)
        rest = n_blk_pad - n_past_blk - 8
        kcv_scr[n_past_blk + 8:n_blk_pad, :] = jnp.zeros((rest, 2 * gd), f32)
        kcv = kcv_scr[...]
        kc_b = kcv[:, 0:gd].astype(bf16)
        vc_b = kcv[:, gd:2 * gd].astype(bf16)
        cur = n_past_blk

        blk = lax.broadcasted_iota(i32, (NSA_HEADS, n_blk_pad), 1)
        mask_c = blk * NSA_BLOCK + (NSA_BLOCK - 1) <= past_len
        s_c = jnp.where(mask_c, _dot_nt(qm_b, kc_b) + bblk_ref[...], NEG_INF)
        m_c = jnp.max(s_c, axis=1, keepdims=True)
        e_c = jnp.where(mask_c, jnp.exp(s_c - m_c), 0.0)
        l_c = jnp.sum(e_c, axis=1, keepdims=True)
        p_c = e_c / jnp.where(l_c > 0.0, l_c, 1.0)
        o_c = _fold_groups(_dot(p_c.astype(bf16), vc_b))

        imp = p_c[0:NSA_KV_HEADS]
        for hh in range(1, NSA_HPG):
            imp = imp + p_c[hh * NSA_KV_HEADS:(hh + 1) * NSA_KV_HEADS]

        blk_g = lax.broadcasted_iota(i32, imp.shape, 1)
        valid = blk_g <= cur
        forced = (blk_g == 0) | (blk_g == cur) | (blk_g == cur - 1)
        n_forced = 1 + int(cur >= 1) + int(cur >= 2)
        if cur + 1 <= NSA_TOPK:
            sel = valid.astype(f32)
        else:
            blk_f = blk_g.astype(f32)
            work = jnp.where(valid & jnp.logical_not(forced), imp, -1.0)
            sel = forced.astype(f32)
            for _ in range(NSA_TOPK - n_forced):
                mx = jnp.max(work, axis=1, keepdims=True)
                first = jnp.min(jnp.where(work == mx, blk_f, float(n_blk_pad)), axis=1, keepdims=True)
                pick = blk_f == first
                work = jnp.where(pick, -2.0, work)
                sel = jnp.where(pick, 1.0, sel)
        sel16 = jnp.concatenate([sel] * NSA_HPG, axis=0).astype(bf16)

        k_new = kvn[:, 2 * gd:3 * gd]
        v_new = kvn[:, 3 * gd:4 * gd]
        ks_b = pages_scr[:, 2 * gd:3 * gd].astype(bf16)
        vs_b = pages_scr[:, 3 * gd:4 * gd].astype(bf16)
        e_mat = (lax.broadcasted_iota(i32, (n_blk_pad, past_len), 0)
                 == (lax.broadcasted_iota(i32, (n_blk_pad, past_len), 1) >> 6)).astype(bf16)
        msk_s = _dot(sel16, e_mat) > 0.5
        b_self = bkey_ref[:, past_len:past_len + LANES][:, 0:1]
        s_s = jnp.where(msk_s, _dot_nt(qm_b, ks_b) + bkey_ref[:, 0:past_len], NEG_INF)
        s_self = jnp.sum(qm * k_new, axis=1, keepdims=True) + b_self
        m_s = jnp.maximum(jnp.max(s_s, axis=1, keepdims=True), s_self)
        e_s = jnp.where(msk_s, jnp.exp(s_s - m_s), 0.0)
        e_self = jnp.exp(s_self - m_s)
        l_s = jnp.sum(e_s, axis=1, keepdims=True) + e_self
        o_s = _fold_groups(_dot(e_s.astype(bf16), vs_b) + e_self * v_new) / l_s

        n_buf = win_ref.shape[2]
        kw_b = win_ref[0, 0, :, 0:gd].astype(bf16)
        vw_b = win_ref[0, 0, :, gd:2 * gd].astype(bf16)
        kw_new = kvn[:, 4 * gd:5 * gd]
        vw_new = kvn[:, 5 * gd:6 * gd]
        wpos = past_len - n_buf + lax.broadcasted_iota(i32, (NSA_HEADS, n_buf), 1)
        rel_w = past_len - wpos
        msk_w = (rel_w <= NSA_WINDOW) & (wpos >= 0)
        s_w = jnp.where(msk_w, _dot_nt(qm_b, kw_b) + bkey_ref[:, past_len - n_buf:past_len], NEG_INF)
        sw_self = jnp.sum(qm * kw_new, axis=1, keepdims=True) + b_self
        m_w = jnp.maximum(jnp.max(s_w, axis=1, keepdims=True), sw_self)
        e_w = jnp.where(msk_w, jnp.exp(s_w - m_w), 0.0)
        ew_self = jnp.exp(sw_self - m_w)
        l_w = jnp.sum(e_w, axis=1, keepdims=True) + ew_self
        o_w = _fold_groups(_dot(e_w.astype(bf16), vw_b) + ew_self * vw_new) / l_w

        gt = gt_ref[0]
        lane_g = lax.broadcasted_iota(i32, gt.shape, 1)

        def gate(branch):
            return jnp.sum(jnp.where(lane_g == branch, gt, 0.0), axis=1, keepdims=True)

        o_ref[0] = gate(0) * o_c + gate(1) * o_s + gate(2) * o_w

        nwin_ref[0, 0, 0:n_buf - 1, :] = win_ref[0, 0, 1:n_buf, :]
        nwin_ref[0, 0, n_buf - 1:n_buf, :] = kvn[:, 4 * gd:6 * gd]


def _nsa_sample_attn(page_table, cache_kv, cache_win, layer, q_hg, kv_nat, gates_hg, wc_nat, bkey, bblk,
                     past_len):
    n_dec, n_pages = page_table.shape
    _, n_phys, page, _, g_, d_ = cache_kv.shape
    gd = g_ * d_
    n_buf = cache_win.shape[2]
    n_blk_pad = bblk.shape[1]
    cache2 = cache_kv.reshape(cache_kv.shape[0], n_phys, page, 4 * gd)
    win2 = cache_win.reshape(cache_win.shape[0], n_dec, n_buf, 2 * gd)
    grid_spec = pltpu.PrefetchScalarGridSpec(
        num_scalar_prefetch=1,
        grid=(n_dec, n_pages),
        in_specs=[
            pl.BlockSpec((1, 1, page, 4 * gd), lambda b, p, pt: (layer, pt[b * n_pages + p], 0, 0)),
            pl.BlockSpec((1, 1, n_buf, 2 * gd), lambda b, p, pt: (layer, b, 0, 0)),
            pl.BlockSpec((1, NSA_HEADS, gd), lambda b, p, pt: (b, 0, 0)),
            pl.BlockSpec(kv_nat.shape, lambda b, p, pt: (0, 0)),
            pl.BlockSpec((1, NSA_HEADS, LANES), lambda b, p, pt: (b, 0, 0)),
            pl.BlockSpec(wc_nat.shape, lambda b, p, pt: (0, 0)),
            pl.BlockSpec(bkey.shape, lambda b, p, pt: (0, 0)),
            pl.BlockSpec(bblk.shape, lambda b, p, pt: (0, 0)),
        ],
        out_specs=[
            pl.BlockSpec((1, NSA_HEADS, LANES), lambda b, p, pt: (b, 0, 0)),
            pl.BlockSpec((1, 1, n_buf, 2 * gd), lambda b, p, pt: (0, b, 0, 0)),
        ],
        scratch_shapes=[pltpu.VMEM((n_blk_pad, 2 * gd), f32), pltpu.VMEM((past_len, 4 * gd), f32)],
    )
    o, nwin = pl.pallas_call(
        functools.partial(_nsa_sample_kernel, n_pages=n_pages, page=page, past_len=past_len),
        grid_spec=grid_spec,
        out_shape=[jax.ShapeDtypeStruct((n_dec, NSA_HEADS, LANES), f32),
                   jax.ShapeDtypeStruct((1, n_dec, n_buf, 2 * gd), f32)],
        compiler_params=_cparams("parallel", "arbitrary"),
        name="nsa_sample_attn",
    )(page_table.reshape(-1), cache2, win2, q_hg, kv_nat, gates_hg, wc_nat, bkey, bblk)
    return o[:, :, :HEAD_DIM], nwin[0]


def _conv_prompt_kernel(x_ref, g_ref, win_ref, cw_ref, wout_ref, o_ref, st_ref, carry_ref, *, n_t):
    t = pl.program_id(1)
    tm, d = x_ref.shape

    @pl.when(t == 0)
    def _():
        carry_ref[...] = jnp.zeros_like(carry_ref)

    x = x_ref[...]
    xn = _rms(x, g_ref[...]).astype(bf16)
    pr = _dot(xn, win_ref[...])
    u = pr[:, d:2 * d] * pr[:, 0:d]
    bg = pr[:, 2 * d:3 * d]
    prev = carry_ref[...]
    row = lax.broadcasted_iota(i32, (tm, d), 0)
    u1 = jnp.where(row == 0, prev[7:8, :], pltpu.roll(u, 1, 0))
    u2 = jnp.where(row == 0, prev[6:7, :], jnp.where(row == 1, prev[7:8, :], pltpu.roll(u, 2, 0)))
    cw = cw_ref[...]
    v = cw[0:1, :] * u2 + cw[1:2, :] * u1 + cw[2:3, :] * u
    o_ref[...] = x + _dot((bg * v).astype(bf16), wout_ref[...])
    carry_ref[...] = u[tm - 8:tm, :]

    @pl.when(t == n_t - 1)
    def _():
        st_ref[0] = u[tm - (CONV_W - 1):tm, :]


def _conv_prompt(x, g, w_in, cw, w_out, n_batch, seq, tm_target=512):
    m, d = x.shape
    tm = _row_tile(seq, tm_target)
    n_t = seq // tm
    return pl.pallas_call(
        functools.partial(_conv_prompt_kernel, n_t=n_t),
        grid=(n_batch, n_t),
        in_specs=[pl.BlockSpec((tm, d), lambda b, t: (b * n_t + t, 0)),
                  pl.BlockSpec((1, d), lambda b, t: (0, 0)),
                  pl.BlockSpec((d, 3 * d), lambda b, t: (0, 0)),
                  pl.BlockSpec((CONV_W, d), lambda b, t: (0, 0)),
                  pl.BlockSpec((d, d), lambda b, t: (0, 0))],
        out_specs=[pl.BlockSpec((tm, d), lambda b, t: (b * n_t + t, 0)),
                   pl.BlockSpec((1, CONV_W - 1, d), lambda b, t: (b, 0, 0))],
        out_shape=[jax.ShapeDtypeStruct((m, d), f32),
                   jax.ShapeDtypeStruct((n_batch, CONV_W - 1, d), f32)],
        scratch_shapes=[pltpu.VMEM((8, d), f32)],
        compiler_params=_cparams("parallel", "arbitrary"),
        name="conv_prompt",
    )(x, g.reshape(1, d), w_in, cw, w_out)


def _conv_sample_kernel(x_ref, g_ref, st_ref, win_ref, cw_ref, wout_ref, o_ref, nst_ref):
    d = x_ref.shape[1]
    x = x_ref[...]
    xn = _rms(x, g_ref[...]).astype(bf16)
    pr = _dot(xn, win_ref[...])
    u = pr[:, d:2 * d] * pr[:, 0:d]
    bg = pr[:, 2 * d:3 * d]
    st = st_ref[...]
    cw = cw_ref[...]
    v = cw[0:1, :] * st[:, 0:d] + cw[1:2, :] * st[:, d:2 * d] + cw[2:3, :] * u
    o_ref[...] = x + _dot((bg * v).astype(bf16), wout_ref[...])
    nst_ref[:, 0:d] = st[:, d:2 * d]
    nst_ref[:, d:2 * d] = u


def _conv_sample(x, g, state, w_in, cw, w_out):
    m, d = x.shape
    o, nst = pl.pallas_call(
        _conv_sample_kernel,
        out_shape=[jax.ShapeDtypeStruct((m, d), f32),
                   jax.ShapeDtypeStruct((m, (CONV_W - 1) * d), f32)],
        compiler_params=pltpu.CompilerParams(vmem_limit_bytes=VMEM_LIMIT_BYTES),
        name="conv_sample",
    )(x, g.reshape(1, d), state.reshape(m, (CONV_W - 1) * d), w_in, cw, w_out)
    return o, nst.reshape(m, CONV_W - 1, d)


def _pool_mix(win_sum_groups, xn, cnt_groups, wgrp_ref, scale):
    gw = xn.shape[1] // len(POOL_WINDOWS)
    outs = []
    for gi in range(len(POOL_WINDOWS)):
        diff = win_sum_groups[gi] / cnt_groups[gi] - xn[:, gi * gw:(gi + 1) * gw]
        outs.append(_dot(diff.astype(bf16), wgrp_ref[gi]))
    return jnp.concatenate(outs, axis=1) * scale


def _pool_prompt_kernel(x_ref, g_ref, wgrp_ref, sc_ref, o_ref, st_ref, carry_ref, *, n_t):
    t = pl.program_id(1)
    tm, d = x_ref.shape
    gw = d // len(POOL_WINDOWS)
    hist = carry_ref.shape[0]

    @pl.when(t == 0)
    def _():
        carry_ref[...] = jnp.zeros_like(carry_ref)

    x = x_ref[...]
    xn = _rms(x, g_ref[...])
    a = jnp.concatenate([carry_ref[...], xn], axis=0)
    sums = []
    s = a
    shift = 1
    for gi, w in enumerate(POOL_WINDOWS):
        while shift < w:
            s = s + pltpu.roll(s, shift, 0)
            shift *= 2
        sums.append(s[hist:, gi * gw:(gi + 1) * gw])
    pos = t * tm + lax.broadcasted_iota(i32, (tm, gw), 0)
    cnts = [jnp.minimum(pos + 1, w).astype(f32) for w in POOL_WINDOWS]
    o_ref[...] = x + _pool_mix(sums, xn, cnts, wgrp_ref, sc_ref[...])
    carry_ref[...] = xn[tm - hist:tm, :]

    @pl.when(t == n_t - 1)
    def _():
        st_ref[0] = xn[tm - POOL_HIST:tm, :]


def _pool_prompt(x, g, w_grp, scale, n_batch, seq, tm_target=512):
    m, d = x.shape
    tm = _row_tile(seq, tm_target)
    n_t = seq // tm
    return pl.pallas_call(
        functools.partial(_pool_prompt_kernel, n_t=n_t),
        grid=(n_batch, n_t),
        in_specs=[pl.BlockSpec((tm, d), lambda b, t: (b * n_t + t, 0)),
                  pl.BlockSpec((1, d), lambda b, t: (0, 0)),
                  pl.BlockSpec(w_grp.shape, lambda b, t: (0, 0, 0)),
                  pl.BlockSpec((1, d), lambda b, t: (0, 0))],
        out_specs=[pl.BlockSpec((tm, d), lambda b, t: (b * n_t + t, 0)),
                   pl.BlockSpec((1, POOL_HIST, d), lambda b, t: (b, 0, 0))],
        out_shape=[jax.ShapeDtypeStruct((m, d), f32),
                   jax.ShapeDtypeStruct((n_batch, POOL_HIST, d), f32)],
        scratch_shapes=[pltpu.VMEM((POOL_HIST + 1, d), f32)],
        compiler_params=_cparams("parallel", "arbitrary"),
        name="pool_prompt",
    )(x, g.reshape(1, d), w_grp, scale.reshape(1, d))


def _pool_sample_kernel(x_ref, g_ref, st_ref, wgrp_ref, sc_ref, o_ref, nst_ref, *, pos):
    m, d = x_ref.shape
    gw = d // len(POOL_WINDOWS)
    x = x_ref[...]
    xn = _rms(x, g_ref[...])
    st = st_ref[...]
    sums, cnts = [], []
    s = xn
    back = 1
    for gi, w in enumerate(POOL_WINDOWS):
        while back < w:
            s = s + st[:, (POOL_HIST - back) * d:(POOL_HIST - back + 1) * d]
            back += 1
        sums.append(s[:, gi * gw:(gi + 1) * gw])
        cnts.append(jnp.full((m, gw), float(min(pos + 1, w)), f32))
    o_ref[...] = x + _pool_mix(sums, xn, cnts, wgrp_ref, sc_ref[...])
    nst_ref[:, 0:(POOL_HIST - 1) * d] = st[:, d:POOL_HIST * d]
    nst_ref[:, (POOL_HIST - 1) * d:POOL_HIST * d] = xn


def _pool_sample(x, g, state, w_grp, scale, pos):
    m, d = x.shape
    o, nst = pl.pallas_call(
        functools.partial(_pool_sample_kernel, pos=pos),
        out_shape=[jax.ShapeDtypeStruct((m, d), f32),
                   jax.ShapeDtypeStruct((m, POOL_HIST * d), f32)],
        compiler_params=pltpu.CompilerParams(vmem_limit_bytes=VMEM_LIMIT_BYTES),
        name="pool_sample",
    )(x, g.reshape(1, d), state.reshape(m, POOL_HIST * d), w_grp, scale.reshape(1, d))
    return o, nst.reshape(m, POOL_HIST, d)


def _ca_prompt_kernel(x_ref, g_ref, wq_ref, kv_ref, wo_ref, o_ref):
    d = x_ref.shape[1]
    hd = d // CA_HEADS
    x = x_ref[...]
    xn = _rms(x, g_ref[...]).astype(bf16)
    q = (_dot(xn, wq_ref[...]) * (hd ** -0.5)).astype(bf16)
    outs = []
    for h in range(CA_HEADS):
        k = kv_ref[:, h * hd:(h + 1) * hd].astype(bf16)
        v = kv_ref[:, d + h * hd:d + (h + 1) * hd].astype(bf16)
        s = _dot_nt(q[:, h * hd:(h + 1) * hd], k)
        e = jnp.exp(s - jnp.max(s, axis=1, keepdims=True))
        p = e / jnp.sum(e, axis=1, keepdims=True)
        outs.append(_dot(p.astype(bf16), v))
    o = jnp.concatenate(outs, axis=1).astype(bf16)
    o_ref[...] = x + _dot(o, wo_ref[...])


def _ca_prompt(x, g, wq, mkv, wo, n_batch, seq, mem_len, tm_target=512):
    m, d = x.shape
    tm = _row_tile(seq, tm_target)
    n_t = seq // tm
    return pl.pallas_call(
        _ca_prompt_kernel,
        grid=(n_batch, n_t),
        in_specs=[pl.BlockSpec((tm, d), lambda b, t: (b * n_t + t, 0)),
                  pl.BlockSpec((1, d), lambda b, t: (0, 0)),
                  pl.BlockSpec((d, d), lambda b, t: (0, 0)),
                  pl.BlockSpec((mem_len, 2 * d), lambda b, t: (b, 0)),
                  pl.BlockSpec((d, d), lambda b, t: (0, 0))],
        out_specs=pl.BlockSpec((tm, d), lambda b, t: (b * n_t + t, 0)),
        out_shape=jax.ShapeDtypeStruct((m, d), f32),
        compiler_params=_cparams("parallel", "parallel"),
        name="ca_prompt",
    )(x, g.reshape(1, d), wq, mkv, wo)


def _ca_sample_kernel(q_ref, kv_ref, o_ref):
    bb = q_ref.shape[0]
    d = q_ref.shape[2]
    hd = d // CA_HEADS
    scale = hd ** -0.5
    for bi in range(bb):
        q = q_ref[bi] * scale
        outs = []
        for h in range(CA_HEADS):
            k = kv_ref[bi, :, h * hd:(h + 1) * hd]
            v = kv_ref[bi, :, d + h * hd:d + (h + 1) * hd]
            s = jnp.sum(k * q[:, h * hd:(h + 1) * hd], axis=1, keepdims=True)
            e = jnp.exp(s - jnp.max(s, axis=0, keepdims=True))
            p = e / jnp.sum(e, axis=0, keepdims=True)
            outs.append(jnp.sum(p * v, axis=0, keepdims=True))
        o_ref[bi] = jnp.concatenate(outs, axis=1)


def _ca_sample(q, cache_kv_layer, bb=4):
    n_dec, d = q.shape
    mem = cache_kv_layer.shape[1]
    o = pl.pallas_call(
        _ca_sample_kernel,
        grid=(n_dec // bb,),
        in_specs=[pl.BlockSpec((bb, 1, d), lambda i: (i, 0, 0)),
                  pl.BlockSpec((bb, mem, 2 * d), lambda i: (i, 0, 0))],
        out_specs=pl.BlockSpec((bb, 1, d), lambda i: (i, 0, 0)),
        out_shape=jax.ShapeDtypeStruct((n_dec, 1, d), f32),
        compiler_params=_cparams("parallel"),
        name="ca_sample",
    )(q.reshape(n_dec, 1, d), cache_kv_layer)
    return o.reshape(n_dec, d)


def _nsa_weights(w_in, w_cmp):
    d_model = w_in.shape[0]
    nq = NSA_HEADS * HEAD_DIM
    nkv = 6 * NSA_KV_HEADS * HEAD_DIM
    wq = w_in[:, :nq].astype(bf16)
    wkv = w_in[:, nq:nq + nkv].reshape(d_model, 3, 2, NSA_KV_HEADS, HEAD_DIM)
    wkv = wkv.transpose(0, 1, 3, 2, 4).reshape(d_model, nkv).astype(bf16)
    wg = w_in[:, nq + nkv:].reshape(d_model, NSA_KV_HEADS, NSA_HPG * 3)
    wg = jnp.pad(wg, ((0, 0), (0, 0), (0, LANES - NSA_HPG * 3))).reshape(d_model, NSA_KV_HEADS * LANES)
    w_pack = jnp.concatenate([w_cmp[0], w_cmp[1]], axis=1)
    wc_nat = jnp.concatenate([jnp.tile(w_cmp[0], (1, NSA_KV_HEADS)),
                              jnp.tile(w_cmp[1], (1, NSA_KV_HEADS))], axis=1)
    return wq, wkv, wg.astype(bf16), w_pack, wc_nat


def _kv_natural(kvp):
    m = kvp.shape[1]
    x = kvp.reshape(3, NSA_KV_HEADS, m, 2, HEAD_DIM)
    return x.transpose(2, 0, 3, 1, 4).reshape(m, 6 * NSA_KV_HEADS * HEAD_DIM)


def kernel(x_prompt, x_sample, cache_nsa_kv, cache_nsa_win, state_conv, state_pool, cache_mem_kv,
           page_table, mem_prompt, rel_bias, norm_g, norm_mem_g, final_g, w_ffn_in, w_ffn_out,
           w_nsa_in, w_nsa_cmp, w_nsa_out, w_conv_in, conv_w, w_conv_out, w_pool, pool_scale,
           w_ca_q, w_ca_kv, w_ca_out):
    n_b, seq, d = x_prompt.shape
    n_dec = x_sample.shape[0]
    depth = norm_g.shape[0]
    mem_len = mem_prompt.shape[1]
    page = cache_nsa_kv.shape[2]
    past_len = page_table.shape[1] * page
    g_, hd = NSA_KV_HEADS, HEAD_DIM
    gd = g_ * hd

    xp = x_prompt.reshape(n_b * seq, d)
    xs = x_sample.reshape(n_dec, d)
    mem = mem_prompt.reshape(n_b * mem_len, d)

    tiles, cmpq, cmpt = _t5_prompt_tables(rel_bias, seq // NSA_BLOCK)
    n_blk_pad = -(-(past_len // NSA_BLOCK + 8) // LANES) * LANES
    bkey, bblk = _t5_sample_tables(rel_bias, past_len, past_len + LANES, n_blk_pad)

    nsa_kv_p, nsa_win_p, conv_p, pool_p, mem_p = [], [], [], [], []
    nsa_kv_s, nsa_win_s, conv_s, pool_s = [], [], [], []
    for i in range(depth):
        kind, j = i % 3, i // 3
        wf_in = w_ffn_in[i].astype(bf16)
        wf_out = w_ffn_out[i].astype(bf16)
        xp = _ffn(xp, norm_g[i, 0], wf_in[0], wf_out[0])
        xs = _ffn(xs, norm_g[i, 0], wf_in[0], wf_out[0])
        if kind == 0:
            wq, wkv, wg, w_pack, wc_nat = _nsa_weights(w_nsa_in[j], w_nsa_cmp[j])
            w_out = w_nsa_out[j].astype(bf16)
            q, kvp, gates = _nsa_proj(xp, norm_g[i, 1], wq, wkv, wg)
            kvc = _compress(kvp, w_pack, g_)
            o = _nsa_prompt_attn(q, gates, kvc, kvp, tiles, cmpq, cmpt, n_b, seq)
            xp = _proj_res(o, w_out, xp)
            kv_nat = _kv_natural(kvp)
            nsa_kv_p.append(kv_nat[:, :4 * gd].reshape(n_b * seq // page, page, 4, g_, hd))
            n_keep = min(NSA_WINDOW, seq)
            nsa_win_p.append(kv_nat.reshape(n_b, seq, 6 * gd)[:, seq - n_keep:, 4 * gd:]
                             .reshape(n_b, n_keep, 2, g_, hd))
            q_s, kvp_s, gates_s = _nsa_proj(xs, norm_g[i, 1], wq, wkv, wg)
            kv_nat_s = _kv_natural(kvp_s)
            q_hg = q_s.reshape(n_dec, g_, NSA_HPG, hd).transpose(0, 2, 1, 3).reshape(n_dec, NSA_HEADS, hd)
            q_hg = jnp.tile(q_hg, (1, 1, g_))
            gates_hg = gates_s.transpose(1, 0, 2)[:, :, :NSA_HPG * 3].reshape(n_dec, g_, NSA_HPG, 3)
            gates_hg = jnp.pad(gates_hg.transpose(0, 2, 1, 3).reshape(n_dec, NSA_HEADS, 3),
                               ((0, 0), (0, 0), (0, LANES - 3)))
            o_s, nwin = _nsa_sample_attn(page_table, cache_nsa_kv, cache_nsa_win, j, q_hg, kv_nat_s,
                                         gates_hg, wc_nat, bkey, bblk, past_len)
            o_s = o_s.reshape(n_dec, NSA_HPG, g_, hd).transpose(0, 2, 1, 3).reshape(n_dec, NSA_HEADS * hd)
            xs = _proj_res(o_s, w_out, xs)
            nsa_kv_s.append(kv_nat_s[:, :4 * gd].reshape(n_dec, 1, 4, g_, hd))
            nsa_win_s.append(nwin.reshape(n_dec, -1, 2, g_, hd))
        elif kind == 1:
            wc_in = w_conv_in[j].astype(bf16)
            wc_out = w_conv_out[j].astype(bf16)
            xp, cst = _conv_prompt(xp, norm_g[i, 1], wc_in, conv_w[j], wc_out, n_b, seq)
            xs, csts = _conv_sample(xs, norm_g[i, 1], state_conv[j], wc_in, conv_w[j], wc_out)
            conv_p.append(cst)
            conv_s.append(csts)
        else:
            wp = w_pool[j].astype(bf16)
            xp, pst = _pool_prompt(xp, norm_g[i, 1], wp, pool_scale[j], n_b, seq)
            xs, psts = _pool_sample(xs, norm_g[i, 1], state_pool[j], wp, pool_scale[j], past_len)
            pool_p.append(pst)
            pool_s.append(psts)
        w_q = w_ca_q[i].astype(bf16)
        w_o = w_ca_out[i].astype(bf16)
        mkv = _normproj(mem, norm_mem_g[i], w_ca_kv[i].astype(bf16))
        mem_p.append(mkv.reshape(n_b, mem_len, 2, CA_HEADS, d // CA_HEADS))
        xp = _ca_prompt(xp, norm_g[i, 2], w_q, mkv, w_o, n_b, seq, mem_len)
        q_ca = _normproj(xs, norm_g[i, 2], w_q)
        o_ca = _ca_sample(q_ca, cache_mem_kv[i].reshape(n_dec, mem_len, 2 * d))
        xs = _proj_res(o_ca, w_o, xs)
        fg = final_g if i == depth - 1 else None
        xp = _ffn(xp, norm_g[i, 3], wf_in[1], wf_out[1], final_g=fg)
        xs = _ffn(xs, norm_g[i, 3], wf_in[1], wf_out[1], final_g=fg)

    return (xp.reshape(n_b, seq, d), xs.reshape(n_dec, 1, d),
            jnp.stack(nsa_kv_p), jnp.stack(nsa_win_p), jnp.stack(conv_p), jnp.stack(pool_p),
            jnp.stack(mem_p),
            jnp.stack(nsa_kv_s), jnp.stack(nsa_win_s), jnp.stack(conv_s), jnp.stack(pool_s))
```

```python
import functools
import math

import jax
import jax.numpy as jnp
from jax import lax
from jax.experimental import pallas as pl
from jax.experimental.pallas import tpu as pltpu

f32 = jnp.float32
bf16 = jnp.bfloat16
i32 = jnp.int32

NORM_EPS = 1e-6
NEG_INF = -1e30

NSA_HEADS = 16
NSA_KV_HEADS = 4
NSA_HPG = NSA_HEADS // NSA_KV_HEADS
HEAD_DIM = 64
NSA_BLOCK = 64
NSA_TOPK = 16
NSA_WINDOW = 512
REL_BUCKETS = 32
REL_MAX_DIST = 128
CONV_W = 3
POOL_WINDOWS = (2, 4, 8, 16)
POOL_HIST = max(POOL_WINDOWS) - 1
CA_HEADS = 4

LANES = 128
VMEM_LIMIT_BYTES = 56 * 1024 * 1024

N_NEAR = 3
SEL_CHUNK_BLOCKS = 8
SEL_CHUNK = SEL_CHUNK_BLOCKS * NSA_BLOCK
WIN_TILE = 640
KV_PAD = WIN_TILE


def _cparams(*sem):
    return pltpu.CompilerParams(dimension_semantics=sem, vmem_limit_bytes=VMEM_LIMIT_BYTES)


def _rms(x, g):
    return x * lax.rsqrt(jnp.mean(x * x, axis=-1, keepdims=True) + NORM_EPS) * g


def _dot(a, b):
    return jnp.dot(a, b, preferred_element_type=f32)


def _dot_nt(a, b):
    return lax.dot_general(a, b, (((1,), (1,)), ((), ())), preferred_element_type=f32)


def _row_tile(m, target):
    t = min(m, target)
    while m % t:
        t //= 2
    return t


def _ffn_kernel(x_ref, g_ref, wg_ref, wu_ref, wo_ref, *rest, n_f, final_norm):
    if final_norm:
        gf_ref, o_ref, xn_ref, acc_ref = rest
    else:
        o_ref, xn_ref, acc_ref = rest
    j = pl.program_id(1)

    @pl.when(j == 0)
    def _():
        xn_ref[...] = _rms(x_ref[...], g_ref[...]).astype(bf16)
        acc_ref[...] = jnp.zeros_like(acc_ref)

    xn = xn_ref[...]
    gate = _dot(xn, wg_ref[...])
    up = _dot(xn, wu_ref[...])
    act = (gate * jax.nn.sigmoid(gate) * up).astype(bf16)
    acc_ref[...] += _dot(act, wo_ref[...])

    @pl.when(j == n_f - 1)
    def _():
        y = x_ref[...] + 0.5 * acc_ref[...]
        if final_norm:
            y = _rms(y, gf_ref[...])
        o_ref[...] = y


def _ffn(x, g, w_in, w_out, final_g=None, tm_target=512):
    m, d = x.shape
    f = w_out.shape[0]
    tm = _row_tile(m, tm_target)
    tf = f
    for cand in (1408, 1024, 512, 256, 128):
        if f % cand == 0:
            tf = cand
            break
    n_f = f // tf
    in_specs = [
        pl.BlockSpec((tm, d), lambda i, j: (i, 0)),
        pl.BlockSpec((1, d), lambda i, j: (0, 0)),
        pl.BlockSpec((d, tf), lambda i, j: (0, j)),
        pl.BlockSpec((d, tf), lambda i, j: (0, n_f + j)),
        pl.BlockSpec((tf, d), lambda i, j: (j, 0)),
    ]
    args = [x, g.reshape(1, d), w_in, w_in, w_out]
    if final_g is not None:
        in_specs.append(pl.BlockSpec((1, d), lambda i, j: (0, 0)))
        args.append(final_g.reshape(1, d))
    return pl.pallas_call(
        functools.partial(_ffn_kernel, n_f=n_f, final_norm=final_g is not None),
        grid=(m // tm, n_f),
        in_specs=in_specs,
        out_specs=pl.BlockSpec((tm, d), lambda i, j: (i, 0)),
        out_shape=jax.ShapeDtypeStruct((m, d), f32),
        scratch_shapes=[pltpu.VMEM((tm, d), bf16), pltpu.VMEM((tm, d), f32)],
        compiler_params=_cparams("parallel", "arbitrary"),
        name="ffn",
    )(*args)


def _normproj_kernel(x_ref, g_ref, w_ref, o_ref):
    xn = _rms(x_ref[...], g_ref[...]).astype(bf16)
    o_ref[...] = _dot(xn, w_ref[...])


def _normproj(x, g, w, tm_target=512):
    m, d = x.shape
    n = w.shape[1]
    tm = _row_tile(m, tm_target)
    return pl.pallas_call(
        _normproj_kernel,
        grid=(m // tm,),
        in_specs=[pl.BlockSpec((tm, d), lambda i: (i, 0)),
                  pl.BlockSpec((1, d), lambda i: (0, 0)),
                  pl.BlockSpec((d, n), lambda i: (0, 0))],
        out_specs=pl.BlockSpec((tm, n), lambda i: (i, 0)),
        out_shape=jax.ShapeDtypeStruct((m, n), f32),
        compiler_params=_cparams("parallel"),
        name="normproj",
    )(x, g.reshape(1, d), w)


def _proj_res_kernel(a_ref, w_ref, x_ref, o_ref):
    o_ref[...] = x_ref[...] + _dot(a_ref[...].astype(bf16), w_ref[...])


def _proj_res(a, w, x, tm_target=512):
    m, k = a.shape
    d = w.shape[1]
    tm = _row_tile(m, tm_target)
    return pl.pallas_call(
        _proj_res_kernel,
        grid=(m // tm,),
        in_specs=[pl.BlockSpec((tm, k), lambda i: (i, 0)),
                  pl.BlockSpec((k, d), lambda i: (0, 0)),
                  pl.BlockSpec((tm, d), lambda i: (i, 0))],
        out_specs=pl.BlockSpec((tm, d), lambda i: (i, 0)),
        out_shape=jax.ShapeDtypeStruct((m, d), f32),
        compiler_params=_cparams("parallel"),
        name="proj_res",
    )(a, w, x)


def _nsa_proj_kernel(x_ref, g_ref, wq_ref, wkv_ref, wg_ref, q_ref, kv_ref, gt_ref):
    xn = _rms(x_ref[...], g_ref[...]).astype(bf16)
    q_ref[...] = _dot(xn, wq_ref[...])
    kv = _dot(xn, wkv_ref[...])
    for s in range(kv_ref.shape[0]):
        kv_ref[s] = kv[:, s * LANES:(s + 1) * LANES]
    gt = jax.nn.sigmoid(_dot(xn, wg_ref[...]))
    for s in range(gt_ref.shape[0]):
        gt_ref[s] = gt[:, s * LANES:(s + 1) * LANES]


def _nsa_proj(x, g, wq, wkv, wg, tm_target=512):
    m, d = x.shape
    nq, nkv, ng = wq.shape[1], wkv.shape[1], wg.shape[1]
    n_slab, n_gs = nkv // LANES, ng // LANES
    tm = _row_tile(m, tm_target)
    return pl.pallas_call(
        _nsa_proj_kernel,
        grid=(m // tm,),
        in_specs=[pl.BlockSpec((tm, d), lambda i: (i, 0)),
                  pl.BlockSpec((1, d), lambda i: (0, 0)),
                  pl.BlockSpec((d, nq), lambda i: (0, 0)),
                  pl.BlockSpec((d, nkv), lambda i: (0, 0)),
                  pl.BlockSpec((d, ng), lambda i: (0, 0))],
        out_specs=[pl.BlockSpec((tm, nq), lambda i: (i, 0)),
                   pl.BlockSpec((n_slab, tm, LANES), lambda i: (0, i, 0)),
                   pl.BlockSpec((n_gs, tm, LANES), lambda i: (0, i, 0))],
        out_shape=[jax.ShapeDtypeStruct((m, nq), f32),
                   jax.ShapeDtypeStruct((n_slab, m, LANES), f32),
                   jax.ShapeDtypeStruct((n_gs, m, LANES), f32)],
        compiler_params=_cparams("parallel"),
        name="nsa_proj",
    )(x, g.reshape(1, d), wq, wkv, wg)


def _compress_kernel(kv_ref, w_ref, o_ref):
    tm = kv_ref.shape[1]
    nb = tm // NSA_BLOCK
    x = kv_ref[0].reshape(nb, NSA_BLOCK, LANES)
    o_ref[0] = jnp.sum(x * w_ref[...][None], axis=1)


def _compress(kvp, w_pack, n_groups, tm_target=512):
    m = kvp.shape[1]
    tm = _row_tile(m, tm_target)
    nb = tm // NSA_BLOCK
    return pl.pallas_call(
        _compress_kernel,
        grid=(n_groups, m // tm),
        in_specs=[pl.BlockSpec((1, tm, LANES), lambda g, i: (g, i, 0)),
                  pl.BlockSpec((NSA_BLOCK, LANES), lambda g, i: (0, 0))],
        out_specs=pl.BlockSpec((1, nb, LANES), lambda g, i: (g, i, 0)),
        out_shape=jax.ShapeDtypeStruct((n_groups, m // NSA_BLOCK, LANES), f32),
        compiler_params=_cparams("parallel", "parallel"),
        name="nsa_compress",
    )(kvp, w_pack)


def _t5_bucket(rel):
    n = jnp.maximum(rel, 0)
    max_exact = REL_BUCKETS // 2
    nf = jnp.maximum(n, 1).astype(f32)
    large = max_exact + (jnp.log(nf / max_exact) / math.log(REL_MAX_DIST / max_exact)
                         * (REL_BUCKETS - max_exact)).astype(i32)
    large = jnp.minimum(large, REL_BUCKETS - 1)
    return jnp.where(n < max_exact, n, large)


def _bias_lookup(table_ref, bucket, head):
    out = jnp.zeros(bucket.shape, f32)
    for k in range(REL_BUCKETS):
        out = jnp.where(bucket == k, table_ref[k, head], out)
    return out


def _t5_prompt_kernel(table_ref, tsel_ref, twin_ref, cmpq_ref, cmpt_ref):
    g = pl.program_id(0)
    n_bc = cmpq_ref.shape[3]
    rel_s = (lax.broadcasted_iota(i32, (NSA_BLOCK, SEL_CHUNK), 0) + (SEL_CHUNK - NSA_BLOCK)
             - lax.broadcasted_iota(i32, (NSA_BLOCK, SEL_CHUNK), 1))
    rel_w = (lax.broadcasted_iota(i32, (NSA_BLOCK, WIN_TILE), 0) + (WIN_TILE - NSA_BLOCK)
             - lax.broadcasted_iota(i32, (NSA_BLOCK, WIN_TILE), 1))
    bucket_s = _t5_bucket(rel_s)
    bucket_w = _t5_bucket(rel_w)
    qi_b = lax.broadcasted_iota(i32, (NSA_BLOCK, n_bc), 0)
    shp_t = (8, NSA_BLOCK)
    bucket_t = _t5_bucket(lax.broadcasted_iota(i32, shp_t, 1) - (NSA_BLOCK - 1)
                          + NSA_BLOCK * lax.broadcasted_iota(i32, shp_t, 0))
    for hh in range(NSA_HPG):
        head = g * NSA_HPG + hh
        far = table_ref[REL_BUCKETS - 1, head]
        rows = slice(hh * NSA_BLOCK, (hh + 1) * NSA_BLOCK)
        tsel_ref[0, rows, :] = jnp.where(rel_s >= 0, _bias_lookup(table_ref, bucket_s, head) - far, NEG_INF)
        twin_ref[0, rows, :] = jnp.where((rel_w >= 0) & (rel_w <= NSA_WINDOW),
                                         _bias_lookup(table_ref, bucket_w, head) - far, NEG_INF)
        for d in range(N_NEAR):
            cmpq_ref[0, d, rows, :] = _bias_lookup(
                table_ref, _t5_bucket(qi_b - (NSA_BLOCK - 1) + NSA_BLOCK * d), head) - far
        cmpt_ref[0, :, rows] = _bias_lookup(table_ref, bucket_t, head) - far


def _t5_prompt_tables(rel_bias, n_blk):
    rows = NSA_HPG * NSA_BLOCK
    return pl.pallas_call(
        _t5_prompt_kernel,
        grid=(NSA_KV_HEADS,),
        in_specs=[pl.BlockSpec(memory_space=pltpu.SMEM)],
        out_specs=[pl.BlockSpec((1, rows, SEL_CHUNK), lambda g: (g, 0, 0)),
                   pl.BlockSpec((1, rows, WIN_TILE), lambda g: (g, 0, 0)),
                   pl.BlockSpec((1, N_NEAR, rows, n_blk), lambda g: (g, 0, 0, 0)),
                   pl.BlockSpec((1, 8, rows), lambda g: (g, 0, 0))],
        out_shape=[jax.ShapeDtypeStruct((NSA_KV_HEADS, rows, SEL_CHUNK), f32),
                   jax.ShapeDtypeStruct((NSA_KV_HEADS, rows, WIN_TILE), f32),
                   jax.ShapeDtypeStruct((NSA_KV_HEADS, N_NEAR, rows, n_blk), f32),
                   jax.ShapeDtypeStruct((NSA_KV_HEADS, 8, rows), f32)],
        compiler_params=_cparams("parallel"),
        name="t5_prompt_tables",
    )(rel_bias)


def _t5_sample_kernel(table_ref, key_ref, blk_ref, *, qpos):
    kpos = lax.broadcasted_iota(i32, (1, key_ref.shape[1]), 1)
    bucket_k = _t5_bucket(qpos - kpos)
    blk = lax.broadcasted_iota(i32, (1, blk_ref.shape[1]), 1)
    bucket_b = _t5_bucket(qpos - (blk * NSA_BLOCK + NSA_BLOCK - 1))
    for r in range(NSA_HEADS):
        head = (r % NSA_KV_HEADS) * NSA_HPG + r // NSA_KV_HEADS
        key_ref[r:r + 1, :] = _bias_lookup(table_ref, bucket_k, head)
        blk_ref[r:r + 1, :] = _bias_lookup(table_ref, bucket_b, head)


def _t5_sample_tables(rel_bias, qpos, n_key, n_blk):
    return pl.pallas_call(
        functools.partial(_t5_sample_kernel, qpos=qpos),
        in_specs=[pl.BlockSpec(memory_space=pltpu.SMEM)],
        out_shape=[jax.ShapeDtypeStruct((NSA_HEADS, n_key), f32),
                   jax.ShapeDtypeStruct((NSA_HEADS, n_blk), f32)],
        name="t5_sample_tables",
    )(rel_bias)


def _flash_update(carry, s, kv):
    m, l, acc = carry
    m_new = jnp.maximum(m, jnp.max(s, axis=1, keepdims=True))
    alpha = jnp.exp(m - m_new)
    p = jnp.exp(s - m_new)
    l = alpha * l + jnp.sum(p, axis=1, keepdims=True)
    acc = alpha * acc + _dot(p.astype(bf16), kv)
    return m_new, l, acc


def _flash_first(s, kv):
    m = jnp.max(s, axis=1, keepdims=True)
    p = jnp.exp(s - m)
    return m, jnp.sum(p, axis=1, keepdims=True), _dot(p.astype(bf16), kv)


def _flash_finish(carry):
    _, l, acc = carry
    return acc / l


def _tile_rows(x, n):
    return jnp.concatenate([x] * n, axis=0)


def _select_blocks(imp_t, c, n_blk):
    blk = lax.broadcasted_iota(i32, imp_t.shape, 0)
    valid = blk <= c
    forced = (blk == 0) | (blk == c) | (blk == c - 1)

    blk_f = blk.astype(f32)
    work = jnp.where(valid & jnp.logical_not(forced), imp_t, -1.0)
    sel = forced.astype(f32)
    for _ in range(NSA_TOPK - 3):
        mx = jnp.max(work, axis=0, keepdims=True)
        first = jnp.min(jnp.where(work == mx, blk_f, float(n_blk)), axis=0, keepdims=True)
        pick = blk_f == first
        work = jnp.where(pick, -2.0, work)
        sel = jnp.where(pick, 1.0, sel)
    return jnp.where(c + 1 <= NSA_TOPK, valid.astype(f32), sel)


def _nsa_prompt_kernel(*refs, n_blk, gps):
    q_ref, gt_ref, kvc_ref = refs[0:3]
    kvs_refs = refs[3:3 + gps]
    kvw_refs = refs[3 + gps:3 + 2 * gps]
    tsel_ref, twin_ref, cmpq_ref, cmpt_ref, o_ref = refs[3 + 2 * gps:]
    c = pl.program_id(2)
    wq = NSA_HPG * HEAD_DIM
    fronts = [_nsa_prompt_front(c, q_ref, kvc_ref, kvs_refs[i], kvw_refs[i], tsel_ref, twin_ref, cmpq_ref,
                                cmpt_ref, i, n_blk) for i in range(gps)]

    def sel_body(jj, carries):
        out = []
        for (q_pad, sel, _, _), kvs_ref, carry in zip(fronts, kvs_refs, carries):
            s, kv = _sel_scores(c, jj, q_pad, sel, kvs_ref, n_blk)
            out.append(_flash_update(carry, s, kv))
        return tuple(out)

    carries = lax.fori_loop(1, (c + SEL_CHUNK_BLOCKS) // SEL_CHUNK_BLOCKS, sel_body,
                            tuple(f[3] for f in fronts))

    lower = lax.broadcasted_iota(i32, (NSA_BLOCK, LANES), 1) < HEAD_DIM
    lane_g = lax.broadcasted_iota(i32, (NSA_BLOCK, LANES), 1)
    for i in range(gps):
        o_c, o_w = fronts[i][2]
        o_s = _flash_finish(carries[i])
        gt = gt_ref[i]

        def gate(h, branch):
            return jnp.sum(jnp.where(lane_g == h * 3 + branch, gt, 0.0), axis=1, keepdims=True)

        heads = []
        for h in range(NSA_HPG):
            r = slice(h * NSA_BLOCK, (h + 1) * NSA_BLOCK)
            heads.append(gate(h, 0) * o_c[r] + gate(h, 1) * o_s[r] + gate(h, 2) * o_w[r])
        for pr in range(NSA_HPG // 2):
            o_ref[:, i * wq + pr * LANES:i * wq + (pr + 1) * LANES] = jnp.where(
                lower, pltpu.roll(heads[2 * pr], HEAD_DIM, 1), heads[2 * pr + 1])


def _sel_scores(c, jj, q_pad, sel, kvs_ref, n_blk):
    start = pl.multiple_of((c + 1) * NSA_BLOCK + (KV_PAD - SEL_CHUNK) - jj * SEL_CHUNK, NSA_BLOCK)
    kv = kvs_ref[0, 0, pl.ds(start, SEL_CHUNK), :]
    first_blk = c - (SEL_CHUNK_BLOCKS - 1) - jj * SEL_CHUNK_BLOCKS
    blk_minus_col = (lax.broadcasted_iota(i32, (n_blk, SEL_CHUNK), 0)
                     - (lax.broadcasted_iota(i32, (n_blk, SEL_CHUNK), 1) >> 6))
    expand = (blk_minus_col == first_blk).astype(bf16)
    hidden = (_dot(sel, expand) - 1.0) * (-NEG_INF)
    return _dot_nt(q_pad, kv) + _tile_rows(hidden, NSA_HPG), kv


def _nsa_prompt_front(c, q_ref, kvc_ref, kvs_ref, kvw_ref, tsel_ref, twin_ref, cmpq_ref, cmpt_ref, i, n_blk):
    rows = NSA_HPG * NSA_BLOCK
    wq = NSA_HPG * HEAD_DIM
    scale = HEAD_DIM ** -0.5
    lower = lax.broadcasted_iota(i32, (NSA_BLOCK, LANES), 1) < HEAD_DIM

    q_parts = []
    for h in range(NSA_HPG):
        x = q_ref[:, i * wq + (h // 2) * LANES:i * wq + (h // 2 + 1) * LANES] * scale
        if h % 2:
            x = pltpu.roll(x, HEAD_DIM, 1)
        q_parts.append(jnp.where(lower, x, 0.0))
    q_pad = jnp.concatenate(q_parts, axis=0).astype(bf16)

    kvc = kvc_ref[i].astype(bf16)
    blk = lax.broadcasted_iota(i32, (rows, n_blk), 1)
    qi = lax.broadcasted_iota(i32, (rows, n_blk), 0) & (NSA_BLOCK - 1)
    mask_c = (blk < c) | ((blk == c) & (qi == NSA_BLOCK - 1))
    bias_c = jnp.where(blk == c, cmpq_ref[i, 0],
                       jnp.where(blk == c - 1, cmpq_ref[i, 1],
                                 jnp.where(blk == c - 2, cmpq_ref[i, 2], 0.0)))
    s_c = jnp.where(mask_c, _dot_nt(q_pad, kvc) + bias_c, NEG_INF)
    m_c = jnp.max(s_c, axis=1, keepdims=True)
    e_c = jnp.where(mask_c, jnp.exp(s_c - m_c), 0.0)
    l_c = jnp.sum(e_c, axis=1, keepdims=True)
    p_c = e_c / jnp.where(l_c > 0.0, l_c, 1.0)
    o_c = _dot(p_c.astype(bf16), kvc)

    blk_t = lax.broadcasted_iota(i32, (n_blk, rows), 0)
    qi_t = lax.broadcasted_iota(i32, (n_blk, rows), 1) & (NSA_BLOCK - 1)
    mask_t = (blk_t < c) | ((blk_t == c) & (qi_t == NSA_BLOCK - 1))
    cols = cmpt_ref[i]
    bias_t = jnp.where(blk_t == c, cols[0:1, :],
                       jnp.where(blk_t == c - 1, cols[1:2, :],
                                 jnp.where(blk_t == c - 2, cols[2:3, :], 0.0)))
    s_t = jnp.where(mask_t, _dot_nt(kvc, q_pad) + bias_t, NEG_INF)
    m_t = jnp.max(s_t, axis=0, keepdims=True)
    e_t = jnp.where(mask_t, jnp.exp(s_t - m_t), 0.0)
    l_t = jnp.sum(e_t, axis=0, keepdims=True)
    p_t = e_t / jnp.where(l_t > 0.0, l_t, 1.0)
    imp_t = p_t[:, 0:LANES]
    for pr in range(1, NSA_HPG // 2):
        imp_t = imp_t + p_t[:, pr * LANES:(pr + 1) * LANES]
    imp_t = imp_t + pltpu.roll(imp_t, NSA_BLOCK, 1)
    sel_t = _select_blocks(imp_t, c, n_blk)
    eye = (lax.broadcasted_iota(i32, (NSA_BLOCK, LANES), 0)
           == lax.broadcasted_iota(i32, (NSA_BLOCK, LANES), 1)).astype(bf16)
    sel = _dot_nt(eye, sel_t.astype(bf16)).astype(bf16)

    start_w = pl.multiple_of((c + 1) * NSA_BLOCK + (KV_PAD - WIN_TILE), NSA_BLOCK)
    kv_w = kvw_ref[0, 0, pl.ds(start_w, WIN_TILE), :]
    col_w = lax.broadcasted_iota(i32, (1, WIN_TILE), 1)
    before_start = jnp.where(col_w < WIN_TILE - (c + 1) * NSA_BLOCK, NEG_INF, 0.0)
    o_w = _flash_finish(_flash_first(_dot_nt(q_pad, kv_w) + twin_ref[i] + before_start, kv_w))

    s0, kv0 = _sel_scores(c, 0, q_pad, sel, kvs_ref, n_blk)
    return q_pad, sel, (o_c, o_w), _flash_first(s0 + tsel_ref[i], kv0)


def _nsa_prompt_attn(q, gates, kvc, kv_slabs, tsel, twin, cmpq, cmpt, n_batch, seq, gps=2):
    m = q.shape[0]
    n_blk = seq // NSA_BLOCK
    g_ = NSA_KV_HEADS
    wq = NSA_HPG * HEAD_DIM
    slab = (1, 1) + kv_slabs.shape[2:]

    def slab_spec(branch, i):
        return pl.BlockSpec(slab, lambda b, gg, c: (branch * g_ + gg * gps + i, b, 0, 0))

    return pl.pallas_call(
        functools.partial(_nsa_prompt_kernel, n_blk=n_blk, gps=gps),
        grid=(n_batch, g_ // gps, n_blk),
        in_specs=[
            pl.BlockSpec((NSA_BLOCK, gps * wq), lambda b, gg, c: (b * n_blk + c, gg)),
            pl.BlockSpec((gps, NSA_BLOCK, LANES), lambda b, gg, c: (gg, b * n_blk + c, 0)),
            pl.BlockSpec((gps, n_blk, LANES), lambda b, gg, c: (gg, b, 0)),
            *[slab_spec(1, i) for i in range(gps)],
            *[slab_spec(2, i) for i in range(gps)],
            pl.BlockSpec((gps,) + tsel.shape[1:], lambda b, gg, c: (gg, 0, 0)),
            pl.BlockSpec((gps,) + twin.shape[1:], lambda b, gg, c: (gg, 0, 0)),
            pl.BlockSpec((gps,) + cmpq.shape[1:], lambda b, gg, c: (gg, 0, 0, 0)),
            pl.BlockSpec((gps,) + cmpt.shape[1:], lambda b, gg, c: (gg, 0, 0)),
        ],
        out_specs=pl.BlockSpec((NSA_BLOCK, gps * wq), lambda b, gg, c: (b * n_blk + c, gg)),
        out_shape=jax.ShapeDtypeStruct((m, NSA_HEADS * HEAD_DIM), f32),
        compiler_params=_cparams("parallel", "parallel", "arbitrary"),
        name="nsa_prompt_attn",
    )(q, gates, kvc, *([kv_slabs] * (2 * gps)), tsel, twin, cmpq, cmpt)


def _group_lane_mask(shape):
    r = lax.broadcasted_iota(i32, shape, 0)
    ln = lax.broadcasted_iota(i32, shape, 1)
    return (ln >> 6) == (r & (NSA_KV_HEADS - 1))


def _fold_groups(x):
    x = jnp.where(_group_lane_mask(x.shape), x, 0.0)
    out = x[:, 0:LANES]
    for pr in range(1, x.shape[1] // LANES):
        out = out + x[:, pr * LANES:(pr + 1) * LANES]
    return out + pltpu.roll(out, HEAD_DIM, 1)


def _select_blocks_sample(imp, cur, n_slots):
    imp_t = imp.T
    blk = lax.broadcasted_iota(i32, imp_t.shape, 0)
    valid = blk <= cur
    forced = (blk == 0) | (blk == cur) | (blk == cur - 1)
    if cur + 1 <= NSA_TOPK:
        return valid.astype(f32).T
    blk_f = blk.astype(f32)
    work = jnp.where(valid & jnp.logical_not(forced), imp_t, -1.0)
    sel = forced.astype(f32)
    for _ in range(NSA_TOPK - 3):
        mx = jnp.max(work, axis=0, keepdims=True)
        first = jnp.min(jnp.where(work == mx, blk_f, float(n_slots)), axis=0, keepdims=True)
        pick = blk_f == first
        work = jnp.where(pick, -2.0, work)
        sel = jnp.where(pick, 1.0, sel)
    return sel.T


def _nsa_sample_kernel(*refs, n_pages, page, past_len):
    pt_ref = refs[0]
    page_refs = refs[1:1 + n_pages]
    (win_ref, q_ref, kvn_ref, gt_ref, wc_ref, bkey_ref, bblk_ref, emat_ref,
     o_ref, nwin_ref, kcv_scr) = refs[1 + n_pages:]
    del pt_ref
    b = pl.program_id(0)
    gd = NSA_KV_HEADS * HEAD_DIM
    bpp = page // NSA_BLOCK
    n_past_blk = past_len // NSA_BLOCK
    n_slots = kcv_scr.shape[0]
    cur = n_past_blk
    scale = HEAD_DIM ** -0.5

    kvn = kvn_ref[pl.ds(b, 1), :]
    qm = jnp.where(_group_lane_mask((NSA_HEADS, gd)), q_ref[0] * scale, 0.0)
    qm_b = qm.astype(bf16)

    wc = wc_ref[...]
    for k in range(n_pages):
        cmp_part = page_refs[k][0, 0, :, 0:2 * gd].reshape(bpp, NSA_BLOCK, 2 * gd)
        kcv_scr[k * bpp:(k + 1) * bpp, :] = jnp.sum(cmp_part * wc[None], axis=1)
    row8 = lax.broadcasted_iota(i32, (8, 2 * gd), 0)
    kcv_scr[n_past_blk:n_past_blk + 8, :] = jnp.where(row8 == 0, kvn[:, 0:2 * gd] * wc[0:1, :], 0.0)
    rest = n_slots - n_past_blk - 8
    kcv_scr[n_past_blk + 8:n_slots, :] = jnp.zeros((rest, 2 * gd), f32)
    kcv = kcv_scr[...]
    kc_b = kcv[:, 0:gd].astype(bf16)
    vc_b = kcv[:, gd:2 * gd].astype(bf16)

    blk = lax.broadcasted_iota(i32, (NSA_HEADS, n_slots), 1)
    mask_c = blk * NSA_BLOCK + (NSA_BLOCK - 1) <= past_len
    s_c = jnp.where(mask_c, _dot_nt(qm_b, kc_b) + bblk_ref[...], NEG_INF)
    m_c = jnp.max(s_c, axis=1, keepdims=True)
    e_c = jnp.where(mask_c, jnp.exp(s_c - m_c), 0.0)
    l_c = jnp.sum(e_c, axis=1, keepdims=True)
    p_c = e_c / jnp.where(l_c > 0.0, l_c, 1.0)
    o_c = _fold_groups(_dot(p_c.astype(bf16), vc_b))

    imp = p_c[0:NSA_KV_HEADS]
    for hh in range(1, NSA_HPG):
        imp = imp + p_c[hh * NSA_KV_HEADS:(hh + 1) * NSA_KV_HEADS]
    sel = _select_blocks_sample(jnp.concatenate([imp, jnp.zeros_like(imp)], axis=0), cur, n_slots)
    sel16 = jnp.concatenate([sel[0:NSA_KV_HEADS]] * NSA_HPG, axis=0).astype(bf16)

    k_new = kvn[:, 2 * gd:3 * gd]
    v_new = kvn[:, 3 * gd:4 * gd]
    hidden = (_dot(sel16, emat_ref[...]) - 1.0) * (-NEG_INF)
    b_self = bkey_ref[:, past_len:past_len + LANES][:, 0:1]
    s_parts = [_dot_nt(qm_b, page_refs[k][0, 0, :, 2 * gd:3 * gd].astype(bf16)) for k in range(n_pages)]
    s_s = jnp.concatenate(s_parts, axis=1) + bkey_ref[:, 0:past_len] + hidden
    s_self = jnp.sum(qm * k_new, axis=1, keepdims=True) + b_self
    m_s = jnp.maximum(jnp.max(s_s, axis=1, keepdims=True), s_self)
    e_s = jnp.exp(s_s - m_s)
    e_self = jnp.exp(s_self - m_s)
    l_s = jnp.sum(e_s, axis=1, keepdims=True) + e_self
    e_sb = e_s.astype(bf16)
    pv = e_self * v_new
    for k in range(n_pages):
        pv = pv + _dot(e_sb[:, k * page:(k + 1) * page], page_refs[k][0, 0, :, 3 * gd:4 * gd].astype(bf16))
    o_s = _fold_groups(pv) / l_s

    n_buf = win_ref.shape[2]
    kw_b = win_ref[0, 0, :, 0:gd].astype(bf16)
    vw_b = win_ref[0, 0, :, gd:2 * gd].astype(bf16)
    kw_new = kvn[:, 4 * gd:5 * gd]
    vw_new = kvn[:, 5 * gd:6 * gd]
    wpos = past_len - n_buf + lax.broadcasted_iota(i32, (NSA_HEADS, n_buf), 1)
    msk_w = (past_len - wpos <= NSA_WINDOW) & (wpos >= 0)
    s_w = jnp.where(msk_w, _dot_nt(qm_b, kw_b) + bkey_ref[:, past_len - n_buf:past_len], NEG_INF)
    sw_self = jnp.sum(qm * kw_new, axis=1, keepdims=True) + b_self
    m_w = jnp.maximum(jnp.max(s_w, axis=1, keepdims=True), sw_self)
    e_w = jnp.where(msk_w, jnp.exp(s_w - m_w), 0.0)
    ew_self = jnp.exp(sw_self - m_w)
    l_w = jnp.sum(e_w, axis=1, keepdims=True) + ew_self
    o_w = _fold_groups(_dot(e_w.astype(bf16), vw_b) + ew_self * vw_new) / l_w

    gt = gt_ref[0]
    lane_g = lax.broadcasted_iota(i32, gt.shape, 1)

    def gate(branch):
        return jnp.sum(jnp.where(lane_g == branch, gt, 0.0), axis=1, keepdims=True)

    o_ref[0] = gate(0) * o_c + gate(1) * o_s + gate(2) * o_w

    nwin_ref[0, 0, 0:n_buf - 1, :] = win_ref[0, 0, 1:n_buf, :]
    nwin_ref[0, 0, n_buf - 1:n_buf, :] = kvn[:, 4 * gd:6 * gd]


def _nsa_sample_attn(page_table, cache_kv, cache_win, layer, q_hg, kv_nat, gates_hg, wc_nat, bkey, bblk,
                     past_len):
    n_dec, n_pages = page_table.shape
    _, n_phys, page, _, g_, d_ = cache_kv.shape
    gd = g_ * d_
    n_buf = cache_win.shape[2]
    n_slots = bblk.shape[1]
    cache2 = cache_kv.reshape(cache_kv.shape[0], n_phys, page, 4 * gd)
    win2 = cache_win.reshape(cache_win.shape[0], n_dec, n_buf, 2 * gd)
    emat = (lax.broadcasted_iota(i32, (n_slots, past_len), 0)
            == lax.broadcasted_iota(i32, (n_slots, past_len), 1) // NSA_BLOCK).astype(bf16)

    def page_spec(k):
        return pl.BlockSpec((1, 1, page, 4 * gd), lambda b, pt: (layer, pt[b * n_pages + k], 0, 0))

    grid_spec = pltpu.PrefetchScalarGridSpec(
        num_scalar_prefetch=1,
        grid=(n_dec,),
        in_specs=[
            *[page_spec(k) for k in range(n_pages)],
            pl.BlockSpec((1, 1, n_buf, 2 * gd), lambda b, pt: (layer, b, 0, 0)),
            pl.BlockSpec((1, NSA_HEADS, gd), lambda b, pt: (b, 0, 0)),
            pl.BlockSpec(kv_nat.shape, lambda b, pt: (0, 0)),
            pl.BlockSpec((1, NSA_HEADS, LANES), lambda b, pt: (b, 0, 0)),
            pl.BlockSpec(wc_nat.shape, lambda b, pt: (0, 0)),
            pl.BlockSpec(bkey.shape, lambda b, pt: (0, 0)),
            pl.BlockSpec(bblk.shape, lambda b, pt: (0, 0)),
            pl.BlockSpec(emat.shape, lambda b, pt: (0, 0)),
        ],
        out_specs=[
            pl.BlockSpec((1, NSA_HEADS, LANES), lambda b, pt: (b, 0, 0)),
            pl.BlockSpec((1, 1, n_buf, 2 * gd), lambda b, pt: (0, b, 0, 0)),
        ],
        scratch_shapes=[pltpu.VMEM((n_slots, 2 * gd), f32)],
    )
    o, nwin = pl.pallas_call(
        functools.partial(_nsa_sample_kernel, n_pages=n_pages, page=page, past_len=past_len),
        grid_spec=grid_spec,
        out_shape=[jax.ShapeDtypeStruct((n_dec, NSA_HEADS, LANES), f32),
                   jax.ShapeDtypeStruct((1, n_dec, n_buf, 2 * gd), f32)],
        compiler_params=_cparams("parallel"),
        name="nsa_sample_attn",
    )(page_table.reshape(-1), *([cache2] * n_pages), win2, q_hg, kv_nat, gates_hg, wc_nat, bkey, bblk, emat)
    return o[:, :, :HEAD_DIM], nwin[0]


def _conv_prompt_kernel(x_ref, g_ref, win_ref, cw_ref, wout_ref, o_ref, st_ref, carry_ref, *, n_t):
    t = pl.program_id(1)
    tm, d = x_ref.shape

    @pl.when(t == 0)
    def _():
        carry_ref[...] = jnp.zeros_like(carry_ref)

    x = x_ref[...]
    xn = _rms(x, g_ref[...]).astype(bf16)
    pr = _dot(xn, win_ref[...])
    u = pr[:, d:2 * d] * pr[:, 0:d]
    bg = pr[:, 2 * d:3 * d]
    prev = carry_ref[...]
    row = lax.broadcasted_iota(i32, (tm, d), 0)
    u1 = jnp.where(row == 0, prev[7:8, :], pltpu.roll(u, 1, 0))
    u2 = jnp.where(row == 0, prev[6:7, :], jnp.where(row == 1, prev[7:8, :], pltpu.roll(u, 2, 0)))
    cw = cw_ref[...]
    v = cw[0:1, :] * u2 + cw[1:2, :] * u1 + cw[2:3, :] * u
    o_ref[...] = x + _dot((bg * v).astype(bf16), wout_ref[...])
    carry_ref[...] = u[tm - 8:tm, :]

    @pl.when(t == n_t - 1)
    def _():
        st_ref[0] = u[tm - (CONV_W - 1):tm, :]


def _conv_prompt(x, g, w_in, cw, w_out, n_batch, seq, tm_target=512):
    m, d = x.shape
    tm = _row_tile(seq, tm_target)
    n_t = seq // tm
    return pl.pallas_call(
        functools.partial(_conv_prompt_kernel, n_t=n_t),
        grid=(n_batch, n_t),
        in_specs=[pl.BlockSpec((tm, d), lambda b, t: (b * n_t + t, 0)),
                  pl.BlockSpec((1, d), lambda b, t: (0, 0)),
                  pl.BlockSpec((d, 3 * d), lambda b, t: (0, 0)),
                  pl.BlockSpec((CONV_W, d), lambda b, t: (0, 0)),
                  pl.BlockSpec((d, d), lambda b, t: (0, 0))],
        out_specs=[pl.BlockSpec((tm, d), lambda b, t: (b * n_t + t, 0)),
                   pl.BlockSpec((1, CONV_W - 1, d), lambda b, t: (b, 0, 0))],
        out_shape=[jax.ShapeDtypeStruct((m, d), f32),
                   jax.ShapeDtypeStruct((n_batch, CONV_W - 1, d), f32)],
        scratch_shapes=[pltpu.VMEM((8, d), f32)],
        compiler_params=_cparams("parallel", "arbitrary"),
        name="conv_prompt",
    )(x, g.reshape(1, d), w_in, cw, w_out)


def _conv_sample_kernel(x_ref, g_ref, st_ref, win_ref, cw_ref, wout_ref, o_ref, nst_ref):
    d = x_ref.shape[1]
    x = x_ref[...]
    xn = _rms(x, g_ref[...]).astype(bf16)
    pr = _dot(xn, win_ref[...])
    u = pr[:, d:2 * d] * pr[:, 0:d]
    bg = pr[:, 2 * d:3 * d]
    st = st_ref[...]
    cw = cw_ref[...]
    v = cw[0:1, :] * st[:, 0:d] + cw[1:2, :] * st[:, d:2 * d] + cw[2:3, :] * u
    o_ref[...] = x + _dot((bg * v).astype(bf16), wout_ref[...])
    nst_ref[:, 0:d] = st[:, d:2 * d]
    nst_ref[:, d:2 * d] = u


def _conv_sample(x, g, state, w_in, cw, w_out):
    m, d = x.shape
    o, nst = pl.pallas_call(
        _conv_sample_kernel,
        out_shape=[jax.ShapeDtypeStruct((m, d), f32),
                   jax.ShapeDtypeStruct((m, (CONV_W - 1) * d), f32)],
        compiler_params=pltpu.CompilerParams(vmem_limit_bytes=VMEM_LIMIT_BYTES),
        name="conv_sample",
    )(x, g.reshape(1, d), state.reshape(m, (CONV_W - 1) * d), w_in, cw, w_out)
    return o, nst.reshape(m, CONV_W - 1, d)


def _pool_mix(win_sum_groups, xn, cnt_groups, wgrp_ref, scale):
    gw = xn.shape[1] // len(POOL_WINDOWS)
    outs = []
    for gi in range(len(POOL_WINDOWS)):
        diff = win_sum_groups[gi] / cnt_groups[gi] - xn[:, gi * gw:(gi + 1) * gw]
        outs.append(_dot(diff.astype(bf16), wgrp_ref[gi]))
    return jnp.concatenate(outs, axis=1) * scale


def _pool_prompt_kernel(x_ref, g_ref, wgrp_ref, sc_ref, o_ref, st_ref, carry_ref, *, n_t):
    t = pl.program_id(1)
    tm, d = x_ref.shape
    gw = d // len(POOL_WINDOWS)
    hist = carry_ref.shape[0]

    @pl.when(t == 0)
    def _():
        carry_ref[...] = jnp.zeros_like(carry_ref)

    x = x_ref[...]
    xn = _rms(x, g_ref[...])
    a = jnp.concatenate([carry_ref[...], xn], axis=0)
    sums = []
    s = a
    shift = 1
    for gi, w in enumerate(POOL_WINDOWS):
        while shift < w:
            s = s + pltpu.roll(s, shift, 0)
            shift *= 2
        sums.append(s[hist:, gi * gw:(gi + 1) * gw])
    pos = t * tm + lax.broadcasted_iota(i32, (tm, gw), 0)
    cnts = [jnp.minimum(pos + 1, w).astype(f32) for w in POOL_WINDOWS]
    o_ref[...] = x + _pool_mix(sums, xn, cnts, wgrp_ref, sc_ref[...])
    carry_ref[...] = xn[tm - hist:tm, :]

    @pl.when(t == n_t - 1)
    def _():
        st_ref[0] = xn[tm - POOL_HIST:tm, :]


def _pool_prompt(x, g, w_grp, scale, n_batch, seq, tm_target=512):
    m, d = x.shape
    tm = _row_tile(seq, tm_target)
    n_t = seq // tm
    return pl.pallas_call(
        functools.partial(_pool_prompt_kernel, n_t=n_t),
        grid=(n_batch, n_t),
        in_specs=[pl.BlockSpec((tm, d), lambda b, t: (b * n_t + t, 0)),
                  pl.BlockSpec((1, d), lambda b, t: (0, 0)),
                  pl.BlockSpec(w_grp.shape, lambda b, t: (0, 0, 0)),
                  pl.BlockSpec((1, d), lambda b, t: (0, 0))],
        out_specs=[pl.BlockSpec((tm, d), lambda b, t: (b * n_t + t, 0)),
                   pl.BlockSpec((1, POOL_HIST, d), lambda b, t: (b, 0, 0))],
        out_shape=[jax.ShapeDtypeStruct((m, d), f32),
                   jax.ShapeDtypeStruct((n_batch, POOL_HIST, d), f32)],
        scratch_shapes=[pltpu.VMEM((POOL_HIST + 1, d), f32)],
        compiler_params=_cparams("parallel", "arbitrary"),
        name="pool_prompt",
    )(x, g.reshape(1, d), w_grp, scale.reshape(1, d))


def _pool_sample_kernel(x_ref, g_ref, st_ref, wgrp_ref, sc_ref, o_ref, nst_ref, *, pos):
    m, d = x_ref.shape
    gw = d // len(POOL_WINDOWS)
    x = x_ref[...]
    xn = _rms(x, g_ref[...])
    st = st_ref[...]
    sums, cnts = [], []
    s = xn
    back = 1
    for gi, w in enumerate(POOL_WINDOWS):
        while back < w:
            s = s + st[:, (POOL_HIST - back) * d:(POOL_HIST - back + 1) * d]
            back += 1
        sums.append(s[:, gi * gw:(gi + 1) * gw])
        cnts.append(jnp.full((m, gw), float(min(pos + 1, w)), f32))
    o_ref[...] = x + _pool_mix(sums, xn, cnts, wgrp_ref, sc_ref[...])
    nst_ref[:, 0:(POOL_HIST - 1) * d] = st[:, d:POOL_HIST * d]
    nst_ref[:, (POOL_HIST - 1) * d:POOL_HIST * d] = xn


def _pool_sample(x, g, state, w_grp, scale, pos):
    m, d = x.shape
    o, nst = pl.pallas_call(
        functools.partial(_pool_sample_kernel, pos=pos),
        out_shape=[jax.ShapeDtypeStruct((m, d), f32),
                   jax.ShapeDtypeStruct((m, POOL_HIST * d), f32)],
        compiler_params=pltpu.CompilerParams(vmem_limit_bytes=VMEM_LIMIT_BYTES),
        name="pool_sample",
    )(x, g.reshape(1, d), state.reshape(m, POOL_HIST * d), w_grp, scale.reshape(1, d))
    return o, nst.reshape(m, POOL_HIST, d)


def _ca_prompt_kernel(x_ref, g_ref, wq_ref, kv_ref, wo_ref, o_ref):
    d = x_ref.shape[1]
    hd = d // CA_HEADS
    x = x_ref[...]
    xn = _rms(x, g_ref[...]).astype(bf16)
    q = (_dot(xn, wq_ref[...]) * (hd ** -0.5)).astype(bf16)
    outs = []
    for h in range(CA_HEADS):
        k = kv_ref[:, h * hd:(h + 1) * hd].astype(bf16)
        v = kv_ref[:, d + h * hd:d + (h + 1) * hd].astype(bf16)
        s = _dot_nt(q[:, h * hd:(h + 1) * hd], k)
        e = jnp.exp(s - jnp.max(s, axis=1, keepdims=True))
        p = e / jnp.sum(e, axis=1, keepdims=True)
        outs.append(_dot(p.astype(bf16), v))
    o = jnp.concatenate(outs, axis=1).astype(bf16)
    o_ref[...] = x + _dot(o, wo_ref[...])


def _ca_prompt(x, g, wq, mkv, wo, n_batch, seq, mem_len, tm_target=512):
    m, d = x.shape
    tm = _row_tile(seq, tm_target)
    n_t = seq // tm
    return pl.pallas_call(
        _ca_prompt_kernel,
        grid=(n_batch, n_t),
        in_specs=[pl.BlockSpec((tm, d), lambda b, t: (b * n_t + t, 0)),
                  pl.BlockSpec((1, d), lambda b, t: (0, 0)),
                  pl.BlockSpec((d, d), lambda b, t: (0, 0)),
                  pl.BlockSpec((mem_len, 2 * d), lambda b, t: (b, 0)),
                  pl.BlockSpec((d, d), lambda b, t: (0, 0))],
        out_specs=pl.BlockSpec((tm, d), lambda b, t: (b * n_t + t, 0)),
        out_shape=jax.ShapeDtypeStruct((m, d), f32),
        compiler_params=_cparams("parallel", "parallel"),
        name="ca_prompt",
    )(x, g.reshape(1, d), wq, mkv, wo)


def _ca_sample_kernel(q_ref, kv_ref, o_ref):
    bb, _, hd = q_ref.shape
    scale = hd ** -0.5
    for bi in range(bb):
        q = q_ref[bi] * scale
        k = kv_ref[0, bi, :, 0]
        v = kv_ref[0, bi, :, 1]
        s = jnp.sum(k * q[None], axis=2, keepdims=True)
        e = jnp.exp(s - jnp.max(s, axis=0, keepdims=True))
        p = e / jnp.sum(e, axis=0, keepdims=True)
        o_ref[bi] = jnp.sum(p * v, axis=0)


def _ca_sample(q, cache_mem_kv, layer, bb=2):
    n_dec, d = q.shape
    _, _, mem, _, heads, hd = cache_mem_kv.shape
    o = pl.pallas_call(
        _ca_sample_kernel,
        grid=(n_dec // bb,),
        in_specs=[pl.BlockSpec((bb, heads, hd), lambda i: (i, 0, 0)),
                  pl.BlockSpec((1, bb, mem, 2, heads, hd), lambda i: (layer, i, 0, 0, 0, 0))],
        out_specs=pl.BlockSpec((bb, heads, hd), lambda i: (i, 0, 0)),
        out_shape=jax.ShapeDtypeStruct((n_dec, heads, hd), f32),
        compiler_params=_cparams("parallel"),
        name="ca_sample",
    )(q.reshape(n_dec, heads, hd), cache_mem_kv)
    return o.reshape(n_dec, d)


def _nsa_weights(w_in, w_cmp):
    d_model = w_in.shape[0]
    nq = NSA_HEADS * HEAD_DIM
    nkv = 6 * NSA_KV_HEADS * HEAD_DIM
    wq = w_in[:, :nq].astype(bf16)
    wkv = w_in[:, nq:nq + nkv].reshape(d_model, 3, 2, NSA_KV_HEADS, HEAD_DIM)
    wkv = wkv.transpose(0, 1, 3, 2, 4).reshape(d_model, nkv).astype(bf16)
    wg = w_in[:, nq + nkv:].reshape(d_model, NSA_KV_HEADS, NSA_HPG * 3)
    wg = jnp.pad(wg, ((0, 0), (0, 0), (0, LANES - NSA_HPG * 3))).reshape(d_model, NSA_KV_HEADS * LANES)
    w_pack = jnp.concatenate([w_cmp[0], w_cmp[1]], axis=1)
    wc_nat = jnp.concatenate([jnp.tile(w_cmp[0], (1, NSA_KV_HEADS)),
                              jnp.tile(w_cmp[1], (1, NSA_KV_HEADS))], axis=1)
    return wq, wkv, wg.astype(bf16), w_pack, wc_nat


def _kv_natural(kvp):
    m = kvp.shape[1]
    x = kvp.reshape(3, NSA_KV_HEADS, m, 2, HEAD_DIM)
    return x.transpose(2, 0, 3, 1, 4).reshape(m, 6 * NSA_KV_HEADS * HEAD_DIM)


def kernel(x_prompt, x_sample, cache_nsa_kv, cache_nsa_win, state_conv, state_pool, cache_mem_kv,
           page_table, mem_prompt, rel_bias, norm_g, norm_mem_g, final_g, w_ffn_in, w_ffn_out,
           w_nsa_in, w_nsa_cmp, w_nsa_out, w_conv_in, conv_w, w_conv_out, w_pool, pool_scale,
           w_ca_q, w_ca_kv, w_ca_out):
    n_b, seq, d = x_prompt.shape
    n_dec = x_sample.shape[0]
    depth = norm_g.shape[0]
    mem_len = mem_prompt.shape[1]
    page = cache_nsa_kv.shape[2]
    past_len = page_table.shape[1] * page
    g_, hd = NSA_KV_HEADS, HEAD_DIM
    gd = g_ * hd

    xp = x_prompt.reshape(n_b * seq, d)
    xs = x_sample.reshape(n_dec, d)
    mem = mem_prompt.reshape(n_b * mem_len, d)

    tsel, twin, cmpq, cmpt = _t5_prompt_tables(rel_bias, seq // NSA_BLOCK)
    n_blk_pad = -(-(past_len // NSA_BLOCK + 8) // LANES) * LANES
    bkey, bblk = _t5_sample_tables(rel_bias, past_len, past_len + LANES, n_blk_pad)

    nsa_kv_p, nsa_win_p, conv_p, pool_p, mem_p = [], [], [], [], []
    nsa_kv_s, nsa_win_s, conv_s, pool_s = [], [], [], []
    for i in range(depth):
        kind, j = i % 3, i // 3
        wf_in = w_ffn_in[i].astype(bf16)
        wf_out = w_ffn_out[i].astype(bf16)
        xp = _ffn(xp, norm_g[i, 0], wf_in[0], wf_out[0])
        xs = _ffn(xs, norm_g[i, 0], wf_in[0], wf_out[0])
        if kind == 0:
            wq, wkv, wg, w_pack, wc_nat = _nsa_weights(w_nsa_in[j], w_nsa_cmp[j])
            w_out = w_nsa_out[j].astype(bf16)
            q, kvp, gates = _nsa_proj(xp, norm_g[i, 1], wq, wkv, wg)
            kvc = _compress(kvp, w_pack, g_)
            kv_slabs = jnp.pad(kvp.reshape(3 * g_, n_b, seq, LANES).astype(bf16),
                               ((0, 0), (0, 0), (KV_PAD, 0), (0, 0)))
            o = _nsa_prompt_attn(q, gates, kvc, kv_slabs, tsel, twin, cmpq, cmpt, n_b, seq)
            xp = _proj_res(o, w_out, xp)
            kv_nat = _kv_natural(kvp)
            nsa_kv_p.append(kv_nat[:, :4 * gd].reshape(n_b * seq // page, page, 4, g_, hd))
            n_keep = min(NSA_WINDOW, seq)
            nsa_win_p.append(kv_nat.reshape(n_b, seq, 6 * gd)[:, seq - n_keep:, 4 * gd:]
                             .reshape(n_b, n_keep, 2, g_, hd))
            q_s, kvp_s, gates_s = _nsa_proj(xs, norm_g[i, 1], wq, wkv, wg)
            kv_nat_s = _kv_natural(kvp_s)
            q_hg = q_s.reshape(n_dec, g_, NSA_HPG, hd).transpose(0, 2, 1, 3).reshape(n_dec, NSA_HEADS, hd)
            q_hg = jnp.tile(q_hg, (1, 1, g_))
            gates_hg = gates_s.transpose(1, 0, 2)[:, :, :NSA_HPG * 3].reshape(n_dec, g_, NSA_HPG, 3)
            gates_hg = jnp.pad(gates_hg.transpose(0, 2, 1, 3).reshape(n_dec, NSA_HEADS, 3),
                               ((0, 0), (0, 0), (0, LANES - 3)))
            o_s, nwin = _nsa_sample_attn(page_table, cache_nsa_kv, cache_nsa_win, j, q_hg, kv_nat_s,
                                         gates_hg, wc_nat, bkey, bblk, past_len)
            o_s = o_s.reshape(n_dec, NSA_HPG, g_, hd).transpose(0, 2, 1, 3).reshape(n_dec, NSA_HEADS * hd)
            xs = _proj_res(o_s, w_out, xs)
            nsa_kv_s.append(kv_nat_s[:, :4 * gd].reshape(n_dec, 1, 4, g_, hd))
            nsa_win_s.append(nwin.reshape(n_dec, -1, 2, g_, hd))
        elif kind == 1:
            wc_in = w_conv_in[j].astype(bf16)
            wc_out = w_conv_out[j].astype(bf16)
            xp, cst = _conv_prompt(xp, norm_g[i, 1], wc_in, conv_w[j], wc_out, n_b, seq)
            xs, csts = _conv_sample(xs, norm_g[i, 1], state_conv[j], wc_in, conv_w[j], wc_out)
            conv_p.append(cst)
            conv_s.append(csts)
        else:
            wp = w_pool[j].astype(bf16)
            xp, pst = _pool_prompt(xp, norm_g[i, 1], wp, pool_scale[j], n_b, seq)
            xs, psts = _pool_sample(xs, norm_g[i, 1], state_pool[j], wp, pool_scale[j], past_len)
            pool_p.append(pst)
            pool_s.append(psts)
        w_q = w_ca_q[i].astype(bf16)
        w_o = w_ca_out[i].astype(bf16)
        mkv = _normproj(mem, norm_mem_g[i], w_ca_kv[i].astype(bf16))
        mem_p.append(mkv.reshape(n_b, mem_len, 2, CA_HEADS, d // CA_HEADS))
        xp = _ca_prompt(xp, norm_g[i, 2], w_q, mkv, w_o, n_b, seq, mem_len)
        q_ca = _normproj(xs, norm_g[i, 2], w_q)
        o_ca = _ca_sample(q_ca, cache_mem_kv, i)
        xs = _proj_res(o_ca, w_o, xs)
        fg = final_g if i == depth - 1 else None
        xp = _ffn(xp, norm_g[i, 3], wf_in[1], wf_out[1], final_g=fg)
        xs = _ffn(xs, norm_g[i, 3], wf_in[1], wf_out[1], final_g=fg)

    return (xp.reshape(n_b, seq, d), xs.reshape(n_dec, 1, d),
            jnp.stack(nsa_kv_p), jnp.stack(nsa_win_p), jnp.stack(conv_p), jnp.stack(pool_p),
            jnp.stack(mem_p),
            jnp.stack(nsa_kv_s), jnp.stack(nsa_win_s), jnp.stack(conv_s), jnp.stack(pool_s))
```

```python
import functools
import math

import jax
import jax.numpy as jnp
from jax import lax
from jax.experimental import pallas as pl
from jax.experimental.pallas import tpu as pltpu

f32 = jnp.float32
bf16 = jnp.bfloat16
i32 = jnp.int32

NORM_EPS = 1e-6
NEG_INF = -1e30

NSA_HEADS = 16
NSA_KV_HEADS = 4
NSA_HPG = NSA_HEADS // NSA_KV_HEADS
HEAD_DIM = 64
NSA_BLOCK = 64
NSA_TOPK = 16
NSA_WINDOW = 512
REL_BUCKETS = 32
REL_MAX_DIST = 128
CONV_W = 3
POOL_WINDOWS = (2, 4, 8, 16)
POOL_HIST = max(POOL_WINDOWS) - 1
CA_HEADS = 4

LANES = 128
VMEM_LIMIT_BYTES = 56 * 1024 * 1024

N_NEAR = 3
SEL_CHUNK_BLOCKS = 16
SEL_CHUNK = SEL_CHUNK_BLOCKS * NSA_BLOCK
NEAR_BLOCKS = 4
NEAR_TILE = NEAR_BLOCKS * NSA_BLOCK
WIN_TILE = 640
KV_PAD = WIN_TILE


def _cparams(*sem):
    return pltpu.CompilerParams(dimension_semantics=sem, vmem_limit_bytes=VMEM_LIMIT_BYTES)


def _rms(x, g):
    return x * lax.rsqrt(jnp.mean(x * x, axis=-1, keepdims=True) + NORM_EPS) * g


def _dot(a, b):
    return jnp.dot(a, b, preferred_element_type=f32)


def _dot_nt(a, b):
    return lax.dot_general(a, b, (((1,), (1,)), ((), ())), preferred_element_type=f32)


def _row_tile(m, target):
    t = min(m, target)
    while m % t:
        t //= 2
    return t


def _ffn_kernel(x_ref, g_ref, wg_ref, wu_ref, wo_ref, *rest, n_f, final_norm):
    if final_norm:
        gf_ref, o_ref, xn_ref, acc_ref = rest
    else:
        o_ref, xn_ref, acc_ref = rest
    j = pl.program_id(1)

    @pl.when(j == 0)
    def _():
        xn_ref[...] = _rms(x_ref[...], g_ref[...]).astype(bf16)
        acc_ref[...] = jnp.zeros_like(acc_ref)

    xn = xn_ref[...]
    gate = _dot(xn, wg_ref[...])
    up = _dot(xn, wu_ref[...])
    act = (gate * jax.nn.sigmoid(gate) * up).astype(bf16)
    acc_ref[...] += _dot(act, wo_ref[...])

    @pl.when(j == n_f - 1)
    def _():
        y = x_ref[...] + 0.5 * acc_ref[...]
        if final_norm:
            y = _rms(y, gf_ref[...])
        o_ref[...] = y


def _ffn(x, g, w_in, w_out, final_g=None, tm_target=512):
    m, d = x.shape
    f = w_out.shape[0]
    tm = _row_tile(m, tm_target)
    tf = f
    for cand in (1408, 1024, 512, 256, 128):
        if f % cand == 0:
            tf = cand
            break
    n_f = f // tf
    in_specs = [
        pl.BlockSpec((tm, d), lambda i, j: (i, 0)),
        pl.BlockSpec((1, d), lambda i, j: (0, 0)),
        pl.BlockSpec((d, tf), lambda i, j: (0, j)),
        pl.BlockSpec((d, tf), lambda i, j: (0, n_f + j)),
        pl.BlockSpec((tf, d), lambda i, j: (j, 0)),
    ]
    args = [x, g.reshape(1, d), w_in, w_in, w_out]
    if final_g is not None:
        in_specs.append(pl.BlockSpec((1, d), lambda i, j: (0, 0)))
        args.append(final_g.reshape(1, d))
    return pl.pallas_call(
        functools.partial(_ffn_kernel, n_f=n_f, final_norm=final_g is not None),
        grid=(m // tm, n_f),
        in_specs=in_specs,
        out_specs=pl.BlockSpec((tm, d), lambda i, j: (i, 0)),
        out_shape=jax.ShapeDtypeStruct((m, d), f32),
        scratch_shapes=[pltpu.VMEM((tm, d), bf16), pltpu.VMEM((tm, d), f32)],
        compiler_params=_cparams("parallel", "arbitrary"),
        name="ffn",
    )(*args)


def _normproj_kernel(x_ref, g_ref, w_ref, o_ref):
    xn = _rms(x_ref[...], g_ref[...]).astype(bf16)
    o_ref[...] = _dot(xn, w_ref[...])


def _normproj(x, g, w, tm_target=512):
    m, d = x.shape
    n = w.shape[1]
    tm = _row_tile(m, tm_target)
    return pl.pallas_call(
        _normproj_kernel,
        grid=(m // tm,),
        in_specs=[pl.BlockSpec((tm, d), lambda i: (i, 0)),
                  pl.BlockSpec((1, d), lambda i: (0, 0)),
                  pl.BlockSpec((d, n), lambda i: (0, 0))],
        out_specs=pl.BlockSpec((tm, n), lambda i: (i, 0)),
        out_shape=jax.ShapeDtypeStruct((m, n), f32),
        compiler_params=_cparams("parallel"),
        name="normproj",
    )(x, g.reshape(1, d), w)


def _proj_res_kernel(a_ref, w_ref, x_ref, o_ref):
    o_ref[...] = x_ref[...] + _dot(a_ref[...].astype(bf16), w_ref[...])


def _proj_res(a, w, x, tm_target=512):
    m, k = a.shape
    d = w.shape[1]
    tm = _row_tile(m, tm_target)
    return pl.pallas_call(
        _proj_res_kernel,
        grid=(m // tm,),
        in_specs=[pl.BlockSpec((tm, k), lambda i: (i, 0)),
                  pl.BlockSpec((k, d), lambda i: (0, 0)),
                  pl.BlockSpec((tm, d), lambda i: (i, 0))],
        out_specs=pl.BlockSpec((tm, d), lambda i: (i, 0)),
        out_shape=jax.ShapeDtypeStruct((m, d), f32),
        compiler_params=_cparams("parallel"),
        name="proj_res",
    )(a, w, x)


def _nsa_proj_kernel(x_ref, g_ref, wq_ref, wkv_ref, wg_ref, q_ref, kv_ref, gt_ref):
    xn = _rms(x_ref[...], g_ref[...]).astype(bf16)
    q_ref[...] = _dot(xn, wq_ref[...])
    kv = _dot(xn, wkv_ref[...])
    for s in range(kv_ref.shape[0]):
        kv_ref[s] = kv[:, s * LANES:(s + 1) * LANES]
    gt = jax.nn.sigmoid(_dot(xn, wg_ref[...]))
    for s in range(gt_ref.shape[0]):
        gt_ref[s] = gt[:, s * LANES:(s + 1) * LANES]


def _nsa_proj(x, g, wq, wkv, wg, tm_target=512):
    m, d = x.shape
    nq, nkv, ng = wq.shape[1], wkv.shape[1], wg.shape[1]
    n_slab, n_gs = nkv // LANES, ng // LANES
    tm = _row_tile(m, tm_target)
    return pl.pallas_call(
        _nsa_proj_kernel,
        grid=(m // tm,),
        in_specs=[pl.BlockSpec((tm, d), lambda i: (i, 0)),
                  pl.BlockSpec((1, d), lambda i: (0, 0)),
                  pl.BlockSpec((d, nq), lambda i: (0, 0)),
                  pl.BlockSpec((d, nkv), lambda i: (0, 0)),
                  pl.BlockSpec((d, ng), lambda i: (0, 0))],
        out_specs=[pl.BlockSpec((tm, nq), lambda i: (i, 0)),
                   pl.BlockSpec((n_slab, tm, LANES), lambda i: (0, i, 0)),
                   pl.BlockSpec((n_gs, tm, LANES), lambda i: (0, i, 0))],
        out_shape=[jax.ShapeDtypeStruct((m, nq), f32),
                   jax.ShapeDtypeStruct((n_slab, m, LANES), f32),
                   jax.ShapeDtypeStruct((n_gs, m, LANES), f32)],
        compiler_params=_cparams("parallel"),
        name="nsa_proj",
    )(x, g.reshape(1, d), wq, wkv, wg)


def _compress_kernel(kv_ref, w_ref, o_ref):
    tm = kv_ref.shape[1]
    nb = tm // NSA_BLOCK
    x = kv_ref[0].reshape(nb, NSA_BLOCK, LANES)
    o_ref[0] = jnp.sum(x * w_ref[...][None], axis=1)


def _compress(kvp, w_pack, n_groups, tm_target=512):
    m = kvp.shape[1]
    tm = _row_tile(m, tm_target)
    nb = tm // NSA_BLOCK
    return pl.pallas_call(
        _compress_kernel,
        grid=(n_groups, m // tm),
        in_specs=[pl.BlockSpec((1, tm, LANES), lambda g, i: (g, i, 0)),
                  pl.BlockSpec((NSA_BLOCK, LANES), lambda g, i: (0, 0))],
        out_specs=pl.BlockSpec((1, nb, LANES), lambda g, i: (g, i, 0)),
        out_shape=jax.ShapeDtypeStruct((n_groups, m // NSA_BLOCK, LANES), f32),
        compiler_params=_cparams("parallel", "parallel"),
        name="nsa_compress",
    )(kvp, w_pack)


def _t5_bucket(rel):
    n = jnp.maximum(rel, 0)
    max_exact = REL_BUCKETS // 2
    nf = jnp.maximum(n, 1).astype(f32)
    large = max_exact + (jnp.log(nf / max_exact) / math.log(REL_MAX_DIST / max_exact)
                         * (REL_BUCKETS - max_exact)).astype(i32)
    large = jnp.minimum(large, REL_BUCKETS - 1)
    return jnp.where(n < max_exact, n, large)


def _bias_lookup(table_ref, bucket, head):
    out = jnp.zeros(bucket.shape, f32)
    for k in range(REL_BUCKETS):
        out = jnp.where(bucket == k, table_ref[k, head], out)
    return out


def _t5_prompt_kernel(table_ref, tsel_ref, twin_ref, cmpq_ref, cmpt_ref):
    g = pl.program_id(0)
    n_bc = cmpq_ref.shape[3]
    rel_s = (lax.broadcasted_iota(i32, (NSA_BLOCK, NEAR_TILE), 0) + (NEAR_TILE - NSA_BLOCK)
             - lax.broadcasted_iota(i32, (NSA_BLOCK, NEAR_TILE), 1))
    rel_w = (lax.broadcasted_iota(i32, (NSA_BLOCK, WIN_TILE), 0) + (WIN_TILE - NSA_BLOCK)
             - lax.broadcasted_iota(i32, (NSA_BLOCK, WIN_TILE), 1))
    bucket_s = _t5_bucket(rel_s)
    bucket_w = _t5_bucket(rel_w)
    qi_b = lax.broadcasted_iota(i32, (NSA_BLOCK, n_bc), 0)
    shp_t = (8, NSA_BLOCK)
    bucket_t = _t5_bucket(lax.broadcasted_iota(i32, shp_t, 1) - (NSA_BLOCK - 1)
                          + NSA_BLOCK * lax.broadcasted_iota(i32, shp_t, 0))
    for hh in range(NSA_HPG):
        head = g * NSA_HPG + hh
        far = table_ref[REL_BUCKETS - 1, head]
        rows = slice(hh * NSA_BLOCK, (hh + 1) * NSA_BLOCK)
        tsel_ref[0, rows, :] = jnp.where(rel_s >= 0, _bias_lookup(table_ref, bucket_s, head) - far, NEG_INF)
        twin_ref[0, rows, :] = jnp.where((rel_w >= 0) & (rel_w <= NSA_WINDOW),
                                         _bias_lookup(table_ref, bucket_w, head) - far, NEG_INF)
        for d in range(N_NEAR):
            cmpq_ref[0, d, rows, :] = _bias_lookup(
                table_ref, _t5_bucket(qi_b - (NSA_BLOCK - 1) + NSA_BLOCK * d), head) - far
        cmpt_ref[0, :, rows] = _bias_lookup(table_ref, bucket_t, head) - far


def _t5_prompt_tables(rel_bias, n_blk):
    rows = NSA_HPG * NSA_BLOCK
    return pl.pallas_call(
        _t5_prompt_kernel,
        grid=(NSA_KV_HEADS,),
        in_specs=[pl.BlockSpec(memory_space=pltpu.SMEM)],
        out_specs=[pl.BlockSpec((1, rows, NEAR_TILE), lambda g: (g, 0, 0)),
                   pl.BlockSpec((1, rows, WIN_TILE), lambda g: (g, 0, 0)),
                   pl.BlockSpec((1, N_NEAR, rows, n_blk), lambda g: (g, 0, 0, 0)),
                   pl.BlockSpec((1, 8, rows), lambda g: (g, 0, 0))],
        out_shape=[jax.ShapeDtypeStruct((NSA_KV_HEADS, rows, NEAR_TILE), f32),
                   jax.ShapeDtypeStruct((NSA_KV_HEADS, rows, WIN_TILE), f32),
                   jax.ShapeDtypeStruct((NSA_KV_HEADS, N_NEAR, rows, n_blk), f32),
                   jax.ShapeDtypeStruct((NSA_KV_HEADS, 8, rows), f32)],
        compiler_params=_cparams("parallel"),
        name="t5_prompt_tables",
    )(rel_bias)


def _t5_sample_kernel(table_ref, key_ref, blk_ref, *, qpos):
    kpos = lax.broadcasted_iota(i32, (1, key_ref.shape[1]), 1)
    bucket_k = _t5_bucket(qpos - kpos)
    blk = lax.broadcasted_iota(i32, (1, blk_ref.shape[1]), 1)
    bucket_b = _t5_bucket(qpos - (blk * NSA_BLOCK + NSA_BLOCK - 1))
    for r in range(NSA_HEADS):
        head = (r % NSA_KV_HEADS) * NSA_HPG + r // NSA_KV_HEADS
        key_ref[r:r + 1, :] = _bias_lookup(table_ref, bucket_k, head)
        blk_ref[r:r + 1, :] = _bias_lookup(table_ref, bucket_b, head)


def _t5_sample_tables(rel_bias, qpos, n_key, n_blk):
    return pl.pallas_call(
        functools.partial(_t5_sample_kernel, qpos=qpos),
        in_specs=[pl.BlockSpec(memory_space=pltpu.SMEM)],
        out_shape=[jax.ShapeDtypeStruct((NSA_HEADS, n_key), f32),
                   jax.ShapeDtypeStruct((NSA_HEADS, n_blk), f32)],
        name="t5_sample_tables",
    )(rel_bias)


def _flash_update(carry, s, v_aug):
    m, acc = carry
    m_new = jnp.maximum(m, jnp.max(s, axis=1, keepdims=True))
    p = jnp.exp(s - m_new).astype(bf16)
    return m_new, jnp.exp(m - m_new) * acc + _dot(p, v_aug)


def _flash_first(s, v_aug):
    m = jnp.max(s, axis=1, keepdims=True)
    return m, _dot(jnp.exp(s - m).astype(bf16), v_aug)


def _flash_finish(carry):
    _, acc = carry
    return acc / pltpu.roll(acc, HEAD_DIM, 1)


def _tile_rows(x, n):
    return jnp.concatenate([x] * n, axis=0)


def _select_blocks(imp_t, c, n_blk):
    blk = lax.broadcasted_iota(i32, imp_t.shape, 0)
    valid = blk <= c
    forced = (blk == 0) | (blk == c) | (blk == c - 1)

    blk_f = blk.astype(f32)
    work = jnp.where(valid & jnp.logical_not(forced), imp_t, -1.0)
    sel = forced.astype(f32)
    for _ in range(NSA_TOPK - 3):
        mx = jnp.max(work, axis=0, keepdims=True)
        first = jnp.min(jnp.where(work == mx, blk_f, float(n_blk)), axis=0, keepdims=True)
        pick = blk_f == first
        work = jnp.where(pick, -2.0, work)
        sel = jnp.where(pick, 1.0, sel)
    return jnp.where(c + 1 <= NSA_TOPK, valid.astype(f32), sel)


def _nsa_prompt_kernel(*refs, n_blk, gps):
    q_ref, gt_ref, kvc_ref = refs[0:3]
    slab_refs = [refs[3 + 4 * i:3 + 4 * (i + 1)] for i in range(gps)]
    tsel_ref, twin_ref, cmpq_ref, cmpt_ref, o_ref = refs[3 + 4 * gps:]
    c = pl.program_id(2)
    rows = NSA_HPG * NSA_BLOCK
    wq = NSA_HPG * HEAD_DIM
    fronts = [_nsa_prompt_front(c, q_ref, kvc_ref, slab_refs[i][2], slab_refs[i][3], twin_ref, cmpq_ref,
                                cmpt_ref, i, n_blk) for i in range(gps)]

    def far_body(jj, carries):
        out = []
        for f, (ks_ref, vs_ref, _, _), carry in zip(fronts, slab_refs, carries):
            start = pl.multiple_of(KV_PAD + jj * SEL_CHUNK, SEL_CHUNK // 8)
            s = _dot_nt(f[1], ks_ref[0, 0, pl.ds(start, SEL_CHUNK), :])
            out.append(_flash_update(carry, s, vs_ref[0, 0, pl.ds(start, SEL_CHUNK), :]))
        return tuple(out)

    n_far = jnp.maximum(c - (NEAR_BLOCKS - 1), 0)
    init = tuple((jnp.full((rows, 1), NEG_INF, f32), jnp.zeros((rows, LANES), f32)) for _ in range(gps))
    carries = lax.fori_loop(0, (n_far + SEL_CHUNK_BLOCKS - 1) // SEL_CHUNK_BLOCKS, far_body, init)

    lower = lax.broadcasted_iota(i32, (NSA_BLOCK, LANES), 1) < HEAD_DIM
    lane_g = lax.broadcasted_iota(i32, (NSA_BLOCK, LANES), 1)
    for i in range(gps):
        q_pad, _, near_hidden, (o_c, o_w) = fronts[i]
        ks_ref, vs_ref = slab_refs[i][0:2]
        start = pl.multiple_of((c + 1) * NSA_BLOCK + (KV_PAD - NEAR_TILE), NSA_BLOCK)
        s_near = (_dot_nt(q_pad, ks_ref[0, 0, pl.ds(start, NEAR_TILE), :]) + tsel_ref[i]
                  + _tile_rows(near_hidden, NSA_HPG))
        o_s = _flash_finish(_flash_update(carries[i], s_near, vs_ref[0, 0, pl.ds(start, NEAR_TILE), :]))
        gt = gt_ref[i]

        def gate(h, branch):
            return jnp.sum(jnp.where(lane_g == h * 3 + branch, gt, 0.0), axis=1, keepdims=True)

        heads = []
        for h in range(NSA_HPG):
            r = slice(h * NSA_BLOCK, (h + 1) * NSA_BLOCK)
            heads.append(gate(h, 0) * o_c[r] + gate(h, 1) * o_s[r] + gate(h, 2) * o_w[r])
        for pr in range(NSA_HPG // 2):
            o_ref[:, i * wq + pr * LANES:i * wq + (pr + 1) * LANES] = jnp.where(
                lower, pltpu.roll(heads[2 * pr], HEAD_DIM, 1), heads[2 * pr + 1])


def _nsa_prompt_front(c, q_ref, kvc_ref, kw_ref, vw_ref, twin_ref, cmpq_ref, cmpt_ref, i, n_blk):
    rows = NSA_HPG * NSA_BLOCK
    wq = NSA_HPG * HEAD_DIM
    scale = HEAD_DIM ** -0.5
    lower = lax.broadcasted_iota(i32, (NSA_BLOCK, LANES), 1) < HEAD_DIM

    q_parts = []
    for h in range(NSA_HPG):
        x = q_ref[:, i * wq + (h // 2) * LANES:i * wq + (h // 2 + 1) * LANES] * scale
        if h % 2:
            x = pltpu.roll(x, HEAD_DIM, 1)
        q_parts.append(jnp.where(lower, x, 0.0))
    q_pad = jnp.concatenate(q_parts, axis=0).astype(bf16)

    kvc = kvc_ref[i].astype(bf16)
    blk = lax.broadcasted_iota(i32, (rows, n_blk), 1)
    qi = lax.broadcasted_iota(i32, (rows, n_blk), 0) & (NSA_BLOCK - 1)
    mask_c = (blk < c) | ((blk == c) & (qi == NSA_BLOCK - 1))
    bias_c = jnp.where(blk == c, cmpq_ref[i, 0],
                       jnp.where(blk == c - 1, cmpq_ref[i, 1],
                                 jnp.where(blk == c - 2, cmpq_ref[i, 2], 0.0)))
    s_c = jnp.where(mask_c, _dot_nt(q_pad, kvc) + bias_c, NEG_INF)
    m_c = jnp.max(s_c, axis=1, keepdims=True)
    e_c = jnp.where(mask_c, jnp.exp(s_c - m_c), 0.0)
    l_c = jnp.sum(e_c, axis=1, keepdims=True)
    p_c = e_c / jnp.where(l_c > 0.0, l_c, 1.0)
    o_c = _dot(p_c.astype(bf16), kvc)

    blk_t = lax.broadcasted_iota(i32, (n_blk, rows), 0)
    qi_t = lax.broadcasted_iota(i32, (n_blk, rows), 1) & (NSA_BLOCK - 1)
    mask_t = (blk_t < c) | ((blk_t == c) & (qi_t == NSA_BLOCK - 1))
    cols = cmpt_ref[i]
    bias_t = jnp.where(blk_t == c, cols[0:1, :],
                       jnp.where(blk_t == c - 1, cols[1:2, :],
                                 jnp.where(blk_t == c - 2, cols[2:3, :], 0.0)))
    s_t = jnp.where(mask_t, _dot_nt(kvc, q_pad) + bias_t, NEG_INF)
    m_t = jnp.max(s_t, axis=0, keepdims=True)
    e_t = jnp.where(mask_t, jnp.exp(s_t - m_t), 0.0)
    l_t = jnp.sum(e_t, axis=0, keepdims=True)
    p_t = e_t / jnp.where(l_t > 0.0, l_t, 1.0)
    imp_t = p_t[:, 0:LANES]
    for pr in range(1, NSA_HPG // 2):
        imp_t = imp_t + p_t[:, pr * LANES:(pr + 1) * LANES]
    imp_t = imp_t + pltpu.roll(imp_t, NSA_BLOCK, 1)
    sel_t = _select_blocks(imp_t, c, n_blk).astype(bf16)
    eye = (lax.broadcasted_iota(i32, (NSA_BLOCK, LANES), 0)
           == lax.broadcasted_iota(i32, (NSA_BLOCK, LANES), 1)).astype(bf16)
    pieces = [jnp.zeros((HEAD_DIM, LANES), bf16), sel_t]
    if n_blk < LANES - HEAD_DIM:
        pieces.append(jnp.zeros((LANES - HEAD_DIM - n_blk, LANES), bf16))
    sel_hi = _dot_nt(eye, jnp.concatenate(pieces, axis=0))
    lane = lax.broadcasted_iota(i32, (NSA_BLOCK, LANES), 1)
    first_near = c - (NEAR_BLOCKS - 1)
    far_mask = jnp.where((sel_hi > 0.5) & (lane - HEAD_DIM < first_near), 0.0, NEG_INF)
    q_aug = jnp.concatenate([jnp.where(lower, x, far_mask) for x in q_parts], axis=0).astype(bf16)

    def sel_col(b):
        col = jnp.sum(jnp.where(lane == HEAD_DIM + b, sel_hi, 0.0), axis=1, keepdims=True)
        return jnp.broadcast_to(col, (NSA_BLOCK, NEAR_TILE))

    near_blk = lax.broadcasted_iota(i32, (NSA_BLOCK, NEAR_TILE), 1) >> 6
    has_prev = jnp.where(c >= 1, 1.0, 0.0)
    visible = jnp.where(near_blk == 0, sel_col(c - 3),
                        jnp.where(near_blk == 1, sel_col(c - 2), jnp.where(near_blk == 2, has_prev, 1.0)))
    near_hidden = jnp.where(visible > 0.5, 0.0, NEG_INF)

    start_w = pl.multiple_of((c + 1) * NSA_BLOCK + (KV_PAD - WIN_TILE), NSA_BLOCK)
    col_w = lax.broadcasted_iota(i32, (1, WIN_TILE), 1)
    before_start = jnp.where(col_w < WIN_TILE - (c + 1) * NSA_BLOCK, NEG_INF, 0.0)
    s_w = _dot_nt(q_pad, kw_ref[0, 0, pl.ds(start_w, WIN_TILE), :]) + twin_ref[i] + before_start
    o_w = _flash_finish(_flash_first(s_w, vw_ref[0, 0, pl.ds(start_w, WIN_TILE), :]))
    return q_pad, q_aug, near_hidden, (o_c, o_w)


def _nsa_slabs(kvp, n_batch, seq):
    g_ = NSA_KV_HEADS
    x = kvp.reshape(3, g_, n_batch, seq, 2, HEAD_DIM)
    blk = lax.broadcasted_iota(i32, (seq, HEAD_DIM), 0) // NSA_BLOCK
    onehot = (blk == lax.broadcasted_iota(i32, (seq, HEAD_DIM), 1)).astype(f32)
    ones = jnp.ones((g_, n_batch, seq, HEAD_DIM), f32)
    slabs = [jnp.concatenate([x[1, ..., 0, :], jnp.broadcast_to(onehot, ones.shape)], axis=-1),
             jnp.concatenate([ones, x[1, ..., 1, :]], axis=-1),
             jnp.concatenate([x[2, ..., 0, :], jnp.zeros_like(ones)], axis=-1),
             jnp.concatenate([ones, x[2, ..., 1, :]], axis=-1)]
    return [jnp.pad(s.astype(bf16), ((0, 0), (0, 0), (KV_PAD, 0), (0, 0))) for s in slabs]


def _nsa_prompt_attn(q, gates, kvc, slabs, tsel, twin, cmpq, cmpt, n_batch, seq, gps=2):
    m = q.shape[0]
    n_blk = seq // NSA_BLOCK
    assert n_blk <= HEAD_DIM, "the block one-hot lives in the 64 spare lanes of the key slab"
    g_ = NSA_KV_HEADS
    wq = NSA_HPG * HEAD_DIM
    slab = (1, 1) + slabs[0].shape[2:]

    def slab_spec(i):
        return pl.BlockSpec(slab, lambda b, gg, c: (gg * gps + i, b, 0, 0))

    return pl.pallas_call(
        functools.partial(_nsa_prompt_kernel, n_blk=n_blk, gps=gps),
        grid=(n_batch, g_ // gps, n_blk),
        in_specs=[
            pl.BlockSpec((NSA_BLOCK, gps * wq), lambda b, gg, c: (b * n_blk + c, gg)),
            pl.BlockSpec((gps, NSA_BLOCK, LANES), lambda b, gg, c: (gg, b * n_blk + c, 0)),
            pl.BlockSpec((gps, n_blk, LANES), lambda b, gg, c: (gg, b, 0)),
            *[slab_spec(i) for i in range(gps) for _ in slabs],
            pl.BlockSpec((gps,) + tsel.shape[1:], lambda b, gg, c: (gg, 0, 0)),
            pl.BlockSpec((gps,) + twin.shape[1:], lambda b, gg, c: (gg, 0, 0)),
            pl.BlockSpec((gps,) + cmpq.shape[1:], lambda b, gg, c: (gg, 0, 0, 0)),
            pl.BlockSpec((gps,) + cmpt.shape[1:], lambda b, gg, c: (gg, 0, 0)),
        ],
        out_specs=pl.BlockSpec((NSA_BLOCK, gps * wq), lambda b, gg, c: (b * n_blk + c, gg)),
        out_shape=jax.ShapeDtypeStruct((m, NSA_HEADS * HEAD_DIM), f32),
        compiler_params=_cparams("parallel", "parallel", "arbitrary"),
        name="nsa_prompt_attn",
    )(q, gates, kvc, *(list(slabs) * gps), tsel, twin, cmpq, cmpt)


def _group_lane_mask(shape):
    r = lax.broadcasted_iota(i32, shape, 0)
    ln = lax.broadcasted_iota(i32, shape, 1)
    return (ln >> 6) == (r & (NSA_KV_HEADS - 1))


def _fold_groups(x):
    x = jnp.where(_group_lane_mask(x.shape), x, 0.0)
    out = x[:, 0:LANES]
    for pr in range(1, x.shape[1] // LANES):
        out = out + x[:, pr * LANES:(pr + 1) * LANES]
    return out + pltpu.roll(out, HEAD_DIM, 1)


def _select_blocks_sample(imp, cur, n_slots):
    imp_t = imp.T
    blk = lax.broadcasted_iota(i32, imp_t.shape, 0)
    valid = blk <= cur
    forced = (blk == 0) | (blk == cur) | (blk == cur - 1)
    if cur + 1 <= NSA_TOPK:
        return valid.astype(f32).T
    blk_f = blk.astype(f32)
    work = jnp.where(valid & jnp.logical_not(forced), imp_t, -1.0)
    sel = forced.astype(f32)
    for _ in range(NSA_TOPK - 3):
        mx = jnp.max(work, axis=0, keepdims=True)
        first = jnp.min(jnp.where(work == mx, blk_f, float(n_slots)), axis=0, keepdims=True)
        pick = blk_f == first
        work = jnp.where(pick, -2.0, work)
        sel = jnp.where(pick, 1.0, sel)
    return sel.T


def _nsa_sample_kernel(*refs, n_pages, page, past_len):
    pt_ref = refs[0]
    page_refs = refs[1:1 + n_pages]
    (win_ref, q_ref, kvn_ref, gt_ref, wc_ref, bkey_ref, bblk_ref, emat_ref,
     o_ref, nwin_ref, kcv_scr) = refs[1 + n_pages:]
    del pt_ref
    b = pl.program_id(0)
    gd = NSA_KV_HEADS * HEAD_DIM
    bpp = page // NSA_BLOCK
    n_past_blk = past_len // NSA_BLOCK
    n_slots = kcv_scr.shape[0]
    cur = n_past_blk
    scale = HEAD_DIM ** -0.5

    kvn = kvn_ref[pl.ds(b, 1), :]
    qm = jnp.where(_group_lane_mask((NSA_HEADS, gd)), q_ref[0] * scale, 0.0)
    qm_b = qm.astype(bf16)

    wc = wc_ref[...]
    for k in range(n_pages):
        cmp_part = page_refs[k][0, 0, :, 0:2 * gd].reshape(bpp, NSA_BLOCK, 2 * gd)
        kcv_scr[k * bpp:(k + 1) * bpp, :] = jnp.sum(cmp_part * wc[None], axis=1)
    row8 = lax.broadcasted_iota(i32, (8, 2 * gd), 0)
    kcv_scr[n_past_blk:n_past_blk + 8, :] = jnp.where(row8 == 0, kvn[:, 0:2 * gd] * wc[0:1, :], 0.0)
    rest = n_slots - n_past_blk - 8
    kcv_scr[n_past_blk + 8:n_slots, :] = jnp.zeros((rest, 2 * gd), f32)
    kcv = kcv_scr[...]
    kc_b = kcv[:, 0:gd].astype(bf16)
    vc_b = kcv[:, gd:2 * gd].astype(bf16)

    blk = lax.broadcasted_iota(i32, (NSA_HEADS, n_slots), 1)
    mask_c = blk * NSA_BLOCK + (NSA_BLOCK - 1) <= past_len
    s_c = jnp.where(mask_c, _dot_nt(qm_b, kc_b) + bblk_ref[...], NEG_INF)
    m_c = jnp.max(s_c, axis=1, keepdims=True)
    e_c = jnp.where(mask_c, jnp.exp(s_c - m_c), 0.0)
    l_c = jnp.sum(e_c, axis=1, keepdims=True)
    p_c = e_c / jnp.where(l_c > 0.0, l_c, 1.0)
    o_c = _fold_groups(_dot(p_c.astype(bf16), vc_b))

    imp = p_c[0:NSA_KV_HEADS]
    for hh in range(1, NSA_HPG):
        imp = imp + p_c[hh * NSA_KV_HEADS:(hh + 1) * NSA_KV_HEADS]
    sel = _select_blocks_sample(jnp.concatenate([imp, jnp.zeros_like(imp)], axis=0), cur, n_slots)
    sel16 = jnp.concatenate([sel[0:NSA_KV_HEADS]] * NSA_HPG, axis=0).astype(bf16)

    k_new = kvn[:, 2 * gd:3 * gd]
    v_new = kvn[:, 3 * gd:4 * gd]
    hidden = (_dot(sel16, emat_ref[...]) - 1.0) * (-NEG_INF)
    b_self = bkey_ref[:, past_len:past_len + LANES][:, 0:1]
    s_parts = [_dot_nt(qm_b, page_refs[k][0, 0, :, 2 * gd:3 * gd].astype(bf16)) for k in range(n_pages)]
    s_s = jnp.concatenate(s_parts, axis=1) + bkey_ref[:, 0:past_len] + hidden
    s_self = jnp.sum(qm * k_new, axis=1, keepdims=True) + b_self
    m_s = jnp.maximum(jnp.max(s_s, axis=1, keepdims=True), s_self)
    e_s = jnp.exp(s_s - m_s)
    e_self = jnp.exp(s_self - m_s)
    l_s = jnp.sum(e_s, axis=1, keepdims=True) + e_self
    e_sb = e_s.astype(bf16)
    pv = e_self * v_new
    for k in range(n_pages):
        pv = pv + _dot(e_sb[:, k * page:(k + 1) * page], page_refs[k][0, 0, :, 3 * gd:4 * gd].astype(bf16))
    o_s = _fold_groups(pv) / l_s

    n_buf = win_ref.shape[2]
    kw_b = win_ref[0, 0, :, 0:gd].astype(bf16)
    vw_b = win_ref[0, 0, :, gd:2 * gd].astype(bf16)
    kw_new = kvn[:, 4 * gd:5 * gd]
    vw_new = kvn[:, 5 * gd:6 * gd]
    wpos = past_len - n_buf + lax.broadcasted_iota(i32, (NSA_HEADS, n_buf), 1)
    msk_w = (past_len - wpos <= NSA_WINDOW) & (wpos >= 0)
    s_w = jnp.where(msk_w, _dot_nt(qm_b, kw_b) + bkey_ref[:, past_len - n_buf:past_len], NEG_INF)
    sw_self = jnp.sum(qm * kw_new, axis=1, keepdims=True) + b_self
    m_w = jnp.maximum(jnp.max(s_w, axis=1, keepdims=True), sw_self)
    e_w = jnp.where(msk_w, jnp.exp(s_w - m_w), 0.0)
    ew_self = jnp.exp(sw_self - m_w)
    l_w = jnp.sum(e_w, axis=1, keepdims=True) + ew_self
    o_w = _fold_groups(_dot(e_w.astype(bf16), vw_b) + ew_self * vw_new) / l_w

    gt = gt_ref[0]
    lane_g = lax.broadcasted_iota(i32, gt.shape, 1)

    def gate(branch):
        return jnp.sum(jnp.where(lane_g == branch, gt, 0.0), axis=1, keepdims=True)

    o_ref[0] = gate(0) * o_c + gate(1) * o_s + gate(2) * o_w

    nwin_ref[0, 0, 0:n_buf - 1, :] = win_ref[0, 0, 1:n_buf, :]
    nwin_ref[0, 0, n_buf - 1:n_buf, :] = kvn[:, 4 * gd:6 * gd]


def _nsa_sample_attn(page_table, cache_kv, cache_win, layer, q_hg, kv_nat, gates_hg, wc_nat, bkey, bblk,
                     past_len):
    n_dec, n_pages = page_table.shape
    _, n_phys, page, _, g_, d_ = cache_kv.shape
    gd = g_ * d_
    n_buf = cache_win.shape[2]
    n_slots = bblk.shape[1]
    cache2 = cache_kv.reshape(cache_kv.shape[0], n_phys, page, 4 * gd)
    win2 = cache_win.reshape(cache_win.shape[0], n_dec, n_buf, 2 * gd)
    emat = (lax.broadcasted_iota(i32, (n_slots, past_len), 0)
            == lax.broadcasted_iota(i32, (n_slots, past_len), 1) // NSA_BLOCK).astype(bf16)

    def page_spec(k):
        return pl.BlockSpec((1, 1, page, 4 * gd), lambda b, pt: (layer, pt[b * n_pages + k], 0, 0))

    grid_spec = pltpu.PrefetchScalarGridSpec(
        num_scalar_prefetch=1,
        grid=(n_dec,),
        in_specs=[
            *[page_spec(k) for k in range(n_pages)],
            pl.BlockSpec((1, 1, n_buf, 2 * gd), lambda b, pt: (layer, b, 0, 0)),
            pl.BlockSpec((1, NSA_HEADS, gd), lambda b, pt: (b, 0, 0)),
            pl.BlockSpec(kv_nat.shape, lambda b, pt: (0, 0)),
            pl.BlockSpec((1, NSA_HEADS, LANES), lambda b, pt: (b, 0, 0)),
            pl.BlockSpec(wc_nat.shape, lambda b, pt: (0, 0)),
            pl.BlockSpec(bkey.shape, lambda b, pt: (0, 0)),
            pl.BlockSpec(bblk.shape, lambda b, pt: (0, 0)),
            pl.BlockSpec(emat.shape, lambda b, pt: (0, 0)),
        ],
        out_specs=[
            pl.BlockSpec((1, NSA_HEADS, LANES), lambda b, pt: (b, 0, 0)),
            pl.BlockSpec((1, 1, n_buf, 2 * gd), lambda b, pt: (0, b, 0, 0)),
        ],
        scratch_shapes=[pltpu.VMEM((n_slots, 2 * gd), f32)],
    )
    o, nwin = pl.pallas_call(
        functools.partial(_nsa_sample_kernel, n_pages=n_pages, page=page, past_len=past_len),
        grid_spec=grid_spec,
        out_shape=[jax.ShapeDtypeStruct((n_dec, NSA_HEADS, LANES), f32),
                   jax.ShapeDtypeStruct((1, n_dec, n_buf, 2 * gd), f32)],
        compiler_params=_cparams("parallel"),
        name="nsa_sample_attn",
    )(page_table.reshape(-1), *([cache2] * n_pages), win2, q_hg, kv_nat, gates_hg, wc_nat, bkey, bblk, emat)
    return o[:, :, :HEAD_DIM], nwin[0]


def _conv_prompt_kernel(x_ref, g_ref, win_ref, cw_ref, wout_ref, o_ref, st_ref, carry_ref, *, n_t):
    t = pl.program_id(1)
    tm, d = x_ref.shape

    @pl.when(t == 0)
    def _():
        carry_ref[...] = jnp.zeros_like(carry_ref)

    x = x_ref[...]
    xn = _rms(x, g_ref[...]).astype(bf16)
    pr = _dot(xn, win_ref[...])
    u = pr[:, d:2 * d] * pr[:, 0:d]
    bg = pr[:, 2 * d:3 * d]
    prev = carry_ref[...]
    row = lax.broadcasted_iota(i32, (tm, d), 0)
    u1 = jnp.where(row == 0, prev[7:8, :], pltpu.roll(u, 1, 0))
    u2 = jnp.where(row == 0, prev[6:7, :], jnp.where(row == 1, prev[7:8, :], pltpu.roll(u, 2, 0)))
    cw = cw_ref[...]
    v = cw[0:1, :] * u2 + cw[1:2, :] * u1 + cw[2:3, :] * u
    o_ref[...] = x + _dot((bg * v).astype(bf16), wout_ref[...])
    carry_ref[...] = u[tm - 8:tm, :]

    @pl.when(t == n_t - 1)
    def _():
        st_ref[0] = u[tm - (CONV_W - 1):tm, :]


def _conv_prompt(x, g, w_in, cw, w_out, n_batch, seq, tm_target=512):
    m, d = x.shape
    tm = _row_tile(seq, tm_target)
    n_t = seq // tm
    return pl.pallas_call(
        functools.partial(_conv_prompt_kernel, n_t=n_t),
        grid=(n_batch, n_t),
        in_specs=[pl.BlockSpec((tm, d), lambda b, t: (b * n_t + t, 0)),
                  pl.BlockSpec((1, d), lambda b, t: (0, 0)),
                  pl.BlockSpec((d, 3 * d), lambda b, t: (0, 0)),
                  pl.BlockSpec((CONV_W, d), lambda b, t: (0, 0)),
                  pl.BlockSpec((d, d), lambda b, t: (0, 0))],
        out_specs=[pl.BlockSpec((tm, d), lambda b, t: (b * n_t + t, 0)),
                   pl.BlockSpec((1, CONV_W - 1, d), lambda b, t: (b, 0, 0))],
        out_shape=[jax.ShapeDtypeStruct((m, d), f32),
                   jax.ShapeDtypeStruct((n_batch, CONV_W - 1, d), f32)],
        scratch_shapes=[pltpu.VMEM((8, d), f32)],
        compiler_params=_cparams("parallel", "arbitrary"),
        name="conv_prompt",
    )(x, g.reshape(1, d), w_in, cw, w_out)


def _conv_sample_kernel(x_ref, g_ref, st_ref, win_ref, cw_ref, wout_ref, o_ref, nst_ref):
    d = x_ref.shape[1]
    x = x_ref[...]
    xn = _rms(x, g_ref[...]).astype(bf16)
    pr = _dot(xn, win_ref[...])
    u = pr[:, d:2 * d] * pr[:, 0:d]
    bg = pr[:, 2 * d:3 * d]
    st = st_ref[...]
    cw = cw_ref[...]
    v = cw[0:1, :] * st[:, 0:d] + cw[1:2, :] * st[:, d:2 * d] + cw[2:3, :] * u
    o_ref[...] = x + _dot((bg * v).astype(bf16), wout_ref[...])
    nst_ref[:, 0:d] = st[:, d:2 * d]
    nst_ref[:, d:2 * d] = u


def _conv_sample(x, g, state, w_in, cw, w_out):
    m, d = x.shape
    o, nst = pl.pallas_call(
        _conv_sample_kernel,
        out_shape=[jax.ShapeDtypeStruct((m, d), f32),
                   jax.ShapeDtypeStruct((m, (CONV_W - 1) * d), f32)],
        compiler_params=pltpu.CompilerParams(vmem_limit_bytes=VMEM_LIMIT_BYTES),
        name="conv_sample",
    )(x, g.reshape(1, d), state.reshape(m, (CONV_W - 1) * d), w_in, cw, w_out)
    return o, nst.reshape(m, CONV_W - 1, d)


def _pool_mix(win_sum_groups, xn, cnt_groups, wgrp_ref, scale):
    gw = xn.shape[1] // len(POOL_WINDOWS)
    outs = []
    for gi in range(len(POOL_WINDOWS)):
        diff = win_sum_groups[gi] / cnt_groups[gi] - xn[:, gi * gw:(gi + 1) * gw]
        outs.append(_dot(diff.astype(bf16), wgrp_ref[gi]))
    return jnp.concatenate(outs, axis=1) * scale


def _pool_prompt_kernel(x_ref, g_ref, wgrp_ref, sc_ref, o_ref, st_ref, carry_ref, *, n_t):
    t = pl.program_id(1)
    tm, d = x_ref.shape
    gw = d // len(POOL_WINDOWS)
    hist = carry_ref.shape[0]

    @pl.when(t == 0)
    def _():
        carry_ref[...] = jnp.zeros_like(carry_ref)

    x = x_ref[...]
    xn = _rms(x, g_ref[...])
    a = jnp.concatenate([carry_ref[...], xn], axis=0)
    sums = []
    s = a
    shift = 1
    for gi, w in enumerate(POOL_WINDOWS):
        while shift < w:
            s = s + pltpu.roll(s, shift, 0)
            shift *= 2
        sums.append(s[hist:, gi * gw:(gi + 1) * gw])
    pos = t * tm + lax.broadcasted_iota(i32, (tm, gw), 0)
    cnts = [jnp.minimum(pos + 1, w).astype(f32) for w in POOL_WINDOWS]
    o_ref[...] = x + _pool_mix(sums, xn, cnts, wgrp_ref, sc_ref[...])
    carry_ref[...] = xn[tm - hist:tm, :]

    @pl.when(t == n_t - 1)
    def _():
        st_ref[0] = xn[tm - POOL_HIST:tm, :]


def _pool_prompt(x, g, w_grp, scale, n_batch, seq, tm_target=512):
    m, d = x.shape
    tm = _row_tile(seq, tm_target)
    n_t = seq // tm
    return pl.pallas_call(
        functools.partial(_pool_prompt_kernel, n_t=n_t),
        grid=(n_batch, n_t),
        in_specs=[pl.BlockSpec((tm, d), lambda b, t: (b * n_t + t, 0)),
                  pl.BlockSpec((1, d), lambda b, t: (0, 0)),
                  pl.BlockSpec(w_grp.shape, lambda b, t: (0, 0, 0)),
                  pl.BlockSpec((1, d), lambda b, t: (0, 0))],
        out_specs=[pl.BlockSpec((tm, d), lambda b, t: (b * n_t + t, 0)),
                   pl.BlockSpec((1, POOL_HIST, d), lambda b, t: (b, 0, 0))],
        out_shape=[jax.ShapeDtypeStruct((m, d), f32),
                   jax.ShapeDtypeStruct((n_batch, POOL_HIST, d), f32)],
        scratch_shapes=[pltpu.VMEM((POOL_HIST + 1, d), f32)],
        compiler_params=_cparams("parallel", "arbitrary"),
        name="pool_prompt",
    )(x, g.reshape(1, d), w_grp, scale.reshape(1, d))


def _pool_sample_kernel(x_ref, g_ref, st_ref, wgrp_ref, sc_ref, o_ref, nst_ref, *, pos):
    m, d = x_ref.shape
    gw = d // len(POOL_WINDOWS)
    x = x_ref[...]
    xn = _rms(x, g_ref[...])
    st = st_ref[...]
    sums, cnts = [], []
    s = xn
    back = 1
    for gi, w in enumerate(POOL_WINDOWS):
        while back < w:
            s = s + st[:, (POOL_HIST - back) * d:(POOL_HIST - back + 1) * d]
            back += 1
        sums.append(s[:, gi * gw:(gi + 1) * gw])
        cnts.append(jnp.full((m, gw), float(min(pos + 1, w)), f32))
    o_ref[...] = x + _pool_mix(sums, xn, cnts, wgrp_ref, sc_ref[...])
    nst_ref[:, 0:(POOL_HIST - 1) * d] = st[:, d:POOL_HIST * d]
    nst_ref[:, (POOL_HIST - 1) * d:POOL_HIST * d] = xn


def _pool_sample(x, g, state, w_grp, scale, pos):
    m, d = x.shape
    o, nst = pl.pallas_call(
        functools.partial(_pool_sample_kernel, pos=pos),
        out_shape=[jax.ShapeDtypeStruct((m, d), f32),
                   jax.ShapeDtypeStruct((m, POOL_HIST * d), f32)],
        compiler_params=pltpu.CompilerParams(vmem_limit_bytes=VMEM_LIMIT_BYTES),
        name="pool_sample",
    )(x, g.reshape(1, d), state.reshape(m, POOL_HIST * d), w_grp, scale.reshape(1, d))
    return o, nst.reshape(m, POOL_HIST, d)


def _ca_prompt_kernel(x_ref, g_ref, wq_ref, kv_ref, wo_ref, o_ref):
    d = x_ref.shape[1]
    hd = d // CA_HEADS
    x = x_ref[...]
    xn = _rms(x, g_ref[...]).astype(bf16)
    q = (_dot(xn, wq_ref[...]) * (hd ** -0.5)).astype(bf16)
    outs = []
    for h in range(CA_HEADS):
        k = kv_ref[:, h * hd:(h + 1) * hd].astype(bf16)
        v = kv_ref[:, d + h * hd:d + (h + 1) * hd].astype(bf16)
        s = _dot_nt(q[:, h * hd:(h + 1) * hd], k)
        e = jnp.exp(s - jnp.max(s, axis=1, keepdims=True))
        p = e / jnp.sum(e, axis=1, keepdims=True)
        outs.append(_dot(p.astype(bf16), v))
    o = jnp.concatenate(outs, axis=1).astype(bf16)
    o_ref[...] = x + _dot(o, wo_ref[...])


def _ca_prompt(x, g, wq, mkv, wo, n_batch, seq, mem_len, tm_target=512):
    m, d = x.shape
    tm = _row_tile(seq, tm_target)
    n_t = seq // tm
    return pl.pallas_call(
        _ca_prompt_kernel,
        grid=(n_batch, n_t),
        in_specs=[pl.BlockSpec((tm, d), lambda b, t: (b * n_t + t, 0)),
                  pl.BlockSpec((1, d), lambda b, t: (0, 0)),
                  pl.BlockSpec((d, d), lambda b, t: (0, 0)),
                  pl.BlockSpec((mem_len, 2 * d), lambda b, t: (b, 0)),
                  pl.BlockSpec((d, d), lambda b, t: (0, 0))],
        out_specs=pl.BlockSpec((tm, d), lambda b, t: (b * n_t + t, 0)),
        out_shape=jax.ShapeDtypeStruct((m, d), f32),
        compiler_params=_cparams("parallel", "parallel"),
        name="ca_prompt",
    )(x, g.reshape(1, d), wq, mkv, wo)


def _ca_sample_kernel(q_ref, kv_ref, o_ref):
    bb, _, hd = q_ref.shape
    scale = hd ** -0.5
    for bi in range(bb):
        q = q_ref[bi] * scale
        k = kv_ref[0, bi, :, 0]
        v = kv_ref[0, bi, :, 1]
        s = jnp.sum(k * q[None], axis=2, keepdims=True)
        e = jnp.exp(s - jnp.max(s, axis=0, keepdims=True))
        p = e / jnp.sum(e, axis=0, keepdims=True)
        o_ref[bi] = jnp.sum(p * v, axis=0)


def _ca_sample(q, cache_mem_kv, layer, bb=2):
    n_dec, d = q.shape
    _, _, mem, _, heads, hd = cache_mem_kv.shape
    o = pl.pallas_call(
        _ca_sample_kernel,
        grid=(n_dec // bb,),
        in_specs=[pl.BlockSpec((bb, heads, hd), lambda i: (i, 0, 0)),
                  pl.BlockSpec((1, bb, mem, 2, heads, hd), lambda i: (layer, i, 0, 0, 0, 0))],
        out_specs=pl.BlockSpec((bb, heads, hd), lambda i: (i, 0, 0)),
        out_shape=jax.ShapeDtypeStruct((n_dec, heads, hd), f32),
        compiler_params=_cparams("parallel"),
        name="ca_sample",
    )(q.reshape(n_dec, heads, hd), cache_mem_kv)
    return o.reshape(n_dec, d)


def _nsa_weights(w_in, w_cmp):
    d_model = w_in.shape[0]
    nq = NSA_HEADS * HEAD_DIM
    nkv = 6 * NSA_KV_HEADS * HEAD_DIM
    wq = w_in[:, :nq].astype(bf16)
    wkv = w_in[:, nq:nq + nkv].reshape(d_model, 3, 2, NSA_KV_HEADS, HEAD_DIM)
    wkv = wkv.transpose(0, 1, 3, 2, 4).reshape(d_model, nkv).astype(bf16)
    wg = w_in[:, nq + nkv:].reshape(d_model, NSA_KV_HEADS, NSA_HPG * 3)
    wg = jnp.pad(wg, ((0, 0), (0, 0), (0, LANES - NSA_HPG * 3))).reshape(d_model, NSA_KV_HEADS * LANES)
    w_pack = jnp.concatenate([w_cmp[0], w_cmp[1]], axis=1)
    wc_nat = jnp.concatenate([jnp.tile(w_cmp[0], (1, NSA_KV_HEADS)),
                              jnp.tile(w_cmp[1], (1, NSA_KV_HEADS))], axis=1)
    return wq, wkv, wg.astype(bf16), w_pack, wc_nat


def _kv_natural(kvp):
    m = kvp.shape[1]
    x = kvp.reshape(3, NSA_KV_HEADS, m, 2, HEAD_DIM)
    return x.transpose(2, 0, 3, 1, 4).reshape(m, 6 * NSA_KV_HEADS * HEAD_DIM)


def kernel(x_prompt, x_sample, cache_nsa_kv, cache_nsa_win, state_conv, state_pool, cache_mem_kv,
           page_table, mem_prompt, rel_bias, norm_g, norm_mem_g, final_g, w_ffn_in, w_ffn_out,
           w_nsa_in, w_nsa_cmp, w_nsa_out, w_conv_in, conv_w, w_conv_out, w_pool, pool_scale,
           w_ca_q, w_ca_kv, w_ca_out):
    n_b, seq, d = x_prompt.shape
    n_dec = x_sample.shape[0]
    depth = norm_g.shape[0]
    mem_len = mem_prompt.shape[1]
    page = cache_nsa_kv.shape[2]
    past_len = page_table.shape[1] * page
    g_, hd = NSA_KV_HEADS, HEAD_DIM
    gd = g_ * hd

    xp = x_prompt.reshape(n_b * seq, d)
    xs = x_sample.reshape(n_dec, d)
    mem = mem_prompt.reshape(n_b * mem_len, d)

    tsel, twin, cmpq, cmpt = _t5_prompt_tables(rel_bias, seq // NSA_BLOCK)
    n_blk_pad = -(-(past_len // NSA_BLOCK + 8) // LANES) * LANES
    bkey, bblk = _t5_sample_tables(rel_bias, past_len, past_len + LANES, n_blk_pad)

    nsa_kv_p, nsa_win_p, conv_p, pool_p, mem_p = [], [], [], [], []
    nsa_kv_s, nsa_win_s, conv_s, pool_s = [], [], [], []
    for i in range(depth):
        kind, j = i % 3, i // 3
        wf_in = w_ffn_in[i].astype(bf16)
        wf_out = w_ffn_out[i].astype(bf16)
        xp = _ffn(xp, norm_g[i, 0], wf_in[0], wf_out[0])
        xs = _ffn(xs, norm_g[i, 0], wf_in[0], wf_out[0])
        if kind == 0:
            wq, wkv, wg, w_pack, wc_nat = _nsa_weights(w_nsa_in[j], w_nsa_cmp[j])
            w_out = w_nsa_out[j].astype(bf16)
            q, kvp, gates = _nsa_proj(xp, norm_g[i, 1], wq, wkv, wg)
            kvc = _compress(kvp, w_pack, g_)
            o = _nsa_prompt_attn(q, gates, kvc, _nsa_slabs(kvp, n_b, seq), tsel, twin, cmpq, cmpt, n_b, seq)
            xp = _proj_res(o, w_out, xp)
            kv_nat = _kv_natural(kvp)
            nsa_kv_p.append(kv_nat[:, :4 * gd].reshape(n_b * seq // page, page, 4, g_, hd))
            n_keep = min(NSA_WINDOW, seq)
            nsa_win_p.append(kv_nat.reshape(n_b, seq, 6 * gd)[:, seq - n_keep:, 4 * gd:]
                             .reshape(n_b, n_keep, 2, g_, hd))
            q_s, kvp_s, gates_s = _nsa_proj(xs, norm_g[i, 1], wq, wkv, wg)
            kv_nat_s = _kv_natural(kvp_s)
            q_hg = q_s.reshape(n_dec, g_, NSA_HPG, hd).transpose(0, 2, 1, 3).reshape(n_dec, NSA_HEADS, hd)
            q_hg = jnp.tile(q_hg, (1, 1, g_))
            gates_hg = gates_s.transpose(1, 0, 2)[:, :, :NSA_HPG * 3].reshape(n_dec, g_, NSA_HPG, 3)
            gates_hg = jnp.pad(gates_hg.transpose(0, 2, 1, 3).reshape(n_dec, NSA_HEADS, 3),
                               ((0, 0), (0, 0), (0, LANES - 3)))
            o_s, nwin = _nsa_sample_attn(page_table, cache_nsa_kv, cache_nsa_win, j, q_hg, kv_nat_s,
                                         gates_hg, wc_nat, bkey, bblk, past_len)
            o_s = o_s.reshape(n_dec, NSA_HPG, g_, hd).transpose(0, 2, 1, 3).reshape(n_dec, NSA_HEADS * hd)
            xs = _proj_res(o_s, w_out, xs)
            nsa_kv_s.append(kv_nat_s[:, :4 * gd].reshape(n_dec, 1, 4, g_, hd))
            nsa_win_s.append(nwin.reshape(n_dec, -1, 2, g_, hd))
        elif kind == 1:
            wc_in = w_conv_in[j].astype(bf16)
            wc_out = w_conv_out[j].astype(bf16)
            xp, cst = _conv_prompt(xp, norm_g[i, 1], wc_in, conv_w[j], wc_out, n_b, seq)
            xs, csts = _conv_sample(xs, norm_g[i, 1], state_conv[j], wc_in, conv_w[j], wc_out)
            conv_p.append(cst)
            conv_s.append(csts)
        else:
            wp = w_pool[j].astype(bf16)
            xp, pst = _pool_prompt(xp, norm_g[i, 1], wp, pool_scale[j], n_b, seq)
            xs, psts = _pool_sample(xs, norm_g[i, 1], state_pool[j], wp, pool_scale[j], past_len)
            pool_p.append(pst)
            pool_s.append(psts)
        w_q = w_ca_q[i].astype(bf16)
        w_o = w_ca_out[i].astype(bf16)
        mkv = _normproj(mem, norm_mem_g[i], w_ca_kv[i].astype(bf16))
        mem_p.append(mkv.reshape(n_b, mem_len, 2, CA_HEADS, d // CA_HEADS))
        xp = _ca_prompt(xp, norm_g[i, 2], w_q, mkv, w_o, n_b, seq, mem_len)
        q_ca = _normproj(xs, norm_g[i, 2], w_q)
        o_ca = _ca_sample(q_ca, cache_mem_kv, i)
        xs = _proj_res(o_ca, w_o, xs)
        fg = final_g if i == depth - 1 else None
        xp = _ffn(xp, norm_g[i, 3], wf_in[1], wf_out[1], final_g=fg)
        xs = _ffn(xs, norm_g[i, 3], wf_in[1], wf_out[1], final_g=fg)

    return (xp.reshape(n_b, seq, d), xs.reshape(n_dec, 1, d),
            jnp.stack(nsa_kv_p), jnp.stack(nsa_win_p), jnp.stack(conv_p), jnp.stack(pool_p),
            jnp.stack(mem_p),
            jnp.stack(nsa_kv_s), jnp.stack(nsa_win_s), jnp.stack(conv_s), jnp.stack(pool_s))
```

```python
import functools
import math

import jax
import jax.numpy as jnp
from jax import lax
from jax.experimental import pallas as pl
from jax.experimental.pallas import tpu as pltpu

f32 = jnp.float32
bf16 = jnp.bfloat16
i32 = jnp.int32

NORM_EPS = 1e-6
NEG_INF = -1e30

NSA_HEADS = 16
NSA_KV_HEADS = 4
NSA_HPG = NSA_HEADS // NSA_KV_HEADS
HEAD_DIM = 64
NSA_BLOCK = 64
NSA_TOPK = 16
NSA_WINDOW = 512
REL_BUCKETS = 32
REL_MAX_DIST = 128
CONV_W = 3
POOL_WINDOWS = (2, 4, 8, 16)
POOL_HIST = max(POOL_WINDOWS) - 1
CA_HEADS = 4

LANES = 128
VMEM_LIMIT_BYTES = 56 * 1024 * 1024

N_NEAR = 3
SEL_CHUNK_BLOCKS = 16
SEL_CHUNK = SEL_CHUNK_BLOCKS * NSA_BLOCK
NEAR_BLOCKS = 4
NEAR_TILE = NEAR_BLOCKS * NSA_BLOCK
WIN_TILE = 640
KV_PAD = 1024


def _cparams(*sem):
    return pltpu.CompilerParams(dimension_semantics=sem, vmem_limit_bytes=VMEM_LIMIT_BYTES)


def _rms(x, g):
    return x * lax.rsqrt(jnp.mean(x * x, axis=-1, keepdims=True) + NORM_EPS) * g


def _dot(a, b):
    return jnp.dot(a, b, preferred_element_type=f32)


def _dot_nt(a, b):
    return lax.dot_general(a, b, (((1,), (1,)), ((), ())), preferred_element_type=f32)


def _row_tile(m, target):
    t = min(m, target)
    while m % t:
        t //= 2
    return t


def _ffn_kernel(x_ref, g_ref, wg_ref, wu_ref, wo_ref, *rest, n_f, final_norm):
    if final_norm:
        gf_ref, o_ref, xn_ref, acc_ref = rest
    else:
        o_ref, xn_ref, acc_ref = rest
    j = pl.program_id(1)

    @pl.when(j == 0)
    def _():
        xn_ref[...] = _rms(x_ref[...], g_ref[...]).astype(bf16)
        acc_ref[...] = jnp.zeros_like(acc_ref)

    xn = xn_ref[...]
    gate = _dot(xn, wg_ref[...])
    up = _dot(xn, wu_ref[...])
    act = (gate * jax.nn.sigmoid(gate) * up).astype(bf16)
    acc_ref[...] += _dot(act, wo_ref[...])

    @pl.when(j == n_f - 1)
    def _():
        y = x_ref[...] + 0.5 * acc_ref[...]
        if final_norm:
            y = _rms(y, gf_ref[...])
        o_ref[...] = y


def _ffn(x, g, w_in, w_out, final_g=None, tm_target=512):
    m, d = x.shape
    f = w_out.shape[0]
    tm = _row_tile(m, tm_target)
    tf = f
    for cand in (1408, 1024, 512, 256, 128):
        if f % cand == 0:
            tf = cand
            break
    n_f = f // tf
    in_specs = [
        pl.BlockSpec((tm, d), lambda i, j: (i, 0)),
        pl.BlockSpec((1, d), lambda i, j: (0, 0)),
        pl.BlockSpec((d, tf), lambda i, j: (0, j)),
        pl.BlockSpec((d, tf), lambda i, j: (0, n_f + j)),
        pl.BlockSpec((tf, d), lambda i, j: (j, 0)),
    ]
    args = [x, g.reshape(1, d), w_in, w_in, w_out]
    if final_g is not None:
        in_specs.append(pl.BlockSpec((1, d), lambda i, j: (0, 0)))
        args.append(final_g.reshape(1, d))
    return pl.pallas_call(
        functools.partial(_ffn_kernel, n_f=n_f, final_norm=final_g is not None),
        grid=(m // tm, n_f),
        in_specs=in_specs,
        out_specs=pl.BlockSpec((tm, d), lambda i, j: (i, 0)),
        out_shape=jax.ShapeDtypeStruct((m, d), f32),
        scratch_shapes=[pltpu.VMEM((tm, d), bf16), pltpu.VMEM((tm, d), f32)],
        compiler_params=_cparams("parallel", "arbitrary"),
        name="ffn",
    )(*args)


def _normproj_kernel(x_ref, g_ref, w_ref, o_ref):
    xn = _rms(x_ref[...], g_ref[...]).astype(bf16)
    o_ref[...] = _dot(xn, w_ref[...])


def _normproj(x, g, w, tm_target=512):
    m, d = x.shape
    n = w.shape[1]
    tm = _row_tile(m, tm_target)
    return pl.pallas_call(
        _normproj_kernel,
        grid=(m // tm,),
        in_specs=[pl.BlockSpec((tm, d), lambda i: (i, 0)),
                  pl.BlockSpec((1, d), lambda i: (0, 0)),
                  pl.BlockSpec((d, n), lambda i: (0, 0))],
        out_specs=pl.BlockSpec((tm, n), lambda i: (i, 0)),
        out_shape=jax.ShapeDtypeStruct((m, n), f32),
        compiler_params=_cparams("parallel"),
        name="normproj",
    )(x, g.reshape(1, d), w)


def _proj_res_kernel(a_ref, w_ref, x_ref, o_ref):
    o_ref[...] = x_ref[...] + _dot(a_ref[...].astype(bf16), w_ref[...])


def _proj_res(a, w, x, tm_target=512):
    m, k = a.shape
    d = w.shape[1]
    tm = _row_tile(m, tm_target)
    return pl.pallas_call(
        _proj_res_kernel,
        grid=(m // tm,),
        in_specs=[pl.BlockSpec((tm, k), lambda i: (i, 0)),
                  pl.BlockSpec((k, d), lambda i: (0, 0)),
                  pl.BlockSpec((tm, d), lambda i: (i, 0))],
        out_specs=pl.BlockSpec((tm, d), lambda i: (i, 0)),
        out_shape=jax.ShapeDtypeStruct((m, d), f32),
        compiler_params=_cparams("parallel"),
        name="proj_res",
    )(a, w, x)


def _nsa_proj_kernel(x_ref, g_ref, wq_ref, wkv_ref, wg_ref, q_ref, kv_ref, gt_ref, *slab_refs, n_pad):
    t = pl.program_id(1) - n_pad

    @pl.when(t >= 0)
    def _():
        xn = _rms(x_ref[...], g_ref[...]).astype(bf16)
        q_ref[...] = _dot(xn, wq_ref[...])
        kv = _dot(xn, wkv_ref[...])
        n_g = kv_ref.shape[0] // 3
        tm = kv.shape[0]
        for s in range(kv_ref.shape[0]):
            kv_ref[s] = kv[:, s * LANES:(s + 1) * LANES]
        gt = jax.nn.sigmoid(_dot(xn, wg_ref[...]))
        for s in range(gt_ref.shape[0]):
            gt_ref[s] = gt[:, s * LANES:(s + 1) * LANES]
        if slab_refs:
            ks_ref, vs_ref, kw_ref, vw_ref = slab_refs
            lane = lax.broadcasted_iota(i32, (tm, LANES), 1)
            lower = lane < HEAD_DIM
            blk = (t * tm + lax.broadcasted_iota(i32, (tm, LANES), 0)) // NSA_BLOCK
            onehot = jnp.where(lane - HEAD_DIM == blk, 1.0, 0.0)
            for gi in range(n_g):
                sel = kv[:, (n_g + gi) * LANES:(n_g + gi + 1) * LANES]
                win = kv[:, (2 * n_g + gi) * LANES:(2 * n_g + gi + 1) * LANES]
                ks_ref[gi, 0] = jnp.where(lower, sel, onehot).astype(bf16)
                vs_ref[gi, 0] = jnp.where(lower, 1.0, sel).astype(bf16)
                kw_ref[gi, 0] = jnp.where(lower, win, 0.0).astype(bf16)
                vw_ref[gi, 0] = jnp.where(lower, 1.0, win).astype(bf16)

    if slab_refs:
        @pl.when(t < 0)
        def _():
            for ref in slab_refs:
                ref[...] = jnp.zeros(ref.shape, ref.dtype)


def _nsa_proj(x, g, wq, wkv, wg, n_batch, slab_pad=None, tm_target=512):
    m, d = x.shape
    seq = m // n_batch
    nq, nkv, ng = wq.shape[1], wkv.shape[1], wg.shape[1]
    n_slab, n_gs = nkv // LANES, ng // LANES
    tm = _row_tile(seq, tm_target)
    n_t = seq // tm
    n_pad = 0 if slab_pad is None else slab_pad // tm

    def row(b, t):
        return b * n_t + jnp.maximum(t - n_pad, 0)

    out_specs = [pl.BlockSpec((tm, nq), lambda b, t: (row(b, t), 0)),
                 pl.BlockSpec((n_slab, tm, LANES), lambda b, t: (0, row(b, t), 0)),
                 pl.BlockSpec((n_gs, tm, LANES), lambda b, t: (0, row(b, t), 0))]
    out_shape = [jax.ShapeDtypeStruct((m, nq), f32),
                 jax.ShapeDtypeStruct((n_slab, m, LANES), f32),
                 jax.ShapeDtypeStruct((n_gs, m, LANES), f32)]
    if slab_pad is not None:
        assert slab_pad % tm == 0
        n_g = n_slab // 3
        out_specs += [pl.BlockSpec((n_g, 1, tm, LANES), lambda b, t: (0, b, t, 0))] * 4
        out_shape += [jax.ShapeDtypeStruct((n_g, n_batch, slab_pad + seq, LANES), bf16)] * 4
    return pl.pallas_call(
        functools.partial(_nsa_proj_kernel, n_pad=n_pad),
        grid=(n_batch, n_t + n_pad),
        in_specs=[pl.BlockSpec((tm, d), lambda b, t: (row(b, t), 0)),
                  pl.BlockSpec((1, d), lambda b, t: (0, 0)),
                  pl.BlockSpec((d, nq), lambda b, t: (0, 0)),
                  pl.BlockSpec((d, nkv), lambda b, t: (0, 0)),
                  pl.BlockSpec((d, ng), lambda b, t: (0, 0))],
        out_specs=out_specs,
        out_shape=out_shape,
        compiler_params=_cparams("parallel", "arbitrary"),
        name="nsa_proj",
    )(x, g.reshape(1, d), wq, wkv, wg)


def _compress_kernel(kv_ref, w_ref, o_ref):
    tm = kv_ref.shape[1]
    nb = tm // NSA_BLOCK
    x = kv_ref[0].reshape(nb, NSA_BLOCK, LANES)
    o_ref[0] = jnp.sum(x * w_ref[...][None], axis=1)


def _compress(kvp, w_pack, n_groups, tm_target=512):
    m = kvp.shape[1]
    tm = _row_tile(m, tm_target)
    nb = tm // NSA_BLOCK
    return pl.pallas_call(
        _compress_kernel,
        grid=(n_groups, m // tm),
        in_specs=[pl.BlockSpec((1, tm, LANES), lambda g, i: (g, i, 0)),
                  pl.BlockSpec((NSA_BLOCK, LANES), lambda g, i: (0, 0))],
        out_specs=pl.BlockSpec((1, nb, LANES), lambda g, i: (g, i, 0)),
        out_shape=jax.ShapeDtypeStruct((n_groups, m // NSA_BLOCK, LANES), f32),
        compiler_params=_cparams("parallel", "parallel"),
        name="nsa_compress",
    )(kvp, w_pack)


def _t5_bucket(rel):
    n = jnp.maximum(rel, 0)
    max_exact = REL_BUCKETS // 2
    nf = jnp.maximum(n, 1).astype(f32)
    large = max_exact + (jnp.log(nf / max_exact) / math.log(REL_MAX_DIST / max_exact)
                         * (REL_BUCKETS - max_exact)).astype(i32)
    large = jnp.minimum(large, REL_BUCKETS - 1)
    return jnp.where(n < max_exact, n, large)


def _bias_lookup(table_ref, bucket, head):
    out = jnp.zeros(bucket.shape, f32)
    for k in range(REL_BUCKETS):
        out = jnp.where(bucket == k, table_ref[k, head], out)
    return out


def _t5_prompt_kernel(table_ref, tsel_ref, twin_ref, cmpq_ref, cmpt_ref):
    g = pl.program_id(0)
    n_bc = cmpq_ref.shape[3]
    rel_s = (lax.broadcasted_iota(i32, (NSA_BLOCK, NEAR_TILE), 0) + (NEAR_TILE - NSA_BLOCK)
             - lax.broadcasted_iota(i32, (NSA_BLOCK, NEAR_TILE), 1))
    rel_w = (lax.broadcasted_iota(i32, (NSA_BLOCK, WIN_TILE), 0) + (WIN_TILE - NSA_BLOCK)
             - lax.broadcasted_iota(i32, (NSA_BLOCK, WIN_TILE), 1))
    bucket_s = _t5_bucket(rel_s)
    bucket_w = _t5_bucket(rel_w)
    qi_b = lax.broadcasted_iota(i32, (NSA_BLOCK, n_bc), 0)
    shp_t = (8, NSA_BLOCK)
    bucket_t = _t5_bucket(lax.broadcasted_iota(i32, shp_t, 1) - (NSA_BLOCK - 1)
                          + NSA_BLOCK * lax.broadcasted_iota(i32, shp_t, 0))
    for hh in range(NSA_HPG):
        head = g * NSA_HPG + hh
        far = table_ref[REL_BUCKETS - 1, head]
        rows = slice(hh * NSA_BLOCK, (hh + 1) * NSA_BLOCK)
        tsel_ref[0, rows, :] = jnp.where(rel_s >= 0, _bias_lookup(table_ref, bucket_s, head) - far, NEG_INF)
        twin_ref[0, rows, :] = jnp.where((rel_w >= 0) & (rel_w <= NSA_WINDOW),
                                         _bias_lookup(table_ref, bucket_w, head) - far, NEG_INF)
        for d in range(N_NEAR):
            cmpq_ref[0, d, rows, :] = _bias_lookup(
                table_ref, _t5_bucket(qi_b - (NSA_BLOCK - 1) + NSA_BLOCK * d), head) - far
        cmpt_ref[0, :, rows] = _bias_lookup(table_ref, bucket_t, head) - far


def _t5_prompt_tables(rel_bias, n_blk):
    rows = NSA_HPG * NSA_BLOCK
    return pl.pallas_call(
        _t5_prompt_kernel,
        grid=(NSA_KV_HEADS,),
        in_specs=[pl.BlockSpec(memory_space=pltpu.SMEM)],
        out_specs=[pl.BlockSpec((1, rows, NEAR_TILE), lambda g: (g, 0, 0)),
                   pl.BlockSpec((1, rows, WIN_TILE), lambda g: (g, 0, 0)),
                   pl.BlockSpec((1, N_NEAR, rows, n_blk), lambda g: (g, 0, 0, 0)),
                   pl.BlockSpec((1, 8, rows), lambda g: (g, 0, 0))],
        out_shape=[jax.ShapeDtypeStruct((NSA_KV_HEADS, rows, NEAR_TILE), f32),
                   jax.ShapeDtypeStruct((NSA_KV_HEADS, rows, WIN_TILE), f32),
                   jax.ShapeDtypeStruct((NSA_KV_HEADS, N_NEAR, rows, n_blk), f32),
                   jax.ShapeDtypeStruct((NSA_KV_HEADS, 8, rows), f32)],
        compiler_params=_cparams("parallel"),
        name="t5_prompt_tables",
    )(rel_bias)


def _t5_sample_kernel(table_ref, key_ref, blk_ref, *, qpos):
    kpos = lax.broadcasted_iota(i32, (1, key_ref.shape[1]), 1)
    bucket_k = _t5_bucket(qpos - kpos)
    blk = lax.broadcasted_iota(i32, (1, blk_ref.shape[1]), 1)
    bucket_b = _t5_bucket(qpos - (blk * NSA_BLOCK + NSA_BLOCK - 1))
    for r in range(NSA_HEADS):
        head = (r % NSA_KV_HEADS) * NSA_HPG + r // NSA_KV_HEADS
        key_ref[r:r + 1, :] = _bias_lookup(table_ref, bucket_k, head)
        blk_ref[r:r + 1, :] = _bias_lookup(table_ref, bucket_b, head)


def _t5_sample_tables(rel_bias, qpos, n_key, n_blk):
    return pl.pallas_call(
        functools.partial(_t5_sample_kernel, qpos=qpos),
        in_specs=[pl.BlockSpec(memory_space=pltpu.SMEM)],
        out_shape=[jax.ShapeDtypeStruct((NSA_HEADS, n_key), f32),
                   jax.ShapeDtypeStruct((NSA_HEADS, n_blk), f32)],
        name="t5_sample_tables",
    )(rel_bias)


def _flash_update(carry, s, v_aug):
    m, acc = carry
    m_new = jnp.maximum(m, jnp.max(s, axis=1, keepdims=True))
    p = jnp.exp(s - m_new).astype(bf16)
    return m_new, jnp.exp(m - m_new) * acc + _dot(p, v_aug)


def _flash_first(s, v_aug):
    m = jnp.max(s, axis=1, keepdims=True)
    return m, _dot(jnp.exp(s - m).astype(bf16), v_aug)


def _flash_finish(carry):
    _, acc = carry
    return acc / pltpu.roll(acc, HEAD_DIM, 1)


def _tile_rows(x, n):
    return jnp.concatenate([x] * n, axis=0)


def _select_blocks(imp_t, c, n_blk):
    blk = lax.broadcasted_iota(i32, imp_t.shape, 0)
    valid = blk <= c
    forced = (blk == 0) | (blk == c) | (blk == c - 1)

    blk_f = blk.astype(f32)
    work = jnp.where(valid & jnp.logical_not(forced), imp_t, -1.0)
    sel = forced.astype(f32)
    for _ in range(NSA_TOPK - 3):
        mx = jnp.max(work, axis=0, keepdims=True)
        first = jnp.min(jnp.where(work == mx, blk_f, float(n_blk)), axis=0, keepdims=True)
        pick = blk_f == first
        work = jnp.where(pick, -2.0, work)
        sel = jnp.where(pick, 1.0, sel)
    return jnp.where(c + 1 <= NSA_TOPK, valid.astype(f32), sel)


def _nsa_prompt_kernel(*refs, n_blk, gps):
    q_ref, gt_ref, kvc_ref = refs[0:3]
    slab_refs = [refs[3 + 4 * i:3 + 4 * (i + 1)] for i in range(gps)]
    tsel_ref, twin_ref, cmpq_ref, cmpt_ref, o_ref = refs[3 + 4 * gps:]
    c = pl.program_id(2)
    rows = NSA_HPG * NSA_BLOCK
    wq = NSA_HPG * HEAD_DIM
    fronts = [_nsa_prompt_front(c, q_ref, kvc_ref, slab_refs[i][2], slab_refs[i][3], twin_ref, cmpq_ref,
                                cmpt_ref, i, n_blk) for i in range(gps)]

    def far_body(jj, carries):
        out = []
        for f, (ks_ref, vs_ref, _, _), carry in zip(fronts, slab_refs, carries):
            start = pl.multiple_of(KV_PAD + jj * SEL_CHUNK, SEL_CHUNK // 8)
            s = _dot_nt(f[1], ks_ref[0, 0, pl.ds(start, SEL_CHUNK), :])
            out.append(_flash_update(carry, s, vs_ref[0, 0, pl.ds(start, SEL_CHUNK), :]))
        return tuple(out)

    n_far = jnp.maximum(c - (NEAR_BLOCKS - 1), 0)
    init = tuple((jnp.full((rows, 1), NEG_INF, f32), jnp.zeros((rows, LANES), f32)) for _ in range(gps))
    carries = lax.fori_loop(0, (n_far + SEL_CHUNK_BLOCKS - 1) // SEL_CHUNK_BLOCKS, far_body, init)

    lower = lax.broadcasted_iota(i32, (NSA_BLOCK, LANES), 1) < HEAD_DIM
    lane_g = lax.broadcasted_iota(i32, (NSA_BLOCK, LANES), 1)
    for i in range(gps):
        q_pad, _, near_hidden, (o_c, o_w) = fronts[i]
        ks_ref, vs_ref = slab_refs[i][0:2]
        start = pl.multiple_of((c + 1) * NSA_BLOCK + (KV_PAD - NEAR_TILE), NSA_BLOCK)
        s_near = (_dot_nt(q_pad, ks_ref[0, 0, pl.ds(start, NEAR_TILE), :]) + tsel_ref[i]
                  + _tile_rows(near_hidden, NSA_HPG))
        o_s = _flash_finish(_flash_update(carries[i], s_near, vs_ref[0, 0, pl.ds(start, NEAR_TILE), :]))
        gt = gt_ref[i]

        def gate(h, branch):
            return jnp.sum(jnp.where(lane_g == h * 3 + branch, gt, 0.0), axis=1, keepdims=True)

        heads = []
        for h in range(NSA_HPG):
            r = slice(h * NSA_BLOCK, (h + 1) * NSA_BLOCK)
            heads.append(gate(h, 0) * o_c[r] + gate(h, 1) * o_s[r] + gate(h, 2) * o_w[r])
        for pr in range(NSA_HPG // 2):
            o_ref[:, i * wq + pr * LANES:i * wq + (pr + 1) * LANES] = jnp.where(
                lower, pltpu.roll(heads[2 * pr], HEAD_DIM, 1), heads[2 * pr + 1])


def _nsa_prompt_front(c, q_ref, kvc_ref, kw_ref, vw_ref, twin_ref, cmpq_ref, cmpt_ref, i, n_blk):
    rows = NSA_HPG * NSA_BLOCK
    wq = NSA_HPG * HEAD_DIM
    scale = HEAD_DIM ** -0.5
    lower = lax.broadcasted_iota(i32, (NSA_BLOCK, LANES), 1) < HEAD_DIM

    q_parts = []
    for h in range(NSA_HPG):
        x = q_ref[:, i * wq + (h // 2) * LANES:i * wq + (h // 2 + 1) * LANES] * scale
        if h % 2:
            x = pltpu.roll(x, HEAD_DIM, 1)
        q_parts.append(jnp.where(lower, x, 0.0))
    q_pad = jnp.concatenate(q_parts, axis=0).astype(bf16)

    kvc = kvc_ref[i].astype(bf16)
    blk = lax.broadcasted_iota(i32, (rows, n_blk), 1)
    qi = lax.broadcasted_iota(i32, (rows, n_blk), 0) & (NSA_BLOCK - 1)
    mask_c = (blk < c) | ((blk == c) & (qi == NSA_BLOCK - 1))
    bias_c = jnp.where(blk == c, cmpq_ref[i, 0],
                       jnp.where(blk == c - 1, cmpq_ref[i, 1],
                                 jnp.where(blk == c - 2, cmpq_ref[i, 2], 0.0)))
    s_c = jnp.where(mask_c, _dot_nt(q_pad, kvc) + bias_c, NEG_INF)
    m_c = jnp.max(s_c, axis=1, keepdims=True)
    e_c = jnp.where(mask_c, jnp.exp(s_c - m_c), 0.0)
    l_c = jnp.sum(e_c, axis=1, keepdims=True)
    p_c = e_c / jnp.where(l_c > 0.0, l_c, 1.0)
    o_c = _dot(p_c.astype(bf16), kvc)

    blk_t = lax.broadcasted_iota(i32, (n_blk, rows), 0)
    qi_t = lax.broadcasted_iota(i32, (n_blk, rows), 1) & (NSA_BLOCK - 1)
    mask_t = (blk_t < c) | ((blk_t == c) & (qi_t == NSA_BLOCK - 1))
    cols = cmpt_ref[i]
    bias_t = jnp.where(blk_t == c, cols[0:1, :],
                       jnp.where(blk_t == c - 1, cols[1:2, :],
                                 jnp.where(blk_t == c - 2, cols[2:3, :], 0.0)))
    s_t = jnp.where(mask_t, _dot_nt(kvc, q_pad) + bias_t, NEG_INF)
    m_t = jnp.max(s_t, axis=0, keepdims=True)
    e_t = jnp.where(mask_t, jnp.exp(s_t - m_t), 0.0)
    l_t = jnp.sum(e_t, axis=0, keepdims=True)
    p_t = e_t / jnp.where(l_t > 0.0, l_t, 1.0)
    imp_t = p_t[:, 0:LANES]
    for pr in range(1, NSA_HPG // 2):
        imp_t = imp_t + p_t[:, pr * LANES:(pr + 1) * LANES]
    imp_t = imp_t + pltpu.roll(imp_t, NSA_BLOCK, 1)
    sel_t = _select_blocks(imp_t, c, n_blk).astype(bf16)
    eye = (lax.broadcasted_iota(i32, (NSA_BLOCK, LANES), 0)
           == lax.broadcasted_iota(i32, (NSA_BLOCK, LANES), 1)).astype(bf16)
    pieces = [jnp.zeros((HEAD_DIM, LANES), bf16), sel_t]
    if n_blk < LANES - HEAD_DIM:
        pieces.append(jnp.zeros((LANES - HEAD_DIM - n_blk, LANES), bf16))
    sel_hi = _dot_nt(eye, jnp.concatenate(pieces, axis=0))
    lane = lax.broadcasted_iota(i32, (NSA_BLOCK, LANES), 1)
    first_near = c - (NEAR_BLOCKS - 1)
    far_mask = jnp.where((sel_hi > 0.5) & (lane - HEAD_DIM < first_near), 0.0, NEG_INF)
    q_aug = jnp.concatenate([jnp.where(lower, x, far_mask) for x in q_parts], axis=0).astype(bf16)

    def sel_col(b):
        col = jnp.sum(jnp.where(lane == HEAD_DIM + b, sel_hi, 0.0), axis=1, keepdims=True)
        return jnp.broadcast_to(col, (NSA_BLOCK, NEAR_TILE))

    near_blk = lax.broadcasted_iota(i32, (NSA_BLOCK, NEAR_TILE), 1) >> 6
    has_prev = jnp.where(c >= 1, 1.0, 0.0)
    visible = jnp.where(near_blk == 0, sel_col(c - 3),
                        jnp.where(near_blk == 1, sel_col(c - 2), jnp.where(near_blk == 2, has_prev, 1.0)))
    near_hidden = jnp.where(visible > 0.5, 0.0, NEG_INF)

    start_w = pl.multiple_of((c + 1) * NSA_BLOCK + (KV_PAD - WIN_TILE), NSA_BLOCK)
    col_w = lax.broadcasted_iota(i32, (1, WIN_TILE), 1)
    before_start = jnp.where(col_w < WIN_TILE - (c + 1) * NSA_BLOCK, NEG_INF, 0.0)
    s_w = _dot_nt(q_pad, kw_ref[0, 0, pl.ds(start_w, WIN_TILE), :]) + twin_ref[i] + before_start
    o_w = _flash_finish(_flash_first(s_w, vw_ref[0, 0, pl.ds(start_w, WIN_TILE), :]))
    return q_pad, q_aug, near_hidden, (o_c, o_w)


def _nsa_prompt_attn(q, gates, kvc, slabs, tsel, twin, cmpq, cmpt, n_batch, seq, gps=4):
    m = q.shape[0]
    n_blk = seq // NSA_BLOCK
    assert n_blk <= HEAD_DIM, "the block one-hot lives in the 64 spare lanes of the key slab"
    g_ = NSA_KV_HEADS
    wq = NSA_HPG * HEAD_DIM
    slab = (1, 1) + slabs[0].shape[2:]

    def slab_spec(i):
        return pl.BlockSpec(slab, lambda b, gg, c: (gg * gps + i, b, 0, 0))

    return pl.pallas_call(
        functools.partial(_nsa_prompt_kernel, n_blk=n_blk, gps=gps),
        grid=(n_batch, g_ // gps, n_blk),
        in_specs=[
            pl.BlockSpec((NSA_BLOCK, gps * wq), lambda b, gg, c: (b * n_blk + c, gg)),
            pl.BlockSpec((gps, NSA_BLOCK, LANES), lambda b, gg, c: (gg, b * n_blk + c, 0)),
            pl.BlockSpec((gps, n_blk, LANES), lambda b, gg, c: (gg, b, 0)),
            *[slab_spec(i) for i in range(gps) for _ in slabs],
            pl.BlockSpec((gps,) + tsel.shape[1:], lambda b, gg, c: (gg, 0, 0)),
            pl.BlockSpec((gps,) + twin.shape[1:], lambda b, gg, c: (gg, 0, 0)),
            pl.BlockSpec((gps,) + cmpq.shape[1:], lambda b, gg, c: (gg, 0, 0, 0)),
            pl.BlockSpec((gps,) + cmpt.shape[1:], lambda b, gg, c: (gg, 0, 0)),
        ],
        out_specs=pl.BlockSpec((NSA_BLOCK, gps * wq), lambda b, gg, c: (b * n_blk + c, gg)),
        out_shape=jax.ShapeDtypeStruct((m, NSA_HEADS * HEAD_DIM), f32),
        compiler_params=_cparams("parallel", "parallel", "arbitrary"),
        name="nsa_prompt_attn",
    )(q, gates, kvc, *(list(slabs) * gps), tsel, twin, cmpq, cmpt)


def _group_lane_mask(shape):
    r = lax.broadcasted_iota(i32, shape, 0)
    ln = lax.broadcasted_iota(i32, shape, 1)
    return (ln >> 6) == (r & (NSA_KV_HEADS - 1))


def _fold_groups(x):
    x = jnp.where(_group_lane_mask(x.shape), x, 0.0)
    out = x[:, 0:LANES]
    for pr in range(1, x.shape[1] // LANES):
        out = out + x[:, pr * LANES:(pr + 1) * LANES]
    return out + pltpu.roll(out, HEAD_DIM, 1)


def _select_blocks_sample(imp, cur, n_slots):
    imp_t = imp.T
    blk = lax.broadcasted_iota(i32, imp_t.shape, 0)
    valid = blk <= cur
    forced = (blk == 0) | (blk == cur) | (blk == cur - 1)
    if cur + 1 <= NSA_TOPK:
        return valid.astype(f32).T
    blk_f = blk.astype(f32)
    work = jnp.where(valid & jnp.logical_not(forced), imp_t, -1.0)
    sel = forced.astype(f32)
    for _ in range(NSA_TOPK - 3):
        mx = jnp.max(work, axis=0, keepdims=True)
        first = jnp.min(jnp.where(work == mx, blk_f, float(n_slots)), axis=0, keepdims=True)
        pick = blk_f == first
        work = jnp.where(pick, -2.0, work)
        sel = jnp.where(pick, 1.0, sel)
    return sel.T


def _nsa_sample_kernel(*refs, n_pages, page, past_len):
    pt_ref = refs[0]
    page_refs = refs[1:1 + n_pages]
    (win_ref, q_ref, kvn_ref, gt_ref, wc_ref, bkey_ref, bblk_ref, emat_ref,
     o_ref, nwin_ref, kcv_scr) = refs[1 + n_pages:]
    del pt_ref
    b = pl.program_id(0)
    gd = NSA_KV_HEADS * HEAD_DIM
    bpp = page // NSA_BLOCK
    n_past_blk = past_len // NSA_BLOCK
    n_slots = kcv_scr.shape[0]
    cur = n_past_blk
    scale = HEAD_DIM ** -0.5

    kvn = kvn_ref[pl.ds(b, 1), :]
    qm = jnp.where(_group_lane_mask((NSA_HEADS, gd)), q_ref[0] * scale, 0.0)
    qm_b = qm.astype(bf16)

    wc = wc_ref[...]
    for k in range(n_pages):
        cmp_part = page_refs[k][0, 0, :, 0:2 * gd].reshape(bpp, NSA_BLOCK, 2 * gd)
        kcv_scr[k * bpp:(k + 1) * bpp, :] = jnp.sum(cmp_part * wc[None], axis=1)
    row8 = lax.broadcasted_iota(i32, (8, 2 * gd), 0)
    kcv_scr[n_past_blk:n_past_blk + 8, :] = jnp.where(row8 == 0, kvn[:, 0:2 * gd] * wc[0:1, :], 0.0)
    rest = n_slots - n_past_blk - 8
    kcv_scr[n_past_blk + 8:n_slots, :] = jnp.zeros((rest, 2 * gd), f32)
    kcv = kcv_scr[...]
    kc_b = kcv[:, 0:gd].astype(bf16)
    vc_b = kcv[:, gd:2 * gd].astype(bf16)

    blk = lax.broadcasted_iota(i32, (NSA_HEADS, n_slots), 1)
    mask_c = blk * NSA_BLOCK + (NSA_BLOCK - 1) <= past_len
    s_c = jnp.where(mask_c, _dot_nt(qm_b, kc_b) + bblk_ref[...], NEG_INF)
    m_c = jnp.max(s_c, axis=1, keepdims=True)
    e_c = jnp.where(mask_c, jnp.exp(s_c - m_c), 0.0)
    l_c = jnp.sum(e_c, axis=1, keepdims=True)
    p_c = e_c / jnp.where(l_c > 0.0, l_c, 1.0)
    o_c = _fold_groups(_dot(p_c.astype(bf16), vc_b))

    imp = p_c[0:NSA_KV_HEADS]
    for hh in range(1, NSA_HPG):
        imp = imp + p_c[hh * NSA_KV_HEADS:(hh + 1) * NSA_KV_HEADS]
    sel = _select_blocks_sample(jnp.concatenate([imp, jnp.zeros_like(imp)], axis=0), cur, n_slots)
    sel16 = jnp.concatenate([sel[0:NSA_KV_HEADS]] * NSA_HPG, axis=0).astype(bf16)

    k_new = kvn[:, 2 * gd:3 * gd]
    v_new = kvn[:, 3 * gd:4 * gd]
    hidden = (_dot(sel16, emat_ref[...]) - 1.0) * (-NEG_INF)
    b_self = bkey_ref[:, past_len:past_len + LANES][:, 0:1]
    s_parts = [_dot_nt(qm_b, page_refs[k][0, 0, :, 2 * gd:3 * gd].astype(bf16)) for k in range(n_pages)]
    s_s = jnp.concatenate(s_parts, axis=1) + bkey_ref[:, 0:past_len] + hidden
    s_self = jnp.sum(qm * k_new, axis=1, keepdims=True) + b_self
    m_s = jnp.maximum(jnp.max(s_s, axis=1, keepdims=True), s_self)
    e_s = jnp.exp(s_s - m_s)
    e_self = jnp.exp(s_self - m_s)
    l_s = jnp.sum(e_s, axis=1, keepdims=True) + e_self
    e_sb = e_s.astype(bf16)
    pv = e_self * v_new
    for k in range(n_pages):
        pv = pv + _dot(e_sb[:, k * page:(k + 1) * page], page_refs[k][0, 0, :, 3 * gd:4 * gd].astype(bf16))
    o_s = _fold_groups(pv) / l_s

    n_buf = win_ref.shape[2]
    kw_b = win_ref[0, 0, :, 0:gd].astype(bf16)
    vw_b = win_ref[0, 0, :, gd:2 * gd].astype(bf16)
    kw_new = kvn[:, 4 * gd:5 * gd]
    vw_new = kvn[:, 5 * gd:6 * gd]
    wpos = past_len - n_buf + lax.broadcasted_iota(i32, (NSA_HEADS, n_buf), 1)
    msk_w = (past_len - wpos <= NSA_WINDOW) & (wpos >= 0)
    s_w = jnp.where(msk_w, _dot_nt(qm_b, kw_b) + bkey_ref[:, past_len - n_buf:past_len], NEG_INF)
    sw_self = jnp.sum(qm * kw_new, axis=1, keepdims=True) + b_self
    m_w = jnp.maximum(jnp.max(s_w, axis=1, keepdims=True), sw_self)
    e_w = jnp.where(msk_w, jnp.exp(s_w - m_w), 0.0)
    ew_self = jnp.exp(sw_self - m_w)
    l_w = jnp.sum(e_w, axis=1, keepdims=True) + ew_self
    o_w = _fold_groups(_dot(e_w.astype(bf16), vw_b) + ew_self * vw_new) / l_w

    gt = gt_ref[0]
    lane_g = lax.broadcasted_iota(i32, gt.shape, 1)

    def gate(branch):
        return jnp.sum(jnp.where(lane_g == branch, gt, 0.0), axis=1, keepdims=True)

    o_ref[0] = gate(0) * o_c + gate(1) * o_s + gate(2) * o_w

    nwin_ref[0, 0, 0:n_buf - 1, :] = win_ref[0, 0, 1:n_buf, :]
    nwin_ref[0, 0, n_buf - 1:n_buf, :] = kvn[:, 4 * gd:6 * gd]


def _nsa_sample_attn(page_table, cache_kv, cache_win, layer, q_hg, kv_nat, gates_hg, wc_nat, bkey, bblk,
                     past_len):
    n_dec, n_pages = page_table.shape
    _, n_phys, page, _, g_, d_ = cache_kv.shape
    gd = g_ * d_
    n_buf = cache_win.shape[2]
    n_slots = bblk.shape[1]
    cache2 = cache_kv.reshape(cache_kv.shape[0], n_phys, page, 4 * gd)
    win2 = cache_win.reshape(cache_win.shape[0], n_dec, n_buf, 2 * gd)
    emat = (lax.broadcasted_iota(i32, (n_slots, past_len), 0)
            == lax.broadcasted_iota(i32, (n_slots, past_len), 1) // NSA_BLOCK).astype(bf16)

    def page_spec(k):
        return pl.BlockSpec((1, 1, page, 4 * gd), lambda b, pt: (layer, pt[b * n_pages + k], 0, 0))

    grid_spec = pltpu.PrefetchScalarGridSpec(
        num_scalar_prefetch=1,
        grid=(n_dec,),
        in_specs=[
            *[page_spec(k) for k in range(n_pages)],
            pl.BlockSpec((1, 1, n_buf, 2 * gd), lambda b, pt: (layer, b, 0, 0)),
            pl.BlockSpec((1, NSA_HEADS, gd), lambda b, pt: (b, 0, 0)),
            pl.BlockSpec(kv_nat.shape, lambda b, pt: (0, 0)),
            pl.BlockSpec((1, NSA_HEADS, LANES), lambda b, pt: (b, 0, 0)),
            pl.BlockSpec(wc_nat.shape, lambda b, pt: (0, 0)),
            pl.BlockSpec(bkey.shape, lambda b, pt: (0, 0)),
            pl.BlockSpec(bblk.shape, lambda b, pt: (0, 0)),
            pl.BlockSpec(emat.shape, lambda b, pt: (0, 0)),
        ],
        out_specs=[
            pl.BlockSpec((1, NSA_HEADS, LANES), lambda b, pt: (b, 0, 0)),
            pl.BlockSpec((1, 1, n_buf, 2 * gd), lambda b, pt: (0, b, 0, 0)),
        ],
        scratch_shapes=[pltpu.VMEM((n_slots, 2 * gd), f32)],
    )
    o, nwin = pl.pallas_call(
        functools.partial(_nsa_sample_kernel, n_pages=n_pages, page=page, past_len=past_len),
        grid_spec=grid_spec,
        out_shape=[jax.ShapeDtypeStruct((n_dec, NSA_HEADS, LANES), f32),
                   jax.ShapeDtypeStruct((1, n_dec, n_buf, 2 * gd), f32)],
        compiler_params=_cparams("parallel"),
        name="nsa_sample_attn",
    )(page_table.reshape(-1), *([cache2] * n_pages), win2, q_hg, kv_nat, gates_hg, wc_nat, bkey, bblk, emat)
    return o[:, :, :HEAD_DIM], nwin[0]


def _conv_prompt_kernel(x_ref, g_ref, win_ref, cw_ref, wout_ref, o_ref, st_ref, carry_ref, *, n_t):
    t = pl.program_id(1)
    tm, d = x_ref.shape

    @pl.when(t == 0)
    def _():
        carry_ref[...] = jnp.zeros_like(carry_ref)

    x = x_ref[...]
    xn = _rms(x, g_ref[...]).astype(bf16)
    pr = _dot(xn, win_ref[...])
    u = pr[:, d:2 * d] * pr[:, 0:d]
    bg = pr[:, 2 * d:3 * d]
    prev = carry_ref[...]
    row = lax.broadcasted_iota(i32, (tm, d), 0)
    u1 = jnp.where(row == 0, prev[7:8, :], pltpu.roll(u, 1, 0))
    u2 = jnp.where(row == 0, prev[6:7, :], jnp.where(row == 1, prev[7:8, :], pltpu.roll(u, 2, 0)))
    cw = cw_ref[...]
    v = cw[0:1, :] * u2 + cw[1:2, :] * u1 + cw[2:3, :] * u
    o_ref[...] = x + _dot((bg * v).astype(bf16), wout_ref[...])
    carry_ref[...] = u[tm - 8:tm, :]

    @pl.when(t == n_t - 1)
    def _():
        st_ref[0] = u[tm - (CONV_W - 1):tm, :]


def _conv_prompt(x, g, w_in, cw, w_out, n_batch, seq, tm_target=512):
    m, d = x.shape
    tm = _row_tile(seq, tm_target)
    n_t = seq // tm
    return pl.pallas_call(
        functools.partial(_conv_prompt_kernel, n_t=n_t),
        grid=(n_batch, n_t),
        in_specs=[pl.BlockSpec((tm, d), lambda b, t: (b * n_t + t, 0)),
                  pl.BlockSpec((1, d), lambda b, t: (0, 0)),
                  pl.BlockSpec((d, 3 * d), lambda b, t: (0, 0)),
                  pl.BlockSpec((CONV_W, d), lambda b, t: (0, 0)),
                  pl.BlockSpec((d, d), lambda b, t: (0, 0))],
        out_specs=[pl.BlockSpec((tm, d), lambda b, t: (b * n_t + t, 0)),
                   pl.BlockSpec((1, CONV_W - 1, d), lambda b, t: (b, 0, 0))],
        out_shape=[jax.ShapeDtypeStruct((m, d), f32),
                   jax.ShapeDtypeStruct((n_batch, CONV_W - 1, d), f32)],
        scratch_shapes=[pltpu.VMEM((8, d), f32)],
        compiler_params=_cparams("parallel", "arbitrary"),
        name="conv_prompt",
    )(x, g.reshape(1, d), w_in, cw, w_out)


def _conv_sample_kernel(x_ref, g_ref, st_ref, win_ref, cw_ref, wout_ref, o_ref, nst_ref):
    d = x_ref.shape[1]
    x = x_ref[...]
    xn = _rms(x, g_ref[...]).astype(bf16)
    pr = _dot(xn, win_ref[...])
    u = pr[:, d:2 * d] * pr[:, 0:d]
    bg = pr[:, 2 * d:3 * d]
    st = st_ref[...]
    cw = cw_ref[...]
    v = cw[0:1, :] * st[:, 0:d] + cw[1:2, :] * st[:, d:2 * d] + cw[2:3, :] * u
    o_ref[...] = x + _dot((bg * v).astype(bf16), wout_ref[...])
    nst_ref[:, 0:d] = st[:, d:2 * d]
    nst_ref[:, d:2 * d] = u


def _conv_sample(x, g, state, w_in, cw, w_out):
    m, d = x.shape
    o, nst = pl.pallas_call(
        _conv_sample_kernel,
        out_shape=[jax.ShapeDtypeStruct((m, d), f32),
                   jax.ShapeDtypeStruct((m, (CONV_W - 1) * d), f32)],
        compiler_params=pltpu.CompilerParams(vmem_limit_bytes=VMEM_LIMIT_BYTES),
        name="conv_sample",
    )(x, g.reshape(1, d), state.reshape(m, (CONV_W - 1) * d), w_in, cw, w_out)
    return o, nst.reshape(m, CONV_W - 1, d)


def _pool_mix(win_sum_groups, xn, cnt_groups, wgrp_ref, scale):
    gw = xn.shape[1] // len(POOL_WINDOWS)
    outs = []
    for gi in range(len(POOL_WINDOWS)):
        diff = win_sum_groups[gi] / cnt_groups[gi] - xn[:, gi * gw:(gi + 1) * gw]
        outs.append(_dot(diff.astype(bf16), wgrp_ref[gi]))
    return jnp.concatenate(outs, axis=1) * scale


def _pool_prompt_kernel(x_ref, g_ref, wgrp_ref, sc_ref, o_ref, st_ref, carry_ref, *, n_t):
    t = pl.program_id(1)
    tm, d = x_ref.shape
    gw = d // len(POOL_WINDOWS)
    hist = carry_ref.shape[0]

    @pl.when(t == 0)
    def _():
        carry_ref[...] = jnp.zeros_like(carry_ref)

    x = x_ref[...]
    xn = _rms(x, g_ref[...])
    a = jnp.concatenate([carry_ref[...], xn], axis=0)
    sums = []
    s = a
    shift = 1
    for gi, w in enumerate(POOL_WINDOWS):
        while shift < w:
            s = s + pltpu.roll(s, shift, 0)
            shift *= 2
        sums.append(s[hist:, gi * gw:(gi + 1) * gw])
    pos = t * tm + lax.broadcasted_iota(i32, (tm, gw), 0)
    cnts = [jnp.minimum(pos + 1, w).astype(f32) for w in POOL_WINDOWS]
    o_ref[...] = x + _pool_mix(sums, xn, cnts, wgrp_ref, sc_ref[...])
    carry_ref[...] = xn[tm - hist:tm, :]

    @pl.when(t == n_t - 1)
    def _():
        st_ref[0] = xn[tm - POOL_HIST:tm, :]


def _pool_prompt(x, g, w_grp, scale, n_batch, seq, tm_target=512):
    m, d = x.shape
    tm = _row_tile(seq, tm_target)
    n_t = seq // tm
    return pl.pallas_call(
        functools.partial(_pool_prompt_kernel, n_t=n_t),
        grid=(n_batch, n_t),
        in_specs=[pl.BlockSpec((tm, d), lambda b, t: (b * n_t + t, 0)),
                  pl.BlockSpec((1, d), lambda b, t: (0, 0)),
                  pl.BlockSpec(w_grp.shape, lambda b, t: (0, 0, 0)),
                  pl.BlockSpec((1, d), lambda b, t: (0, 0))],
        out_specs=[pl.BlockSpec((tm, d), lambda b, t: (b * n_t + t, 0)),
                   pl.BlockSpec((1, POOL_HIST, d), lambda b, t: (b, 0, 0))],
        out_shape=[jax.ShapeDtypeStruct((m, d), f32),
                   jax.ShapeDtypeStruct((n_batch, POOL_HIST, d), f32)],
        scratch_shapes=[pltpu.VMEM((POOL_HIST + 1, d), f32)],
        compiler_params=_cparams("parallel", "arbitrary"),
        name="pool_prompt",
    )(x, g.reshape(1, d), w_grp, scale.reshape(1, d))


def _pool_sample_kernel(x_ref, g_ref, st_ref, wgrp_ref, sc_ref, o_ref, nst_ref, *, pos):
    m, d = x_ref.shape
    gw = d // len(POOL_WINDOWS)
    x = x_ref[...]
    xn = _rms(x, g_ref[...])
    st = st_ref[...]
    sums, cnts = [], []
    s = xn
    back = 1
    for gi, w in enumerate(POOL_WINDOWS):
        while back < w:
            s = s + st[:, (POOL_HIST - back) * d:(POOL_HIST - back + 1) * d]
            back += 1
        sums.append(s[:, gi * gw:(gi + 1) * gw])
        cnts.append(jnp.full((m, gw), float(min(pos + 1, w)), f32))
    o_ref[...] = x + _pool_mix(sums, xn, cnts, wgrp_ref, sc_ref[...])
    nst_ref[:, 0:(POOL_HIST - 1) * d] = st[:, d:POOL_HIST * d]
    nst_ref[:, (POOL_HIST - 1) * d:POOL_HIST * d] = xn


def _pool_sample(x, g, state, w_grp, scale, pos):
    m, d = x.shape
    o, nst = pl.pallas_call(
        functools.partial(_pool_sample_kernel, pos=pos),
        out_shape=[jax.ShapeDtypeStruct((m, d), f32),
                   jax.ShapeDtypeStruct((m, POOL_HIST * d), f32)],
        compiler_params=pltpu.CompilerParams(vmem_limit_bytes=VMEM_LIMIT_BYTES),
        name="pool_sample",
    )(x, g.reshape(1, d), state.reshape(m, POOL_HIST * d), w_grp, scale.reshape(1, d))
    return o, nst.reshape(m, POOL_HIST, d)


def _ca_prompt_kernel(x_ref, g_ref, wq_ref, kv_ref, wo_ref, o_ref):
    d = x_ref.shape[1]
    hd = d // CA_HEADS
    x = x_ref[...]
    xn = _rms(x, g_ref[...]).astype(bf16)
    q = (_dot(xn, wq_ref[...]) * (hd ** -0.5)).astype(bf16)
    outs = []
    for h in range(CA_HEADS):
        k = kv_ref[:, h * hd:(h + 1) * hd].astype(bf16)
        v = kv_ref[:, d + h * hd:d + (h + 1) * hd].astype(bf16)
        s = _dot_nt(q[:, h * hd:(h + 1) * hd], k)
        e = jnp.exp(s - jnp.max(s, axis=1, keepdims=True))
        p = e / jnp.sum(e, axis=1, keepdims=True)
        outs.append(_dot(p.astype(bf16), v))
    o = jnp.concatenate(outs, axis=1).astype(bf16)
    o_ref[...] = x + _dot(o, wo_ref[...])


def _ca_prompt(x, g, wq, mkv, wo, n_batch, seq, mem_len, tm_target=512):
    m, d = x.shape
    tm = _row_tile(seq, tm_target)
    n_t = seq // tm
    return pl.pallas_call(
        _ca_prompt_kernel,
        grid=(n_batch, n_t),
        in_specs=[pl.BlockSpec((tm, d), lambda b, t: (b * n_t + t, 0)),
                  pl.BlockSpec((1, d), lambda b, t: (0, 0)),
                  pl.BlockSpec((d, d), lambda b, t: (0, 0)),
                  pl.BlockSpec((mem_len, 2 * d), lambda b, t: (b, 0)),
                  pl.BlockSpec((d, d), lambda b, t: (0, 0))],
        out_specs=pl.BlockSpec((tm, d), lambda b, t: (b * n_t + t, 0)),
        out_shape=jax.ShapeDtypeStruct((m, d), f32),
        compiler_params=_cparams("parallel", "parallel"),
        name="ca_prompt",
    )(x, g.reshape(1, d), wq, mkv, wo)


def _ca_sample_kernel(q_ref, kv_ref, o_ref):
    bb, _, hd = q_ref.shape
    scale = hd ** -0.5
    for bi in range(bb):
        q = q_ref[bi] * scale
        k = kv_ref[0, bi, :, 0]
        v = kv_ref[0, bi, :, 1]
        s = jnp.sum(k * q[None], axis=2, keepdims=True)
        e = jnp.exp(s - jnp.max(s, axis=0, keepdims=True))
        p = e / jnp.sum(e, axis=0, keepdims=True)
        o_ref[bi] = jnp.sum(p * v, axis=0)


def _ca_sample(q, cache_mem_kv, layer, bb=4):
    n_dec, d = q.shape
    _, _, mem, _, heads, hd = cache_mem_kv.shape
    o = pl.pallas_call(
        _ca_sample_kernel,
        grid=(n_dec // bb,),
        in_specs=[pl.BlockSpec((bb, heads, hd), lambda i: (i, 0, 0)),
                  pl.BlockSpec((1, bb, mem, 2, heads, hd), lambda i: (layer, i, 0, 0, 0, 0))],
        out_specs=pl.BlockSpec((bb, heads, hd), lambda i: (i, 0, 0)),
        out_shape=jax.ShapeDtypeStruct((n_dec, heads, hd), f32),
        compiler_params=_cparams("parallel"),
        name="ca_sample",
    )(q.reshape(n_dec, heads, hd), cache_mem_kv)
    return o.reshape(n_dec, d)


def _nsa_weights(w_in, w_cmp):
    d_model = w_in.shape[0]
    nq = NSA_HEADS * HEAD_DIM
    nkv = 6 * NSA_KV_HEADS * HEAD_DIM
    wq = w_in[:, :nq].astype(bf16)
    wkv = w_in[:, nq:nq + nkv].reshape(d_model, 3, 2, NSA_KV_HEADS, HEAD_DIM)
    wkv = wkv.transpose(0, 1, 3, 2, 4).reshape(d_model, nkv).astype(bf16)
    wg = w_in[:, nq + nkv:].reshape(d_model, NSA_KV_HEADS, NSA_HPG * 3)
    wg = jnp.pad(wg, ((0, 0), (0, 0), (0, LANES - NSA_HPG * 3))).reshape(d_model, NSA_KV_HEADS * LANES)
    w_pack = jnp.concatenate([w_cmp[0], w_cmp[1]], axis=1)
    wc_nat = jnp.concatenate([jnp.tile(w_cmp[0], (1, NSA_KV_HEADS)),
                              jnp.tile(w_cmp[1], (1, NSA_KV_HEADS))], axis=1)
    return wq, wkv, wg.astype(bf16), w_pack, wc_nat


def _kv_natural(kvp):
    m = kvp.shape[1]
    x = kvp.reshape(3, NSA_KV_HEADS, m, 2, HEAD_DIM)
    return x.transpose(2, 0, 3, 1, 4).reshape(m, 6 * NSA_KV_HEADS * HEAD_DIM)


def kernel(x_prompt, x_sample, cache_nsa_kv, cache_nsa_win, state_conv, state_pool, cache_mem_kv,
           page_table, mem_prompt, rel_bias, norm_g, norm_mem_g, final_g, w_ffn_in, w_ffn_out,
           w_nsa_in, w_nsa_cmp, w_nsa_out, w_conv_in, conv_w, w_conv_out, w_pool, pool_scale,
           w_ca_q, w_ca_kv, w_ca_out):
    n_b, seq, d = x_prompt.shape
    n_dec = x_sample.shape[0]
    depth = norm_g.shape[0]
    mem_len = mem_prompt.shape[1]
    page = cache_nsa_kv.shape[2]
    past_len = page_table.shape[1] * page
    g_, hd = NSA_KV_HEADS, HEAD_DIM
    gd = g_ * hd

    xp = x_prompt.reshape(n_b * seq, d)
    xs = x_sample.reshape(n_dec, d)
    mem = mem_prompt.reshape(n_b * mem_len, d)

    tsel, twin, cmpq, cmpt = _t5_prompt_tables(rel_bias, seq // NSA_BLOCK)
    n_blk_pad = -(-(past_len // NSA_BLOCK + 8) // LANES) * LANES
    bkey, bblk = _t5_sample_tables(rel_bias, past_len, past_len + LANES, n_blk_pad)

    nsa_kv_p, nsa_win_p, conv_p, pool_p, mem_p = [], [], [], [], []
    nsa_kv_s, nsa_win_s, conv_s, pool_s = [], [], [], []
    for i in range(depth):
        kind, j = i % 3, i // 3
        wf_in = w_ffn_in[i].astype(bf16)
        wf_out = w_ffn_out[i].astype(bf16)
        xp = _ffn(xp, norm_g[i, 0], wf_in[0], wf_out[0])
        xs = _ffn(xs, norm_g[i, 0], wf_in[0], wf_out[0])
        if kind == 0:
            wq, wkv, wg, w_pack, wc_nat = _nsa_weights(w_nsa_in[j], w_nsa_cmp[j])
            w_out = w_nsa_out[j].astype(bf16)
            q, kvp, gates, *slabs = _nsa_proj(xp, norm_g[i, 1], wq, wkv, wg, n_b, slab_pad=KV_PAD)
            kvc = _compress(kvp, w_pack, g_)
            o = _nsa_prompt_attn(q, gates, kvc, slabs, tsel, twin, cmpq, cmpt, n_b, seq)
            xp = _proj_res(o, w_out, xp)
            kv_nat = _kv_natural(kvp)
            nsa_kv_p.append(kv_nat[:, :4 * gd].reshape(n_b * seq // page, page, 4, g_, hd))
            n_keep = min(NSA_WINDOW, seq)
            nsa_win_p.append(kv_nat.reshape(n_b, seq, 6 * gd)[:, seq - n_keep:, 4 * gd:]
                             .reshape(n_b, n_keep, 2, g_, hd))
            q_s, kvp_s, gates_s = _nsa_proj(xs, norm_g[i, 1], wq, wkv, wg, 1)
            kv_nat_s = _kv_natural(kvp_s)
            q_hg = q_s.reshape(n_dec, g_, NSA_HPG, hd).transpose(0, 2, 1, 3).reshape(n_dec, NSA_HEADS, hd)
            q_hg = jnp.tile(q_hg, (1, 1, g_))
            gates_hg = gates_s.transpose(1, 0, 2)[:, :, :NSA_HPG * 3].reshape(n_dec, g_, NSA_HPG, 3)
            gates_hg = jnp.pad(gates_hg.transpose(0, 2, 1, 3).reshape(n_dec, NSA_HEADS, 3),
                               ((0, 0), (0, 0), (0, LANES - 3)))
            o_s, nwin = _nsa_sample_attn(page_table, cache_nsa_kv, cache_nsa_win, j, q_hg, kv_nat_s,
                                         gates_hg, wc_nat, bkey, bblk, past_len)
            o_s = o_s.reshape(n_dec, NSA_HPG, g_, hd).transpose(0, 2, 1, 3).reshape(n_dec, NSA_HEADS * hd)
            xs = _proj_res(o_s, w_out, xs)
            nsa_kv_s.append(kv_nat_s[:, :4 * gd].reshape(n_dec, 1, 4, g_, hd))
            nsa_win_s.append(nwin.reshape(n_dec, -1, 2, g_, hd))
        elif kind == 1:
            wc_in = w_conv_in[j].astype(bf16)
            wc_out = w_conv_out[j].astype(bf16)
            xp, cst = _conv_prompt(xp, norm_g[i, 1], wc_in, conv_w[j], wc_out, n_b, seq)
            xs, csts = _conv_sample(xs, norm_g[i, 1], state_conv[j], wc_in, conv_w[j], wc_out)
            conv_p.append(cst)
            conv_s.append(csts)
        else:
            wp = w_pool[j].astype(bf16)
            xp, pst = _pool_prompt(xp, norm_g[i, 1], wp, pool_scale[j], n_b, seq)
            xs, psts = _pool_sample(xs, norm_g[i, 1], state_pool[j], wp, pool_scale[j], past_len)
            pool_p.append(pst)
            pool_s.append(psts)
        w_q = w_ca_q[i].astype(bf16)
        w_o = w_ca_out[i].astype(bf16)
        mkv = _normproj(mem, norm_mem_g[i], w_ca_kv[i].astype(bf16))
        mem_p.append(mkv.reshape(n_b, mem_len, 2, CA_HEADS, d // CA_HEADS))
        xp = _ca_prompt(xp, norm_g[i, 2], w_q, mkv, w_o, n_b, seq, mem_len)
        q_ca = _normproj(xs, norm_g[i, 2], w_q)
        o_ca = _ca_sample(q_ca, cache_mem_kv, i)
        xs = _proj_res(o_ca, w_o, xs)
        fg = final_g if i == depth - 1 else None
        xp = _ffn(xp, norm_g[i, 3], wf_in[1], wf_out[1], final_g=fg)
        xs = _ffn(xs, norm_g[i, 3], wf_in[1], wf_out[1], final_g=fg)

    return (xp.reshape(n_b, seq, d), xs.reshape(n_dec, 1, d),
            jnp.stack(nsa_kv_p), jnp.stack(nsa_win_p), jnp.stack(conv_p), jnp.stack(pool_p),
            jnp.stack(mem_p),
            jnp.stack(nsa_kv_s), jnp.stack(nsa_win_s), jnp.stack(conv_s), jnp.stack(pool_s))
```

```python
import functools
import math

import jax
import jax.numpy as jnp
from jax import lax
from jax.experimental import pallas as pl
from jax.experimental.pallas import tpu as pltpu

f32 = jnp.float32
bf16 = jnp.bfloat16
i32 = jnp.int32

NORM_EPS = 1e-6
NEG_INF = -1e30

NSA_HEADS = 16
NSA_KV_HEADS = 4
NSA_HPG = NSA_HEADS // NSA_KV_HEADS
HEAD_DIM = 64
NSA_BLOCK = 64
NSA_TOPK = 16
NSA_WINDOW = 512
REL_BUCKETS = 32
REL_MAX_DIST = 128
CONV_W = 3
POOL_WINDOWS = (2, 4, 8, 16)
POOL_HIST = max(POOL_WINDOWS) - 1
CA_HEADS = 4

LANES = 128
VMEM_LIMIT_BYTES = 56 * 1024 * 1024

N_NEAR = 3
SEL_CHUNK_BLOCKS = 16
SEL_CHUNK = SEL_CHUNK_BLOCKS * NSA_BLOCK
NEAR_BLOCKS = 4
NEAR_TILE = NEAR_BLOCKS * NSA_BLOCK
WIN_TILE = 640
KV_PAD = 1024


def _cparams(*sem):
    return pltpu.CompilerParams(dimension_semantics=sem, vmem_limit_bytes=VMEM_LIMIT_BYTES)


def _rms(x, g):
    return x * lax.rsqrt(jnp.mean(x * x, axis=-1, keepdims=True) + NORM_EPS) * g


def _dot(a, b):
    return jnp.dot(a, b, preferred_element_type=f32)


def _dot_nt(a, b):
    return lax.dot_general(a, b, (((1,), (1,)), ((), ())), preferred_element_type=f32)


def _row_tile(m, target):
    t = min(m, target)
    while m % t:
        t //= 2
    return t


def _ffn_kernel(x_ref, g_ref, wg_ref, wu_ref, wo_ref, *rest, n_f, final_norm):
    if final_norm:
        gf_ref, o_ref, xn_ref, acc_ref = rest
    else:
        o_ref, xn_ref, acc_ref = rest
    j = pl.program_id(1)

    @pl.when(j == 0)
    def _():
        xn_ref[...] = _rms(x_ref[...], g_ref[...]).astype(bf16)
        acc_ref[...] = jnp.zeros_like(acc_ref)

    xn = xn_ref[...]
    gate = _dot(xn, wg_ref[...])
    up = _dot(xn, wu_ref[...])
    act = (gate * jax.nn.sigmoid(gate) * up).astype(bf16)
    acc_ref[...] += _dot(act, wo_ref[...])

    @pl.when(j == n_f - 1)
    def _():
        y = x_ref[...] + 0.5 * acc_ref[...]
        if final_norm:
            y = _rms(y, gf_ref[...])
        o_ref[...] = y


def _ffn(x, g, w_in, w_out, final_g=None, tm_target=512):
    m, d = x.shape
    f = w_out.shape[0]
    tm = _row_tile(m, tm_target)
    tf = f
    for cand in (1408, 1024, 512, 256, 128):
        if f % cand == 0:
            tf = cand
            break
    n_f = f // tf
    in_specs = [
        pl.BlockSpec((tm, d), lambda i, j: (i, 0)),
        pl.BlockSpec((1, d), lambda i, j: (0, 0)),
        pl.BlockSpec((d, tf), lambda i, j: (0, j)),
        pl.BlockSpec((d, tf), lambda i, j: (0, n_f + j)),
        pl.BlockSpec((tf, d), lambda i, j: (j, 0)),
    ]
    args = [x, g.reshape(1, d), w_in, w_in, w_out]
    if final_g is not None:
        in_specs.append(pl.BlockSpec((1, d), lambda i, j: (0, 0)))
        args.append(final_g.reshape(1, d))
    return pl.pallas_call(
        functools.partial(_ffn_kernel, n_f=n_f, final_norm=final_g is not None),
        grid=(m // tm, n_f),
        in_specs=in_specs,
        out_specs=pl.BlockSpec((tm, d), lambda i, j: (i, 0)),
        out_shape=jax.ShapeDtypeStruct((m, d), f32),
        scratch_shapes=[pltpu.VMEM((tm, d), bf16), pltpu.VMEM((tm, d), f32)],
        compiler_params=_cparams("parallel", "arbitrary"),
        name="ffn",
    )(*args)


def _normproj_kernel(x_ref, g_ref, w_ref, o_ref):
    xn = _rms(x_ref[...], g_ref[...]).astype(bf16)
    o_ref[...] = _dot(xn, w_ref[...])


def _normproj(x, g, w, tm_target=512):
    m, d = x.shape
    n = w.shape[1]
    tm = _row_tile(m, tm_target)
    return pl.pallas_call(
        _normproj_kernel,
        grid=(m // tm,),
        in_specs=[pl.BlockSpec((tm, d), lambda i: (i, 0)),
                  pl.BlockSpec((1, d), lambda i: (0, 0)),
                  pl.BlockSpec((d, n), lambda i: (0, 0))],
        out_specs=pl.BlockSpec((tm, n), lambda i: (i, 0)),
        out_shape=jax.ShapeDtypeStruct((m, n), f32),
        compiler_params=_cparams("parallel"),
        name="normproj",
    )(x, g.reshape(1, d), w)


def _proj_res_kernel(a_ref, w_ref, x_ref, o_ref):
    o_ref[...] = x_ref[...] + _dot(a_ref[...].astype(bf16), w_ref[...])


def _proj_res(a, w, x, tm_target=512):
    m, k = a.shape
    d = w.shape[1]
    tm = _row_tile(m, tm_target)
    return pl.pallas_call(
        _proj_res_kernel,
        grid=(m // tm,),
        in_specs=[pl.BlockSpec((tm, k), lambda i: (i, 0)),
                  pl.BlockSpec((k, d), lambda i: (0, 0)),
                  pl.BlockSpec((tm, d), lambda i: (i, 0))],
        out_specs=pl.BlockSpec((tm, d), lambda i: (i, 0)),
        out_shape=jax.ShapeDtypeStruct((m, d), f32),
        compiler_params=_cparams("parallel"),
        name="proj_res",
    )(a, w, x)


def _nsa_proj_kernel(x_ref, g_ref, wq_ref, wkv_ref, wg_ref, q_ref, kv_ref, gt_ref, *slab_refs, n_pad):
    t = pl.program_id(1) - n_pad

    @pl.when(t >= 0)
    def _():
        xn = _rms(x_ref[...], g_ref[...]).astype(bf16)
        q_ref[...] = _dot(xn, wq_ref[...])
        kv = _dot(xn, wkv_ref[...])
        n_g = kv_ref.shape[0] // 3
        tm = kv.shape[0]
        for s in range(kv_ref.shape[0]):
            kv_ref[s] = kv[:, s * LANES:(s + 1) * LANES]
        gt = jax.nn.sigmoid(_dot(xn, wg_ref[...]))
        for s in range(gt_ref.shape[0]):
            gt_ref[s] = gt[:, s * LANES:(s + 1) * LANES]
        if slab_refs:
            ks_ref, vs_ref, kw_ref, vw_ref = slab_refs
            lane = lax.broadcasted_iota(i32, (tm, LANES), 1)
            lower = lane < HEAD_DIM
            blk = (t * tm + lax.broadcasted_iota(i32, (tm, LANES), 0)) // NSA_BLOCK
            onehot = jnp.where(lane - HEAD_DIM == blk, 1.0, 0.0)
            for gi in range(n_g):
                sel = kv[:, (n_g + gi) * LANES:(n_g + gi + 1) * LANES]
                win = kv[:, (2 * n_g + gi) * LANES:(2 * n_g + gi + 1) * LANES]
                ks_ref[gi, 0] = jnp.where(lower, sel, onehot).astype(bf16)
                vs_ref[gi, 0] = jnp.where(lower, 1.0, sel).astype(bf16)
                kw_ref[gi, 0] = jnp.where(lower, win, 0.0).astype(bf16)
                vw_ref[gi, 0] = jnp.where(lower, 1.0, win).astype(bf16)

    if slab_refs:
        @pl.when(t < 0)
        def _():
            for ref in slab_refs:
                ref[...] = jnp.zeros(ref.shape, ref.dtype)


def _nsa_proj(x, g, wq, wkv, wg, n_batch, slab_pad=None, tm_target=512):
    m, d = x.shape
    seq = m // n_batch
    nq, nkv, ng = wq.shape[1], wkv.shape[1], wg.shape[1]
    n_slab, n_gs = nkv // LANES, ng // LANES
    tm = _row_tile(seq, tm_target)
    n_t = seq // tm
    n_pad = 0 if slab_pad is None else slab_pad // tm

    def row(b, t):
        return b * n_t + jnp.maximum(t - n_pad, 0)

    out_specs = [pl.BlockSpec((tm, nq), lambda b, t: (row(b, t), 0)),
                 pl.BlockSpec((n_slab, tm, LANES), lambda b, t: (0, row(b, t), 0)),
                 pl.BlockSpec((n_gs, tm, LANES), lambda b, t: (0, row(b, t), 0))]
    out_shape = [jax.ShapeDtypeStruct((m, nq), f32),
                 jax.ShapeDtypeStruct((n_slab, m, LANES), f32),
                 jax.ShapeDtypeStruct((n_gs, m, LANES), f32)]
    if slab_pad is not None:
        assert slab_pad % tm == 0
        n_g = n_slab // 3
        out_specs += [pl.BlockSpec((n_g, 1, tm, LANES), lambda b, t: (0, b, t, 0))] * 4
        out_shape += [jax.ShapeDtypeStruct((n_g, n_batch, slab_pad + seq, LANES), bf16)] * 4
    return pl.pallas_call(
        functools.partial(_nsa_proj_kernel, n_pad=n_pad),
        grid=(n_batch, n_t + n_pad),
        in_specs=[pl.BlockSpec((tm, d), lambda b, t: (row(b, t), 0)),
                  pl.BlockSpec((1, d), lambda b, t: (0, 0)),
                  pl.BlockSpec((d, nq), lambda b, t: (0, 0)),
                  pl.BlockSpec((d, nkv), lambda b, t: (0, 0)),
                  pl.BlockSpec((d, ng), lambda b, t: (0, 0))],
        out_specs=out_specs,
        out_shape=out_shape,
        compiler_params=_cparams("parallel", "arbitrary"),
        name="nsa_proj",
    )(x, g.reshape(1, d), wq, wkv, wg)


def _compress_kernel(kv_ref, w_ref, o_ref):
    tm = kv_ref.shape[1]
    nb = tm // NSA_BLOCK
    x = kv_ref[0].reshape(nb, NSA_BLOCK, LANES)
    o_ref[0] = jnp.sum(x * w_ref[...][None], axis=1)


def _compress(kvp, w_pack, n_groups, tm_target=512):
    m = kvp.shape[1]
    tm = _row_tile(m, tm_target)
    nb = tm // NSA_BLOCK
    return pl.pallas_call(
        _compress_kernel,
        grid=(n_groups, m // tm),
        in_specs=[pl.BlockSpec((1, tm, LANES), lambda g, i: (g, i, 0)),
                  pl.BlockSpec((NSA_BLOCK, LANES), lambda g, i: (0, 0))],
        out_specs=pl.BlockSpec((1, nb, LANES), lambda g, i: (g, i, 0)),
        out_shape=jax.ShapeDtypeStruct((n_groups, m // NSA_BLOCK, LANES), f32),
        compiler_params=_cparams("parallel", "parallel"),
        name="nsa_compress",
    )(kvp, w_pack)


def _t5_bucket(rel):
    n = jnp.maximum(rel, 0)
    max_exact = REL_BUCKETS // 2
    nf = jnp.maximum(n, 1).astype(f32)
    large = max_exact + (jnp.log(nf / max_exact) / math.log(REL_MAX_DIST / max_exact)
                         * (REL_BUCKETS - max_exact)).astype(i32)
    large = jnp.minimum(large, REL_BUCKETS - 1)
    return jnp.where(n < max_exact, n, large)


def _bias_lookup(table_ref, bucket, head):
    out = jnp.zeros(bucket.shape, f32)
    for k in range(REL_BUCKETS):
        out = jnp.where(bucket == k, table_ref[k, head], out)
    return out


def _t5_prompt_kernel(table_ref, tsel_ref, twin_ref, cmpq_ref, cmpt_ref):
    g = pl.program_id(0)
    n_bc = cmpq_ref.shape[3]
    rel_s = (lax.broadcasted_iota(i32, (NSA_BLOCK, NEAR_TILE), 0) + (NEAR_TILE - NSA_BLOCK)
             - lax.broadcasted_iota(i32, (NSA_BLOCK, NEAR_TILE), 1))
    rel_w = (lax.broadcasted_iota(i32, (NSA_BLOCK, WIN_TILE), 0) + (WIN_TILE - NSA_BLOCK)
             - lax.broadcasted_iota(i32, (NSA_BLOCK, WIN_TILE), 1))
    bucket_s = _t5_bucket(rel_s)
    bucket_w = _t5_bucket(rel_w)
    qi_b = lax.broadcasted_iota(i32, (NSA_BLOCK, n_bc), 0)
    shp_t = (8, NSA_BLOCK)
    bucket_t = _t5_bucket(lax.broadcasted_iota(i32, shp_t, 1) - (NSA_BLOCK - 1)
                          + NSA_BLOCK * lax.broadcasted_iota(i32, shp_t, 0))
    for hh in range(NSA_HPG):
        head = g * NSA_HPG + hh
        far = table_ref[REL_BUCKETS - 1, head]
        rows = slice(hh * NSA_BLOCK, (hh + 1) * NSA_BLOCK)
        tsel_ref[0, rows, :] = jnp.where(rel_s >= 0, _bias_lookup(table_ref, bucket_s, head) - far, NEG_INF)
        twin_ref[0, rows, :] = jnp.where((rel_w >= 0) & (rel_w <= NSA_WINDOW),
                                         _bias_lookup(table_ref, bucket_w, head) - far, NEG_INF)
        for d in range(N_NEAR):
            cmpq_ref[0, d, rows, :] = _bias_lookup(
                table_ref, _t5_bucket(qi_b - (NSA_BLOCK - 1) + NSA_BLOCK * d), head) - far
        cmpt_ref[0, :, rows] = _bias_lookup(table_ref, bucket_t, head) - far


def _t5_prompt_tables(rel_bias, n_blk):
    rows = NSA_HPG * NSA_BLOCK
    return pl.pallas_call(
        _t5_prompt_kernel,
        grid=(NSA_KV_HEADS,),
        in_specs=[pl.BlockSpec(memory_space=pltpu.SMEM)],
        out_specs=[pl.BlockSpec((1, rows, NEAR_TILE), lambda g: (g, 0, 0)),
                   pl.BlockSpec((1, rows, WIN_TILE), lambda g: (g, 0, 0)),
                   pl.BlockSpec((1, N_NEAR, rows, n_blk), lambda g: (g, 0, 0, 0)),
                   pl.BlockSpec((1, 8, rows), lambda g: (g, 0, 0))],
        out_shape=[jax.ShapeDtypeStruct((NSA_KV_HEADS, rows, NEAR_TILE), f32),
                   jax.ShapeDtypeStruct((NSA_KV_HEADS, rows, WIN_TILE), f32),
                   jax.ShapeDtypeStruct((NSA_KV_HEADS, N_NEAR, rows, n_blk), f32),
                   jax.ShapeDtypeStruct((NSA_KV_HEADS, 8, rows), f32)],
        compiler_params=_cparams("parallel"),
        name="t5_prompt_tables",
    )(rel_bias)


def _t5_sample_kernel(table_ref, key_ref, blk_ref, *, qpos):
    kpos = lax.broadcasted_iota(i32, (1, key_ref.shape[1]), 1)
    bucket_k = _t5_bucket(qpos - kpos)
    blk = lax.broadcasted_iota(i32, (1, blk_ref.shape[1]), 1)
    bucket_b = _t5_bucket(qpos - (blk * NSA_BLOCK + NSA_BLOCK - 1))
    for r in range(NSA_HEADS):
        head = (r % NSA_KV_HEADS) * NSA_HPG + r // NSA_KV_HEADS
        key_ref[r:r + 1, :] = _bias_lookup(table_ref, bucket_k, head)
        blk_ref[r:r + 1, :] = _bias_lookup(table_ref, bucket_b, head)


def _t5_sample_tables(rel_bias, qpos, n_key, n_blk):
    return pl.pallas_call(
        functools.partial(_t5_sample_kernel, qpos=qpos),
        in_specs=[pl.BlockSpec(memory_space=pltpu.SMEM)],
        out_shape=[jax.ShapeDtypeStruct((NSA_HEADS, n_key), f32),
                   jax.ShapeDtypeStruct((NSA_HEADS, n_blk), f32)],
        name="t5_sample_tables",
    )(rel_bias)


def _flash_update(carry, s, v_aug):
    m, acc = carry
    m_new = jnp.maximum(m, jnp.max(s, axis=1, keepdims=True))
    p = jnp.exp(s - m_new).astype(bf16)
    return m_new, jnp.exp(m - m_new) * acc + _dot(p, v_aug)


def _flash_first(s, v_aug):
    m = jnp.max(s, axis=1, keepdims=True)
    return m, _dot(jnp.exp(s - m).astype(bf16), v_aug)


def _flash_finish(carry):
    _, acc = carry
    return acc / pltpu.roll(acc, HEAD_DIM, 1)


def _tile_rows(x, n):
    return jnp.concatenate([x] * n, axis=0)


def _select_blocks(imp_t, c, n_blk):
    blk = lax.broadcasted_iota(i32, imp_t.shape, 0)
    valid = blk <= c
    forced = (blk == 0) | (blk == c) | (blk == c - 1)

    blk_f = blk.astype(f32)
    work = jnp.where(valid & jnp.logical_not(forced), imp_t, -1.0)
    sel = forced.astype(f32)
    for _ in range(NSA_TOPK - 3):
        mx = jnp.max(work, axis=0, keepdims=True)
        first = jnp.min(jnp.where(work == mx, blk_f, float(n_blk)), axis=0, keepdims=True)
        pick = blk_f == first
        work = jnp.where(pick, -2.0, work)
        sel = jnp.where(pick, 1.0, sel)
    return jnp.where(c + 1 <= NSA_TOPK, valid.astype(f32), sel)


def _nsa_prompt_kernel(*refs, n_blk, gps):
    q_ref, gt_ref, kvc_ref = refs[0:3]
    slab_refs = [refs[3 + 4 * i:3 + 4 * (i + 1)] for i in range(gps)]
    tsel_ref, twin_ref, cmpq_ref, cmpt_ref, o_ref = refs[3 + 4 * gps:]
    c = pl.program_id(2)
    rows = NSA_HPG * NSA_BLOCK
    wq = NSA_HPG * HEAD_DIM
    fronts = [_nsa_prompt_front(c, q_ref, kvc_ref, slab_refs[i][2], slab_refs[i][3], twin_ref, cmpq_ref,
                                cmpt_ref, i, n_blk) for i in range(gps)]

    def far_body(jj, carries):
        out = []
        for f, (ks_ref, vs_ref, _, _), carry in zip(fronts, slab_refs, carries):
            start = pl.multiple_of(KV_PAD + jj * SEL_CHUNK, SEL_CHUNK // 8)
            s = _dot_nt(f[1], ks_ref[0, 0, pl.ds(start, SEL_CHUNK), :])
            out.append(_flash_update(carry, s, vs_ref[0, 0, pl.ds(start, SEL_CHUNK), :]))
        return tuple(out)

    n_far = jnp.maximum(c - (NEAR_BLOCKS - 1), 0)
    init = tuple((jnp.full((rows, 1), NEG_INF, f32), jnp.zeros((rows, LANES), f32)) for _ in range(gps))
    carries = lax.fori_loop(0, (n_far + SEL_CHUNK_BLOCKS - 1) // SEL_CHUNK_BLOCKS, far_body, init)

    lower = lax.broadcasted_iota(i32, (NSA_BLOCK, LANES), 1) < HEAD_DIM
    lane_g = lax.broadcasted_iota(i32, (NSA_BLOCK, LANES), 1)
    for i in range(gps):
        q_pad, _, near_hidden, (o_c, o_w) = fronts[i]
        ks_ref, vs_ref = slab_refs[i][0:2]
        start = pl.multiple_of((c + 1) * NSA_BLOCK + (KV_PAD - NEAR_TILE), NSA_BLOCK)
        s_near = (_dot_nt(q_pad, ks_ref[0, 0, pl.ds(start, NEAR_TILE), :]) + tsel_ref[i]
                  + _tile_rows(near_hidden, NSA_HPG))
        o_s = _flash_finish(_flash_update(carries[i], s_near, vs_ref[0, 0, pl.ds(start, NEAR_TILE), :]))
        gt = gt_ref[i]

        def gate(h, branch):
            return jnp.sum(jnp.where(lane_g == h * 3 + branch, gt, 0.0), axis=1, keepdims=True)

        heads = []
        for h in range(NSA_HPG):
            r = slice(h * NSA_BLOCK, (h + 1) * NSA_BLOCK)
            heads.append(gate(h, 0) * o_c[r] + gate(h, 1) * o_s[r] + gate(h, 2) * o_w[r])
        for pr in range(NSA_HPG // 2):
            o_ref[:, i * wq + pr * LANES:i * wq + (pr + 1) * LANES] = jnp.where(
                lower, pltpu.roll(heads[2 * pr], HEAD_DIM, 1), heads[2 * pr + 1])


def _nsa_prompt_front(c, q_ref, kvc_ref, kw_ref, vw_ref, twin_ref, cmpq_ref, cmpt_ref, i, n_blk):
    rows = NSA_HPG * NSA_BLOCK
    wq = NSA_HPG * HEAD_DIM
    scale = HEAD_DIM ** -0.5
    lower = lax.broadcasted_iota(i32, (NSA_BLOCK, LANES), 1) < HEAD_DIM

    q_parts = []
    for h in range(NSA_HPG):
        x = q_ref[:, i * wq + (h // 2) * LANES:i * wq + (h // 2 + 1) * LANES] * scale
        if h % 2:
            x = pltpu.roll(x, HEAD_DIM, 1)
        q_parts.append(jnp.where(lower, x, 0.0))
    q_pad = jnp.concatenate(q_parts, axis=0).astype(bf16)

    kvc = kvc_ref[i].astype(bf16)
    blk = lax.broadcasted_iota(i32, (rows, n_blk), 1)
    qi = lax.broadcasted_iota(i32, (rows, n_blk), 0) & (NSA_BLOCK - 1)
    mask_c = (blk < c) | ((blk == c) & (qi == NSA_BLOCK - 1))
    bias_c = jnp.where(blk == c, cmpq_ref[i, 0],
                       jnp.where(blk == c - 1, cmpq_ref[i, 1],
                                 jnp.where(blk == c - 2, cmpq_ref[i, 2], 0.0)))
    s_c = jnp.where(mask_c, _dot_nt(q_pad, kvc) + bias_c, NEG_INF)
    m_c = jnp.max(s_c, axis=1, keepdims=True)
    e_c = jnp.where(mask_c, jnp.exp(s_c - m_c), 0.0)
    l_c = jnp.sum(e_c, axis=1, keepdims=True)
    p_c = e_c / jnp.where(l_c > 0.0, l_c, 1.0)
    o_c = _dot(p_c.astype(bf16), kvc)

    blk_t = lax.broadcasted_iota(i32, (n_blk, rows), 0)
    qi_t = lax.broadcasted_iota(i32, (n_blk, rows), 1) & (NSA_BLOCK - 1)
    mask_t = (blk_t < c) | ((blk_t == c) & (qi_t == NSA_BLOCK - 1))
    cols = cmpt_ref[i]
    bias_t = jnp.where(blk_t == c, cols[0:1, :],
                       jnp.where(blk_t == c - 1, cols[1:2, :],
                                 jnp.where(blk_t == c - 2, cols[2:3, :], 0.0)))
    s_t = jnp.where(mask_t, _dot_nt(kvc, q_pad) + bias_t, NEG_INF)
    m_t = jnp.max(s_t, axis=0, keepdims=True)
    e_t = jnp.where(mask_t, jnp.exp(s_t - m_t), 0.0)
    l_t = jnp.sum(e_t, axis=0, keepdims=True)
    p_t = e_t / jnp.where(l_t > 0.0, l_t, 1.0)
    imp_t = p_t[:, 0:LANES]
    for pr in range(1, NSA_HPG // 2):
        imp_t = imp_t + p_t[:, pr * LANES:(pr + 1) * LANES]
    imp_t = imp_t + pltpu.roll(imp_t, NSA_BLOCK, 1)
    sel_t = _select_blocks(imp_t, c, n_blk).astype(bf16)
    eye = (lax.broadcasted_iota(i32, (NSA_BLOCK, LANES), 0)
           == lax.broadcasted_iota(i32, (NSA_BLOCK, LANES), 1)).astype(bf16)
    pieces = [jnp.zeros((HEAD_DIM, LANES), bf16), sel_t]
    if n_blk < LANES - HEAD_DIM:
        pieces.append(jnp.zeros((LANES - HEAD_DIM - n_blk, LANES), bf16))
    sel_hi = _dot_nt(eye, jnp.concatenate(pieces, axis=0))
    lane = lax.broadcasted_iota(i32, (NSA_BLOCK, LANES), 1)
    first_near = c - (NEAR_BLOCKS - 1)
    far_mask = jnp.where((sel_hi > 0.5) & (lane - HEAD_DIM < first_near), 0.0, NEG_INF)
    q_aug = jnp.concatenate([jnp.where(lower, x, far_mask) for x in q_parts], axis=0).astype(bf16)

    def sel_col(b):
        col = jnp.sum(jnp.where(lane == HEAD_DIM + b, sel_hi, 0.0), axis=1, keepdims=True)
        return jnp.broadcast_to(col, (NSA_BLOCK, NEAR_TILE))

    near_blk = lax.broadcasted_iota(i32, (NSA_BLOCK, NEAR_TILE), 1) >> 6
    has_prev = jnp.where(c >= 1, 1.0, 0.0)
    visible = jnp.where(near_blk == 0, sel_col(c - 3),
                        jnp.where(near_blk == 1, sel_col(c - 2), jnp.where(near_blk == 2, has_prev, 1.0)))
    near_hidden = jnp.where(visible > 0.5, 0.0, NEG_INF)

    start_w = pl.multiple_of((c + 1) * NSA_BLOCK + (KV_PAD - WIN_TILE), NSA_BLOCK)
    col_w = lax.broadcasted_iota(i32, (1, WIN_TILE), 1)
    before_start = jnp.where(col_w < WIN_TILE - (c + 1) * NSA_BLOCK, NEG_INF, 0.0)
    s_w = _dot_nt(q_pad, kw_ref[0, 0, pl.ds(start_w, WIN_TILE), :]) + twin_ref[i] + before_start
    o_w = _flash_finish(_flash_first(s_w, vw_ref[0, 0, pl.ds(start_w, WIN_TILE), :]))
    return q_pad, q_aug, near_hidden, (o_c, o_w)


def _nsa_prompt_attn(q, gates, kvc, slabs, tsel, twin, cmpq, cmpt, n_batch, seq, gps=4):
    m = q.shape[0]
    n_blk = seq // NSA_BLOCK
    assert n_blk <= HEAD_DIM, "the block one-hot lives in the 64 spare lanes of the key slab"
    g_ = NSA_KV_HEADS
    wq = NSA_HPG * HEAD_DIM
    slab = (1, 1) + slabs[0].shape[2:]

    def slab_spec(i):
        return pl.BlockSpec(slab, lambda b, gg, c: (gg * gps + i, b, 0, 0))

    return pl.pallas_call(
        functools.partial(_nsa_prompt_kernel, n_blk=n_blk, gps=gps),
        grid=(n_batch, g_ // gps, n_blk),
        in_specs=[
            pl.BlockSpec((NSA_BLOCK, gps * wq), lambda b, gg, c: (b * n_blk + c, gg)),
            pl.BlockSpec((gps, NSA_BLOCK, LANES), lambda b, gg, c: (gg, b * n_blk + c, 0)),
            pl.BlockSpec((gps, n_blk, LANES), lambda b, gg, c: (gg, b, 0)),
            *[slab_spec(i) for i in range(gps) for _ in slabs],
            pl.BlockSpec((gps,) + tsel.shape[1:], lambda b, gg, c: (gg, 0, 0)),
            pl.BlockSpec((gps,) + twin.shape[1:], lambda b, gg, c: (gg, 0, 0)),
            pl.BlockSpec((gps,) + cmpq.shape[1:], lambda b, gg, c: (gg, 0, 0, 0)),
            pl.BlockSpec((gps,) + cmpt.shape[1:], lambda b, gg, c: (gg, 0, 0)),
        ],
        out_specs=pl.BlockSpec((NSA_BLOCK, gps * wq), lambda b, gg, c: (b * n_blk + c, gg)),
        out_shape=jax.ShapeDtypeStruct((m, NSA_HEADS * HEAD_DIM), f32),
        compiler_params=_cparams("parallel", "parallel", "arbitrary"),
        name="nsa_prompt_attn",
    )(q, gates, kvc, *(list(slabs) * gps), tsel, twin, cmpq, cmpt)


def _group_lane_mask(shape):
    r = lax.broadcasted_iota(i32, shape, 0)
    ln = lax.broadcasted_iota(i32, shape, 1)
    return (ln >> 6) == (r & (NSA_KV_HEADS - 1))


def _fold_groups(x):
    x = jnp.where(_group_lane_mask(x.shape), x, 0.0)
    out = x[:, 0:LANES]
    for pr in range(1, x.shape[1] // LANES):
        out = out + x[:, pr * LANES:(pr + 1) * LANES]
    return out + pltpu.roll(out, HEAD_DIM, 1)


def _select_blocks_sample(imp, cur, n_slots):
    imp_t = imp.T
    blk = lax.broadcasted_iota(i32, imp_t.shape, 0)
    valid = blk <= cur
    forced = (blk == 0) | (blk == cur) | (blk == cur - 1)
    if cur + 1 <= NSA_TOPK:
        return valid.astype(f32).T
    blk_f = blk.astype(f32)
    work = jnp.where(valid & jnp.logical_not(forced), imp_t, -1.0)
    sel = forced.astype(f32)
    for _ in range(NSA_TOPK - 3):
        mx = jnp.max(work, axis=0, keepdims=True)
        first = jnp.min(jnp.where(work == mx, blk_f, float(n_slots)), axis=0, keepdims=True)
        pick = blk_f == first
        work = jnp.where(pick, -2.0, work)
        sel = jnp.where(pick, 1.0, sel)
    return sel.T


def _nsa_sample_kernel(*refs, n_pages, page, past_len):
    pt_ref = refs[0]
    page_refs = refs[1:1 + n_pages]
    cmp_refs = refs[1 + n_pages:1 + 2 * n_pages]
    (win_ref, q_ref, kvn_ref, gt_ref, wc_ref, wc3_ref, bkey_ref, bblk_ref, emat_ref,
     o_ref, nwin_ref, kcv_scr) = refs[1 + 2 * n_pages:]
    del pt_ref
    b = pl.program_id(0)
    gd = NSA_KV_HEADS * HEAD_DIM
    bpp = page // NSA_BLOCK
    n_past_blk = past_len // NSA_BLOCK
    n_slots = kcv_scr.shape[0]
    cur = n_past_blk
    scale = HEAD_DIM ** -0.5

    kvn = kvn_ref[pl.ds(b, 1), :]
    qm = jnp.where(_group_lane_mask((NSA_HEADS, gd)), q_ref[0] * scale, 0.0)
    qm_b = qm.astype(bf16)

    wc3 = wc3_ref[...]
    for k in range(n_pages):
        part = cmp_refs[k][0, 0].reshape(bpp, NSA_BLOCK, 2 * NSA_KV_HEADS, HEAD_DIM)
        kcv_scr[k * bpp:(k + 1) * bpp] = jnp.sum(part * wc3[None], axis=1)
    kcv_scr[n_past_blk:n_slots] = jnp.zeros((n_slots - n_past_blk, 2 * NSA_KV_HEADS, HEAD_DIM), f32)
    new_row = lax.broadcasted_iota(i32, (n_slots, gd), 0) == n_past_blk
    wc = wc_ref[...]
    kc_rows = jnp.concatenate([kcv_scr[:, gi, :] for gi in range(NSA_KV_HEADS)], axis=1)
    vc_rows = jnp.concatenate([kcv_scr[:, NSA_KV_HEADS + gi, :] for gi in range(NSA_KV_HEADS)], axis=1)
    kc_b = (kc_rows + jnp.where(new_row, kvn[:, 0:gd] * wc[0:1, 0:gd], 0.0)).astype(bf16)
    vc_b = (vc_rows + jnp.where(new_row, kvn[:, gd:2 * gd] * wc[0:1, gd:2 * gd], 0.0)).astype(bf16)

    blk = lax.broadcasted_iota(i32, (NSA_HEADS, n_slots), 1)
    mask_c = blk * NSA_BLOCK + (NSA_BLOCK - 1) <= past_len
    s_c = jnp.where(mask_c, _dot_nt(qm_b, kc_b) + bblk_ref[...], NEG_INF)
    m_c = jnp.max(s_c, axis=1, keepdims=True)
    e_c = jnp.where(mask_c, jnp.exp(s_c - m_c), 0.0)
    l_c = jnp.sum(e_c, axis=1, keepdims=True)
    p_c = e_c / jnp.where(l_c > 0.0, l_c, 1.0)
    o_c = _fold_groups(_dot(p_c.astype(bf16), vc_b))

    imp = p_c[0:NSA_KV_HEADS]
    for hh in range(1, NSA_HPG):
        imp = imp + p_c[hh * NSA_KV_HEADS:(hh + 1) * NSA_KV_HEADS]
    sel = _select_blocks_sample(jnp.concatenate([imp, jnp.zeros_like(imp)], axis=0), cur, n_slots)
    sel16 = jnp.concatenate([sel[0:NSA_KV_HEADS]] * NSA_HPG, axis=0).astype(bf16)

    k_new = kvn[:, 2 * gd:3 * gd]
    v_new = kvn[:, 3 * gd:4 * gd]
    hidden = (_dot(sel16, emat_ref[...]) - 1.0) * (-NEG_INF)
    b_self = bkey_ref[:, past_len:past_len + LANES][:, 0:1]
    s_parts = [_dot_nt(qm_b, page_refs[k][0, 0, :, 0:gd].astype(bf16)) for k in range(n_pages)]
    s_s = jnp.concatenate(s_parts, axis=1) + bkey_ref[:, 0:past_len] + hidden
    s_self = jnp.sum(qm * k_new, axis=1, keepdims=True) + b_self
    m_s = jnp.maximum(jnp.max(s_s, axis=1, keepdims=True), s_self)
    e_s = jnp.exp(s_s - m_s)
    e_self = jnp.exp(s_self - m_s)
    l_s = jnp.sum(e_s, axis=1, keepdims=True) + e_self
    e_sb = e_s.astype(bf16)
    pv = e_self * v_new
    for k in range(n_pages):
        pv = pv + _dot(e_sb[:, k * page:(k + 1) * page], page_refs[k][0, 0, :, gd:2 * gd].astype(bf16))
    o_s = _fold_groups(pv) / l_s

    n_buf = win_ref.shape[2]
    kw_b = win_ref[0, 0, :, 0:gd].astype(bf16)
    vw_b = win_ref[0, 0, :, gd:2 * gd].astype(bf16)
    kw_new = kvn[:, 4 * gd:5 * gd]
    vw_new = kvn[:, 5 * gd:6 * gd]
    wpos = past_len - n_buf + lax.broadcasted_iota(i32, (NSA_HEADS, n_buf), 1)
    msk_w = (past_len - wpos <= NSA_WINDOW) & (wpos >= 0)
    s_w = jnp.where(msk_w, _dot_nt(qm_b, kw_b) + bkey_ref[:, past_len - n_buf:past_len], NEG_INF)
    sw_self = jnp.sum(qm * kw_new, axis=1, keepdims=True) + b_self
    m_w = jnp.maximum(jnp.max(s_w, axis=1, keepdims=True), sw_self)
    e_w = jnp.where(msk_w, jnp.exp(s_w - m_w), 0.0)
    ew_self = jnp.exp(sw_self - m_w)
    l_w = jnp.sum(e_w, axis=1, keepdims=True) + ew_self
    o_w = _fold_groups(_dot(e_w.astype(bf16), vw_b) + ew_self * vw_new) / l_w

    gt = gt_ref[0]
    lane_g = lax.broadcasted_iota(i32, gt.shape, 1)

    def gate(branch):
        return jnp.sum(jnp.where(lane_g == branch, gt, 0.0), axis=1, keepdims=True)

    o_ref[0] = gate(0) * o_c + gate(1) * o_s + gate(2) * o_w

    nwin_ref[0, 0, 0:n_buf - 1, :] = win_ref[0, 0, 1:n_buf, :]
    nwin_ref[0, 0, n_buf - 1:n_buf, :] = kvn[:, 4 * gd:6 * gd]


def _nsa_sample_attn(page_table, cache_kv, cache_slc, cache_win, layer, q_hg, kv_nat, gates_hg, w_cmp, wc_nat,
                     bkey, bblk, past_len):
    n_dec, n_pages = page_table.shape
    _, n_phys, page, _, g_, d_ = cache_kv.shape
    gd = g_ * d_
    n_buf = cache_win.shape[2]
    n_slots = bblk.shape[1]
    win2 = cache_win.reshape(cache_win.shape[0], n_dec, n_buf, 2 * gd)
    emat = (lax.broadcasted_iota(i32, (n_slots, past_len), 0)
            == lax.broadcasted_iota(i32, (n_slots, past_len), 1) // NSA_BLOCK).astype(bf16)
    wc3 = jnp.broadcast_to(w_cmp.transpose(1, 0, 2)[:, :, None, :],
                           (NSA_BLOCK, 2, g_, d_)).reshape(NSA_BLOCK, 2 * g_, d_)

    def page_spec(k):
        return pl.BlockSpec((1, 1, page, 2 * gd), lambda b, pt: (layer, pt[b * n_pages + k], 0, 0))

    def cmp_spec(k):
        return pl.BlockSpec((1, 1, page, 2, g_, d_), lambda b, pt: (layer, pt[b * n_pages + k], 0, 0, 0, 0))

    grid_spec = pltpu.PrefetchScalarGridSpec(
        num_scalar_prefetch=1,
        grid=(n_dec,),
        in_specs=[
            *[page_spec(k) for k in range(n_pages)],
            *[cmp_spec(k) for k in range(n_pages)],
            pl.BlockSpec((1, 1, n_buf, 2 * gd), lambda b, pt: (layer, b, 0, 0)),
            pl.BlockSpec((1, NSA_HEADS, gd), lambda b, pt: (b, 0, 0)),
            pl.BlockSpec(kv_nat.shape, lambda b, pt: (0, 0)),
            pl.BlockSpec((1, NSA_HEADS, LANES), lambda b, pt: (b, 0, 0)),
            pl.BlockSpec(wc_nat.shape, lambda b, pt: (0, 0)),
            pl.BlockSpec(wc3.shape, lambda b, pt: (0, 0, 0)),
            pl.BlockSpec(bkey.shape, lambda b, pt: (0, 0)),
            pl.BlockSpec(bblk.shape, lambda b, pt: (0, 0)),
            pl.BlockSpec(emat.shape, lambda b, pt: (0, 0)),
        ],
        out_specs=[
            pl.BlockSpec((1, NSA_HEADS, LANES), lambda b, pt: (b, 0, 0)),
            pl.BlockSpec((1, 1, n_buf, 2 * gd), lambda b, pt: (0, b, 0, 0)),
        ],
        scratch_shapes=[pltpu.VMEM((n_slots, 2 * g_, d_), f32)],
    )
    o, nwin = pl.pallas_call(
        functools.partial(_nsa_sample_kernel, n_pages=n_pages, page=page, past_len=past_len),
        grid_spec=grid_spec,
        out_shape=[jax.ShapeDtypeStruct((n_dec, NSA_HEADS, LANES), f32),
                   jax.ShapeDtypeStruct((1, n_dec, n_buf, 2 * gd), f32)],
        compiler_params=_cparams("parallel"),
        name="nsa_sample_attn",
    )(page_table.reshape(-1), *([cache_slc] * n_pages), *([cache_kv] * n_pages), win2, q_hg, kv_nat, gates_hg,
      wc_nat, wc3, bkey, bblk, emat)
    return o[:, :, :HEAD_DIM], nwin[0]


def _conv_prompt_kernel(x_ref, g_ref, win_ref, cw_ref, wout_ref, o_ref, st_ref, carry_ref, *, n_t):
    t = pl.program_id(1)
    tm, d = x_ref.shape

    @pl.when(t == 0)
    def _():
        carry_ref[...] = jnp.zeros_like(carry_ref)

    x = x_ref[...]
    xn = _rms(x, g_ref[...]).astype(bf16)
    pr = _dot(xn, win_ref[...])
    u = pr[:, d:2 * d] * pr[:, 0:d]
    bg = pr[:, 2 * d:3 * d]
    prev = carry_ref[...]
    row = lax.broadcasted_iota(i32, (tm, d), 0)
    u1 = jnp.where(row == 0, prev[7:8, :], pltpu.roll(u, 1, 0))
    u2 = jnp.where(row == 0, prev[6:7, :], jnp.where(row == 1, prev[7:8, :], pltpu.roll(u, 2, 0)))
    cw = cw_ref[...]
    v = cw[0:1, :] * u2 + cw[1:2, :] * u1 + cw[2:3, :] * u
    o_ref[...] = x + _dot((bg * v).astype(bf16), wout_ref[...])
    carry_ref[...] = u[tm - 8:tm, :]

    @pl.when(t == n_t - 1)
    def _():
        st_ref[0] = u[tm - (CONV_W - 1):tm, :]


def _conv_prompt(x, g, w_in, cw, w_out, n_batch, seq, tm_target=512):
    m, d = x.shape
    tm = _row_tile(seq, tm_target)
    n_t = seq // tm
    return pl.pallas_call(
        functools.partial(_conv_prompt_kernel, n_t=n_t),
        grid=(n_batch, n_t),
        in_specs=[pl.BlockSpec((tm, d), lambda b, t: (b * n_t + t, 0)),
                  pl.BlockSpec((1, d), lambda b, t: (0, 0)),
                  pl.BlockSpec((d, 3 * d), lambda b, t: (0, 0)),
                  pl.BlockSpec((CONV_W, d), lambda b, t: (0, 0)),
                  pl.BlockSpec((d, d), lambda b, t: (0, 0))],
        out_specs=[pl.BlockSpec((tm, d), lambda b, t: (b * n_t + t, 0)),
                   pl.BlockSpec((1, CONV_W - 1, d), lambda b, t: (b, 0, 0))],
        out_shape=[jax.ShapeDtypeStruct((m, d), f32),
                   jax.ShapeDtypeStruct((n_batch, CONV_W - 1, d), f32)],
        scratch_shapes=[pltpu.VMEM((8, d), f32)],
        compiler_params=_cparams("parallel", "arbitrary"),
        name="conv_prompt",
    )(x, g.reshape(1, d), w_in, cw, w_out)


def _conv_sample_kernel(x_ref, g_ref, st_ref, win_ref, cw_ref, wout_ref, o_ref, nst_ref):
    d = x_ref.shape[1]
    x = x_ref[...]
    xn = _rms(x, g_ref[...]).astype(bf16)
    pr = _dot(xn, win_ref[...])
    u = pr[:, d:2 * d] * pr[:, 0:d]
    bg = pr[:, 2 * d:3 * d]
    st = st_ref[...]
    cw = cw_ref[...]
    v = cw[0:1, :] * st[:, 0:d] + cw[1:2, :] * st[:, d:2 * d] + cw[2:3, :] * u
    o_ref[...] = x + _dot((bg * v).astype(bf16), wout_ref[...])
    nst_ref[:, 0:d] = st[:, d:2 * d]
    nst_ref[:, d:2 * d] = u


def _conv_sample(x, g, state, w_in, cw, w_out):
    m, d = x.shape
    o, nst = pl.pallas_call(
        _conv_sample_kernel,
        out_shape=[jax.ShapeDtypeStruct((m, d), f32),
                   jax.ShapeDtypeStruct((m, (CONV_W - 1) * d), f32)],
        compiler_params=pltpu.CompilerParams(vmem_limit_bytes=VMEM_LIMIT_BYTES),
        name="conv_sample",
    )(x, g.reshape(1, d), state.reshape(m, (CONV_W - 1) * d), w_in, cw, w_out)
    return o, nst.reshape(m, CONV_W - 1, d)


def _pool_mix(win_sum_groups, xn, cnt_groups, wgrp_ref, scale):
    gw = xn.shape[1] // len(POOL_WINDOWS)
    outs = []
    for gi in range(len(POOL_WINDOWS)):
        diff = win_sum_groups[gi] / cnt_groups[gi] - xn[:, gi * gw:(gi + 1) * gw]
        outs.append(_dot(diff.astype(bf16), wgrp_ref[gi]))
    return jnp.concatenate(outs, axis=1) * scale


def _pool_prompt_kernel(x_ref, g_ref, wgrp_ref, sc_ref, o_ref, st_ref, carry_ref, *, n_t):
    t = pl.program_id(1)
    tm, d = x_ref.shape
    gw = d // len(POOL_WINDOWS)
    hist = carry_ref.shape[0]

    @pl.when(t == 0)
    def _():
        carry_ref[...] = jnp.zeros_like(carry_ref)

    x = x_ref[...]
    xn = _rms(x, g_ref[...])
    a = jnp.concatenate([carry_ref[...], xn], axis=0)
    sums = []
    s = a
    shift = 1
    for gi, w in enumerate(POOL_WINDOWS):
        while shift < w:
            s = s + pltpu.roll(s, shift, 0)
            shift *= 2
        sums.append(s[hist:, gi * gw:(gi + 1) * gw])
    pos = t * tm + lax.broadcasted_iota(i32, (tm, gw), 0)
    cnts = [jnp.minimum(pos + 1, w).astype(f32) for w in POOL_WINDOWS]
    o_ref[...] = x + _pool_mix(sums, xn, cnts, wgrp_ref, sc_ref[...])
    carry_ref[...] = xn[tm - hist:tm, :]

    @pl.when(t == n_t - 1)
    def _():
        st_ref[0] = xn[tm - POOL_HIST:tm, :]


def _pool_prompt(x, g, w_grp, scale, n_batch, seq, tm_target=512):
    m, d = x.shape
    tm = _row_tile(seq, tm_target)
    n_t = seq // tm
    return pl.pallas_call(
        functools.partial(_pool_prompt_kernel, n_t=n_t),
        grid=(n_batch, n_t),
        in_specs=[pl.BlockSpec((tm, d), lambda b, t: (b * n_t + t, 0)),
                  pl.BlockSpec((1, d), lambda b, t: (0, 0)),
                  pl.BlockSpec(w_grp.shape, lambda b, t: (0, 0, 0)),
                  pl.BlockSpec((1, d), lambda b, t: (0, 0))],
        out_specs=[pl.BlockSpec((tm, d), lambda b, t: (b * n_t + t, 0)),
                   pl.BlockSpec((1, POOL_HIST, d), lambda b, t: (b, 0, 0))],
        out_shape=[jax.ShapeDtypeStruct((m, d), f32),
                   jax.ShapeDtypeStruct((n_batch, POOL_HIST, d), f32)],
        scratch_shapes=[pltpu.VMEM((POOL_HIST + 1, d), f32)],
        compiler_params=_cparams("parallel", "arbitrary"),
        name="pool_prompt",
    )(x, g.reshape(1, d), w_grp, scale.reshape(1, d))


def _pool_sample_kernel(x_ref, g_ref, st_ref, wgrp_ref, sc_ref, o_ref, nst_ref, *, pos):
    m, d = x_ref.shape
    gw = d // len(POOL_WINDOWS)
    x = x_ref[...]
    xn = _rms(x, g_ref[...])
    st = st_ref[...]
    sums, cnts = [], []
    s = xn
    back = 1
    for gi, w in enumerate(POOL_WINDOWS):
        while back < w:
            s = s + st[:, (POOL_HIST - back) * d:(POOL_HIST - back + 1) * d]
            back += 1
        sums.append(s[:, gi * gw:(gi + 1) * gw])
        cnts.append(jnp.full((m, gw), float(min(pos + 1, w)), f32))
    o_ref[...] = x + _pool_mix(sums, xn, cnts, wgrp_ref, sc_ref[...])
    nst_ref[:, 0:(POOL_HIST - 1) * d] = st[:, d:POOL_HIST * d]
    nst_ref[:, (POOL_HIST - 1) * d:POOL_HIST * d] = xn


def _pool_sample(x, g, state, w_grp, scale, pos):
    m, d = x.shape
    o, nst = pl.pallas_call(
        functools.partial(_pool_sample_kernel, pos=pos),
        out_shape=[jax.ShapeDtypeStruct((m, d), f32),
                   jax.ShapeDtypeStruct((m, POOL_HIST * d), f32)],
        compiler_params=pltpu.CompilerParams(vmem_limit_bytes=VMEM_LIMIT_BYTES),
        name="pool_sample",
    )(x, g.reshape(1, d), state.reshape(m, POOL_HIST * d), w_grp, scale.reshape(1, d))
    return o, nst.reshape(m, POOL_HIST, d)


def _ca_prompt_kernel(x_ref, g_ref, wq_ref, kv_ref, wo_ref, o_ref):
    d = x_ref.shape[1]
    hd = d // CA_HEADS
    x = x_ref[...]
    xn = _rms(x, g_ref[...]).astype(bf16)
    q = (_dot(xn, wq_ref[...]) * (hd ** -0.5)).astype(bf16)
    outs = []
    for h in range(CA_HEADS):
        k = kv_ref[:, h * hd:(h + 1) * hd].astype(bf16)
        v = kv_ref[:, d + h * hd:d + (h + 1) * hd].astype(bf16)
        s = _dot_nt(q[:, h * hd:(h + 1) * hd], k)
        e = jnp.exp(s - jnp.max(s, axis=1, keepdims=True))
        p = e / jnp.sum(e, axis=1, keepdims=True)
        outs.append(_dot(p.astype(bf16), v))
    o = jnp.concatenate(outs, axis=1).astype(bf16)
    o_ref[...] = x + _dot(o, wo_ref[...])


def _ca_prompt(x, g, wq, mkv, wo, n_batch, seq, mem_len, tm_target=512):
    m, d = x.shape
    tm = _row_tile(seq, tm_target)
    n_t = seq // tm
    return pl.pallas_call(
        _ca_prompt_kernel,
        grid=(n_batch, n_t),
        in_specs=[pl.BlockSpec((tm, d), lambda b, t: (b * n_t + t, 0)),
                  pl.BlockSpec((1, d), lambda b, t: (0, 0)),
                  pl.BlockSpec((d, d), lambda b, t: (0, 0)),
                  pl.BlockSpec((mem_len, 2 * d), lambda b, t: (b, 0)),
                  pl.BlockSpec((d, d), lambda b, t: (0, 0))],
        out_specs=pl.BlockSpec((tm, d), lambda b, t: (b * n_t + t, 0)),
        out_shape=jax.ShapeDtypeStruct((m, d), f32),
        compiler_params=_cparams("parallel", "parallel"),
        name="ca_prompt",
    )(x, g.reshape(1, d), wq, mkv, wo)


def _ca_sample_kernel(q_ref, kv_ref, o_ref):
    bb, _, hd = q_ref.shape
    scale = hd ** -0.5
    for bi in range(bb):
        q = q_ref[bi] * scale
        k = kv_ref[0, bi, :, 0]
        v = kv_ref[0, bi, :, 1]
        s = jnp.sum(k * q[None], axis=2, keepdims=True)
        e = jnp.exp(s - jnp.max(s, axis=0, keepdims=True))
        p = e / jnp.sum(e, axis=0, keepdims=True)
        o_ref[bi] = jnp.sum(p * v, axis=0)


def _ca_sample(q, cache_mem_kv, layer, bb=4):
    n_dec, d = q.shape
    _, _, mem, _, heads, hd = cache_mem_kv.shape
    o = pl.pallas_call(
        _ca_sample_kernel,
        grid=(n_dec // bb,),
        in_specs=[pl.BlockSpec((bb, heads, hd), lambda i: (i, 0, 0)),
                  pl.BlockSpec((1, bb, mem, 2, heads, hd), lambda i: (layer, i, 0, 0, 0, 0))],
        out_specs=pl.BlockSpec((bb, heads, hd), lambda i: (i, 0, 0)),
        out_shape=jax.ShapeDtypeStruct((n_dec, heads, hd), f32),
        compiler_params=_cparams("parallel"),
        name="ca_sample",
    )(q.reshape(n_dec, heads, hd), cache_mem_kv)
    return o.reshape(n_dec, d)


def _nsa_weights(w_in, w_cmp):
    d_model = w_in.shape[0]
    nq = NSA_HEADS * HEAD_DIM
    nkv = 6 * NSA_KV_HEADS * HEAD_DIM
    wq = w_in[:, :nq].astype(bf16)
    wkv = w_in[:, nq:nq + nkv].reshape(d_model, 3, 2, NSA_KV_HEADS, HEAD_DIM)
    wkv = wkv.transpose(0, 1, 3, 2, 4).reshape(d_model, nkv).astype(bf16)
    wg = w_in[:, nq + nkv:].reshape(d_model, NSA_KV_HEADS, NSA_HPG * 3)
    wg = jnp.pad(wg, ((0, 0), (0, 0), (0, LANES - NSA_HPG * 3))).reshape(d_model, NSA_KV_HEADS * LANES)
    w_pack = jnp.concatenate([w_cmp[0], w_cmp[1]], axis=1)
    wc_nat = jnp.concatenate([jnp.tile(w_cmp[0], (1, NSA_KV_HEADS)),
                              jnp.tile(w_cmp[1], (1, NSA_KV_HEADS))], axis=1)
    return wq, wkv, wg.astype(bf16), w_pack, wc_nat


def _kv_natural(kvp):
    m = kvp.shape[1]
    x = kvp.reshape(3, NSA_KV_HEADS, m, 2, HEAD_DIM)
    return x.transpose(2, 0, 3, 1, 4).reshape(m, 6 * NSA_KV_HEADS * HEAD_DIM)


def kernel(x_prompt, x_sample, cache_nsa_kv, cache_nsa_win, state_conv, state_pool, cache_mem_kv,
           page_table, mem_prompt, rel_bias, norm_g, norm_mem_g, final_g, w_ffn_in, w_ffn_out,
           w_nsa_in, w_nsa_cmp, w_nsa_out, w_conv_in, conv_w, w_conv_out, w_pool, pool_scale,
           w_ca_q, w_ca_kv, w_ca_out):
    n_b, seq, d = x_prompt.shape
    n_dec = x_sample.shape[0]
    depth = norm_g.shape[0]
    mem_len = mem_prompt.shape[1]
    page = cache_nsa_kv.shape[2]
    past_len = page_table.shape[1] * page
    g_, hd = NSA_KV_HEADS, HEAD_DIM
    gd = g_ * hd

    xp = x_prompt.reshape(n_b * seq, d)
    xs = x_sample.reshape(n_dec, d)
    mem = mem_prompt.reshape(n_b * mem_len, d)

    tsel, twin, cmpq, cmpt = _t5_prompt_tables(rel_bias, seq // NSA_BLOCK)
    n_blk_pad = -(-(past_len // NSA_BLOCK + 8) // LANES) * LANES
    bkey, bblk = _t5_sample_tables(rel_bias, past_len, past_len + LANES, n_blk_pad)

    cache_slc = cache_nsa_kv[:, :, :, 2:4].reshape(cache_nsa_kv.shape[0], cache_nsa_kv.shape[1], page, 2 * gd)

    nsa_kv_p, nsa_win_p, conv_p, pool_p, mem_p = [], [], [], [], []
    nsa_kv_s, nsa_win_s, conv_s, pool_s = [], [], [], []
    for i in range(depth):
        kind, j = i % 3, i // 3
        wf_in = w_ffn_in[i].astype(bf16)
        wf_out = w_ffn_out[i].astype(bf16)
        xp = _ffn(xp, norm_g[i, 0], wf_in[0], wf_out[0])
        xs = _ffn(xs, norm_g[i, 0], wf_in[0], wf_out[0])
        if kind == 0:
            wq, wkv, wg, w_pack, wc_nat = _nsa_weights(w_nsa_in[j], w_nsa_cmp[j])
            w_out = w_nsa_out[j].astype(bf16)
            q, kvp, gates, *slabs = _nsa_proj(xp, norm_g[i, 1], wq, wkv, wg, n_b, slab_pad=KV_PAD)
            kvc = _compress(kvp, w_pack, g_)
            o = _nsa_prompt_attn(q, gates, kvc, slabs, tsel, twin, cmpq, cmpt, n_b, seq)
            xp = _proj_res(o, w_out, xp)
            kv_nat = _kv_natural(kvp)
            nsa_kv_p.append(kv_nat[:, :4 * gd].reshape(n_b * seq // page, page, 4, g_, hd))
            n_keep = min(NSA_WINDOW, seq)
            nsa_win_p.append(kv_nat.reshape(n_b, seq, 6 * gd)[:, seq - n_keep:, 4 * gd:]
                             .reshape(n_b, n_keep, 2, g_, hd))
            q_s, kvp_s, gates_s = _nsa_proj(xs, norm_g[i, 1], wq, wkv, wg, 1)
            kv_nat_s = _kv_natural(kvp_s)
            q_hg = q_s.reshape(n_dec, g_, NSA_HPG, hd).transpose(0, 2, 1, 3).reshape(n_dec, NSA_HEADS, hd)
            q_hg = jnp.tile(q_hg, (1, 1, g_))
            gates_hg = gates_s.transpose(1, 0, 2)[:, :, :NSA_HPG * 3].reshape(n_dec, g_, NSA_HPG, 3)
            gates_hg = jnp.pad(gates_hg.transpose(0, 2, 1, 3).reshape(n_dec, NSA_HEADS, 3),
                               ((0, 0), (0, 0), (0, LANES - 3)))
            o_s, nwin = _nsa_sample_attn(page_table, cache_nsa_kv, cache_slc, cache_nsa_win, j, q_hg, kv_nat_s,
                                         gates_hg, w_nsa_cmp[j], wc_nat, bkey, bblk, past_len)
            o_s = o_s.reshape(n_dec, NSA_HPG, g_, hd).transpose(0, 2, 1, 3).reshape(n_dec, NSA_HEADS * hd)
            xs = _proj_res(o_s, w_out, xs)
            nsa_kv_s.append(kv_nat_s[:, :4 * gd].reshape(n_dec, 1, 4, g_, hd))
            nsa_win_s.append(nwin.reshape(n_dec, -1, 2, g_, hd))
        elif kind == 1:
            wc_in = w_conv_in[j].astype(bf16)
            wc_out = w_conv_out[j].astype(bf16)
            xp, cst = _conv_prompt(xp, norm_g[i, 1], wc_in, conv_w[j], wc_out, n_b, seq)
            xs, csts = _conv_sample(xs, norm_g[i, 1], state_conv[j], wc_in, conv_w[j], wc_out)
            conv_p.append(cst)
            conv_s.append(csts)
        else:
            wp = w_pool[j].astype(bf16)
            xp, pst = _pool_prompt(xp, norm_g[i, 1], wp, pool_scale[j], n_b, seq)
            xs, psts = _pool_sample(xs, norm_g[i, 1], state_pool[j], wp, pool_scale[j], past_len)
            pool_p.append(pst)
            pool_s.append(psts)
        w_q = w_ca_q[i].astype(bf16)
        w_o = w_ca_out[i].astype(bf16)
        mkv = _normproj(mem, norm_mem_g[i], w_ca_kv[i].astype(bf16))
        mem_p.append(mkv.reshape(n_b, mem_len, 2, CA_HEADS, d // CA_HEADS))
        xp = _ca_prompt(xp, norm_g[i, 2], w_q, mkv, w_o, n_b, seq, mem_len)
        q_ca = _normproj(xs, norm_g[i, 2], w_q)
        o_ca = _ca_sample(q_ca, cache_mem_kv, i)
        xs = _proj_res(o_ca, w_o, xs)
        fg = final_g if i == depth - 1 else None
        xp = _ffn(xp, norm_g[i, 3], wf_in[1], wf_out[1], final_g=fg)
        xs = _ffn(xs, norm_g[i, 3], wf_in[1], wf_out[1], final_g=fg)

    return (xp.reshape(n_b, seq, d), xs.reshape(n_dec, 1, d),
            jnp.stack(nsa_kv_p), jnp.stack(nsa_win_p), jnp.stack(conv_p), jnp.stack(pool_p),
            jnp.stack(mem_p),
            jnp.stack(nsa_kv_s), jnp.stack(nsa_win_s), jnp.stack(conv_s), jnp.stack(pool_s))
```

```python
import functools
import math

import jax
import jax.numpy as jnp
from jax import lax
from jax.experimental import pallas as pl
from jax.experimental.pallas import tpu as pltpu

f32 = jnp.float32
bf16 = jnp.bfloat16
i32 = jnp.int32

NORM_EPS = 1e-6
NEG_INF = -1e30

NSA_HEADS = 16
NSA_KV_HEADS = 4
NSA_HPG = NSA_HEADS // NSA_KV_HEADS
HEAD_DIM = 64
NSA_BLOCK = 64
NSA_TOPK = 16
NSA_WINDOW = 512
REL_BUCKETS = 32
REL_MAX_DIST = 128
CONV_W = 3
POOL_WINDOWS = (2, 4, 8, 16)
POOL_HIST = max(POOL_WINDOWS) - 1
CA_HEADS = 4

LANES = 128
VMEM_LIMIT_BYTES = 56 * 1024 * 1024

N_NEAR = 3
SEL_CHUNK_BLOCKS = 16
SEL_CHUNK = SEL_CHUNK_BLOCKS * NSA_BLOCK
NEAR_BLOCKS = 4
NEAR_TILE = NEAR_BLOCKS * NSA_BLOCK
WIN_TILE = 640
KV_PAD = 1024


def _cparams(*sem):
    return pltpu.CompilerParams(dimension_semantics=sem, vmem_limit_bytes=VMEM_LIMIT_BYTES)


def _rms(x, g):
    return x * lax.rsqrt(jnp.mean(x * x, axis=-1, keepdims=True) + NORM_EPS) * g


def _dot(a, b):
    return jnp.dot(a, b, preferred_element_type=f32)


def _dot_nt(a, b):
    return lax.dot_general(a, b, (((1,), (1,)), ((), ())), preferred_element_type=f32)


def _row_tile(m, target):
    t = min(m, target)
    while m % t:
        t //= 2
    return t


def _ffn_kernel(x_ref, g_ref, wg_ref, wu_ref, wo_ref, *rest, n_f, final_norm):
    if final_norm:
        gf_ref, o_ref, xn_ref, acc_ref = rest
    else:
        o_ref, xn_ref, acc_ref = rest
    j = pl.program_id(1)

    @pl.when(j == 0)
    def _():
        xn_ref[...] = _rms(x_ref[...], g_ref[...]).astype(bf16)
        acc_ref[...] = jnp.zeros_like(acc_ref)

    xn = xn_ref[...]
    gate = _dot(xn, wg_ref[...])
    up = _dot(xn, wu_ref[...])
    act = (gate * jax.nn.sigmoid(gate) * up).astype(bf16)
    acc_ref[...] += _dot(act, wo_ref[...])

    @pl.when(j == n_f - 1)
    def _():
        y = x_ref[...] + 0.5 * acc_ref[...]
        if final_norm:
            y = _rms(y, gf_ref[...])
        o_ref[...] = y


def _ffn(x, g, w_in, w_out, final_g=None, tm_target=512):
    m, d = x.shape
    f = w_out.shape[0]
    tm = _row_tile(m, tm_target)
    tf = f
    for cand in (1408, 1024, 512, 256, 128):
        if f % cand == 0:
            tf = cand
            break
    n_f = f // tf
    in_specs = [
        pl.BlockSpec((tm, d), lambda i, j: (i, 0)),
        pl.BlockSpec((1, d), lambda i, j: (0, 0)),
        pl.BlockSpec((d, tf), lambda i, j: (0, j)),
        pl.BlockSpec((d, tf), lambda i, j: (0, n_f + j)),
        pl.BlockSpec((tf, d), lambda i, j: (j, 0)),
    ]
    args = [x, g.reshape(1, d), w_in, w_in, w_out]
    if final_g is not None:
        in_specs.append(pl.BlockSpec((1, d), lambda i, j: (0, 0)))
        args.append(final_g.reshape(1, d))
    return pl.pallas_call(
        functools.partial(_ffn_kernel, n_f=n_f, final_norm=final_g is not None),
        grid=(m // tm, n_f),
        in_specs=in_specs,
        out_specs=pl.BlockSpec((tm, d), lambda i, j: (i, 0)),
        out_shape=jax.ShapeDtypeStruct((m, d), f32),
        scratch_shapes=[pltpu.VMEM((tm, d), bf16), pltpu.VMEM((tm, d), f32)],
        compiler_params=_cparams("parallel", "arbitrary"),
        name="ffn",
    )(*args)


def _normproj_kernel(x_ref, g_ref, w_ref, o_ref):
    xn = _rms(x_ref[...], g_ref[...]).astype(bf16)
    o_ref[...] = _dot(xn, w_ref[...])


def _normproj(x, g, w, tm_target=512):
    m, d = x.shape
    n = w.shape[1]
    tm = _row_tile(m, tm_target)
    return pl.pallas_call(
        _normproj_kernel,
        grid=(m // tm,),
        in_specs=[pl.BlockSpec((tm, d), lambda i: (i, 0)),
                  pl.BlockSpec((1, d), lambda i: (0, 0)),
                  pl.BlockSpec((d, n), lambda i: (0, 0))],
        out_specs=pl.BlockSpec((tm, n), lambda i: (i, 0)),
        out_shape=jax.ShapeDtypeStruct((m, n), f32),
        compiler_params=_cparams("parallel"),
        name="normproj",
    )(x, g.reshape(1, d), w)


def _proj_res_kernel(a_ref, w_ref, x_ref, o_ref):
    o_ref[...] = x_ref[...] + _dot(a_ref[...].astype(bf16), w_ref[...])


def _proj_res(a, w, x, tm_target=512):
    m, k = a.shape
    d = w.shape[1]
    tm = _row_tile(m, tm_target)
    return pl.pallas_call(
        _proj_res_kernel,
        grid=(m // tm,),
        in_specs=[pl.BlockSpec((tm, k), lambda i: (i, 0)),
                  pl.BlockSpec((k, d), lambda i: (0, 0)),
                  pl.BlockSpec((tm, d), lambda i: (i, 0))],
        out_specs=pl.BlockSpec((tm, d), lambda i: (i, 0)),
        out_shape=jax.ShapeDtypeStruct((m, d), f32),
        compiler_params=_cparams("parallel"),
        name="proj_res",
    )(a, w, x)


def _nsa_proj_kernel(x_ref, g_ref, wq_ref, wkv_ref, wg_ref, q_ref, kv_ref, gt_ref, *slab_refs, n_pad):
    t = pl.program_id(1) - n_pad

    @pl.when(t >= 0)
    def _():
        xn = _rms(x_ref[...], g_ref[...]).astype(bf16)
        q_ref[...] = _dot(xn, wq_ref[...])
        kv = _dot(xn, wkv_ref[...])
        n_g = kv_ref.shape[0] // 3
        tm = kv.shape[0]
        for s in range(kv_ref.shape[0]):
            kv_ref[s] = kv[:, s * LANES:(s + 1) * LANES]
        gt = jax.nn.sigmoid(_dot(xn, wg_ref[...]))
        for s in range(gt_ref.shape[0]):
            gt_ref[s] = gt[:, s * LANES:(s + 1) * LANES]
        if slab_refs:
            ks_ref, vs_ref, kw_ref, vw_ref = slab_refs
            lane = lax.broadcasted_iota(i32, (tm, LANES), 1)
            lower = lane < HEAD_DIM
            blk = (t * tm + lax.broadcasted_iota(i32, (tm, LANES), 0)) // NSA_BLOCK
            onehot = jnp.where(lane - HEAD_DIM == blk, 1.0, 0.0)
            for gi in range(n_g):
                sel = kv[:, (n_g + gi) * LANES:(n_g + gi + 1) * LANES]
                win = kv[:, (2 * n_g + gi) * LANES:(2 * n_g + gi + 1) * LANES]
                ks_ref[gi, 0] = jnp.where(lower, sel, onehot).astype(bf16)
                vs_ref[gi, 0] = jnp.where(lower, 1.0, sel).astype(bf16)
                kw_ref[gi, 0] = jnp.where(lower, win, 0.0).astype(bf16)
                vw_ref[gi, 0] = jnp.where(lower, 1.0, win).astype(bf16)

    if slab_refs:
        @pl.when(t < 0)
        def _():
            for ref in slab_refs:
                ref[...] = jnp.zeros(ref.shape, ref.dtype)


def _nsa_proj(x, g, wq, wkv, wg, n_batch, slab_pad=None, tm_target=512):
    m, d = x.shape
    seq = m // n_batch
    nq, nkv, ng = wq.shape[1], wkv.shape[1], wg.shape[1]
    n_slab, n_gs = nkv // LANES, ng // LANES
    tm = _row_tile(seq, tm_target)
    n_t = seq // tm
    n_pad = 0 if slab_pad is None else slab_pad // tm

    def row(b, t):
        return b * n_t + jnp.maximum(t - n_pad, 0)

    out_specs = [pl.BlockSpec((tm, nq), lambda b, t: (row(b, t), 0)),
                 pl.BlockSpec((n_slab, tm, LANES), lambda b, t: (0, row(b, t), 0)),
                 pl.BlockSpec((n_gs, tm, LANES), lambda b, t: (0, row(b, t), 0))]
    out_shape = [jax.ShapeDtypeStruct((m, nq), f32),
                 jax.ShapeDtypeStruct((n_slab, m, LANES), f32),
                 jax.ShapeDtypeStruct((n_gs, m, LANES), f32)]
    if slab_pad is not None:
        assert slab_pad % tm == 0
        n_g = n_slab // 3
        out_specs += [pl.BlockSpec((n_g, 1, tm, LANES), lambda b, t: (0, b, t, 0))] * 4
        out_shape += [jax.ShapeDtypeStruct((n_g, n_batch, slab_pad + seq, LANES), bf16)] * 4
    return pl.pallas_call(
        functools.partial(_nsa_proj_kernel, n_pad=n_pad),
        grid=(n_batch, n_t + n_pad),
        in_specs=[pl.BlockSpec((tm, d), lambda b, t: (row(b, t), 0)),
                  pl.BlockSpec((1, d), lambda b, t: (0, 0)),
                  pl.BlockSpec((d, nq), lambda b, t: (0, 0)),
                  pl.BlockSpec((d, nkv), lambda b, t: (0, 0)),
                  pl.BlockSpec((d, ng), lambda b, t: (0, 0))],
        out_specs=out_specs,
        out_shape=out_shape,
        compiler_params=_cparams("parallel", "arbitrary"),
        name="nsa_proj",
    )(x, g.reshape(1, d), wq, wkv, wg)


def _compress_kernel(kv_ref, w_ref, o_ref):
    tm = kv_ref.shape[1]
    nb = tm // NSA_BLOCK
    x = kv_ref[0].reshape(nb, NSA_BLOCK, LANES)
    o_ref[0] = jnp.sum(x * w_ref[...][None], axis=1)


def _compress(kvp, w_pack, n_groups, tm_target=512):
    m = kvp.shape[1]
    tm = _row_tile(m, tm_target)
    nb = tm // NSA_BLOCK
    return pl.pallas_call(
        _compress_kernel,
        grid=(n_groups, m // tm),
        in_specs=[pl.BlockSpec((1, tm, LANES), lambda g, i: (g, i, 0)),
                  pl.BlockSpec((NSA_BLOCK, LANES), lambda g, i: (0, 0))],
        out_specs=pl.BlockSpec((1, nb, LANES), lambda g, i: (g, i, 0)),
        out_shape=jax.ShapeDtypeStruct((n_groups, m // NSA_BLOCK, LANES), f32),
        compiler_params=_cparams("parallel", "parallel"),
        name="nsa_compress",
    )(kvp, w_pack)


def _t5_bucket(rel):
    n = jnp.maximum(rel, 0)
    max_exact = REL_BUCKETS // 2
    nf = jnp.maximum(n, 1).astype(f32)
    large = max_exact + (jnp.log(nf / max_exact) / math.log(REL_MAX_DIST / max_exact)
                         * (REL_BUCKETS - max_exact)).astype(i32)
    large = jnp.minimum(large, REL_BUCKETS - 1)
    return jnp.where(n < max_exact, n, large)


def _bias_lookup(table_ref, bucket, head):
    out = jnp.zeros(bucket.shape, f32)
    for k in range(REL_BUCKETS):
        out = jnp.where(bucket == k, table_ref[k, head], out)
    return out


def _t5_prompt_kernel(table_ref, tsel_ref, twin_ref, cmpq_ref, cmpt_ref):
    g = pl.program_id(0)
    n_bc = cmpq_ref.shape[3]
    rel_s = (lax.broadcasted_iota(i32, (NSA_BLOCK, NEAR_TILE), 0) + (NEAR_TILE - NSA_BLOCK)
             - lax.broadcasted_iota(i32, (NSA_BLOCK, NEAR_TILE), 1))
    rel_w = (lax.broadcasted_iota(i32, (NSA_BLOCK, WIN_TILE), 0) + (WIN_TILE - NSA_BLOCK)
             - lax.broadcasted_iota(i32, (NSA_BLOCK, WIN_TILE), 1))
    bucket_s = _t5_bucket(rel_s)
    bucket_w = _t5_bucket(rel_w)
    qi_b = lax.broadcasted_iota(i32, (NSA_BLOCK, n_bc), 0)
    shp_t = (8, NSA_BLOCK)
    bucket_t = _t5_bucket(lax.broadcasted_iota(i32, shp_t, 1) - (NSA_BLOCK - 1)
                          + NSA_BLOCK * lax.broadcasted_iota(i32, shp_t, 0))
    for hh in range(NSA_HPG):
        head = g * NSA_HPG + hh
        far = table_ref[REL_BUCKETS - 1, head]
        rows = slice(hh * NSA_BLOCK, (hh + 1) * NSA_BLOCK)
        tsel_ref[0, rows, :] = jnp.where(rel_s >= 0, _bias_lookup(table_ref, bucket_s, head) - far, NEG_INF)
        twin_ref[0, rows, :] = jnp.where((rel_w >= 0) & (rel_w <= NSA_WINDOW),
                                         _bias_lookup(table_ref, bucket_w, head) - far, NEG_INF)
        for d in range(N_NEAR):
            cmpq_ref[0, d, rows, :] = _bias_lookup(
                table_ref, _t5_bucket(qi_b - (NSA_BLOCK - 1) + NSA_BLOCK * d), head) - far
        cmpt_ref[0, :, rows] = _bias_lookup(table_ref, bucket_t, head) - far


def _t5_prompt_tables(rel_bias, n_blk):
    rows = NSA_HPG * NSA_BLOCK
    return pl.pallas_call(
        _t5_prompt_kernel,
        grid=(NSA_KV_HEADS,),
        in_specs=[pl.BlockSpec(memory_space=pltpu.SMEM)],
        out_specs=[pl.BlockSpec((1, rows, NEAR_TILE), lambda g: (g, 0, 0)),
                   pl.BlockSpec((1, rows, WIN_TILE), lambda g: (g, 0, 0)),
                   pl.BlockSpec((1, N_NEAR, rows, n_blk), lambda g: (g, 0, 0, 0)),
                   pl.BlockSpec((1, 8, rows), lambda g: (g, 0, 0))],
        out_shape=[jax.ShapeDtypeStruct((NSA_KV_HEADS, rows, NEAR_TILE), f32),
                   jax.ShapeDtypeStruct((NSA_KV_HEADS, rows, WIN_TILE), f32),
                   jax.ShapeDtypeStruct((NSA_KV_HEADS, N_NEAR, rows, n_blk), f32),
                   jax.ShapeDtypeStruct((NSA_KV_HEADS, 8, rows), f32)],
        compiler_params=_cparams("parallel"),
        name="t5_prompt_tables",
    )(rel_bias)


def _t5_sample_kernel(table_ref, key_ref, blk_ref, *, qpos):
    kpos = lax.broadcasted_iota(i32, (1, key_ref.shape[1]), 1)
    bucket_k = _t5_bucket(qpos - kpos)
    blk = lax.broadcasted_iota(i32, (1, blk_ref.shape[1]), 1)
    bucket_b = _t5_bucket(qpos - (blk * NSA_BLOCK + NSA_BLOCK - 1))
    for r in range(NSA_HEADS):
        head = (r % NSA_KV_HEADS) * NSA_HPG + r // NSA_KV_HEADS
        key_ref[r:r + 1, :] = _bias_lookup(table_ref, bucket_k, head)
        blk_ref[r:r + 1, :] = _bias_lookup(table_ref, bucket_b, head)


def _t5_sample_tables(rel_bias, qpos, n_key, n_blk):
    return pl.pallas_call(
        functools.partial(_t5_sample_kernel, qpos=qpos),
        in_specs=[pl.BlockSpec(memory_space=pltpu.SMEM)],
        out_shape=[jax.ShapeDtypeStruct((NSA_HEADS, n_key), f32),
                   jax.ShapeDtypeStruct((NSA_HEADS, n_blk), f32)],
        name="t5_sample_tables",
    )(rel_bias)


def _flash_update(carry, s, v_aug):
    m, acc = carry
    m_new = jnp.maximum(m, jnp.max(s, axis=1, keepdims=True))
    p = jnp.exp(s - m_new).astype(bf16)
    return m_new, jnp.exp(m - m_new) * acc + _dot(p, v_aug)


def _flash_first(s, v_aug):
    m = jnp.max(s, axis=1, keepdims=True)
    return m, _dot(jnp.exp(s - m).astype(bf16), v_aug)


def _flash_finish(carry):
    _, acc = carry
    return acc / pltpu.roll(acc, HEAD_DIM, 1)


def _tile_rows(x, n):
    return jnp.concatenate([x] * n, axis=0)


def _select_blocks(imp_t, c, n_blk):
    blk = lax.broadcasted_iota(i32, imp_t.shape, 0)
    valid = blk <= c
    forced = (blk == 0) | (blk == c) | (blk == c - 1)

    blk_f = blk.astype(f32)
    work = jnp.where(valid & jnp.logical_not(forced), imp_t, -1.0)
    sel = forced.astype(f32)
    for _ in range(NSA_TOPK - 3):
        mx = jnp.max(work, axis=0, keepdims=True)
        first = jnp.min(jnp.where(work == mx, blk_f, float(n_blk)), axis=0, keepdims=True)
        pick = blk_f == first
        work = jnp.where(pick, -2.0, work)
        sel = jnp.where(pick, 1.0, sel)
    return jnp.where(c + 1 <= NSA_TOPK, valid.astype(f32), sel)


def _nsa_prompt_kernel(*refs, n_blk, gps):
    q_ref, gt_ref, kvc_ref = refs[0:3]
    slab_refs = [refs[3 + 4 * i:3 + 4 * (i + 1)] for i in range(gps)]
    tsel_ref, twin_ref, cmpq_ref, cmpt_ref, o_ref = refs[3 + 4 * gps:]
    c = pl.program_id(2)
    rows = NSA_HPG * NSA_BLOCK
    wq = NSA_HPG * HEAD_DIM
    pre = _nsa_prompt_cmp(c, q_ref, kvc_ref, cmpq_ref, cmpt_ref, gps, n_blk)
    sel_all = _select_blocks(jnp.concatenate([p[3] for p in pre], axis=1), c, n_blk).astype(bf16)
    fronts = [_nsa_prompt_front(c, pre[i], sel_all[:, i * LANES:(i + 1) * LANES], n_blk) for i in range(gps)]
    o_win = _nsa_prompt_window(c, [p[1] for p in pre], slab_refs, twin_ref)

    def far_body(jj, carries):
        out = []
        for f, (ks_ref, vs_ref, _, _), carry in zip(fronts, slab_refs, carries):
            start = pl.multiple_of(KV_PAD + jj * SEL_CHUNK, SEL_CHUNK // 8)
            s = _dot_nt(f[1], ks_ref[0, 0, pl.ds(start, SEL_CHUNK), :])
            out.append(_flash_update(carry, s, vs_ref[0, 0, pl.ds(start, SEL_CHUNK), :]))
        return tuple(out)

    n_far = jnp.maximum(c - (NEAR_BLOCKS - 1), 0)
    init = tuple((jnp.full((rows, 1), NEG_INF, f32), jnp.zeros((rows, LANES), f32)) for _ in range(gps))
    carries = lax.fori_loop(0, (n_far + SEL_CHUNK_BLOCKS - 1) // SEL_CHUNK_BLOCKS, far_body, init)

    lower = lax.broadcasted_iota(i32, (NSA_BLOCK, LANES), 1) < HEAD_DIM
    lane_g = lax.broadcasted_iota(i32, (NSA_BLOCK, LANES), 1)
    for i in range(gps):
        q_pad, _, near_hidden, o_c = fronts[i]
        o_w = o_win[i]
        ks_ref, vs_ref = slab_refs[i][0:2]
        start = pl.multiple_of((c + 1) * NSA_BLOCK + (KV_PAD - NEAR_TILE), NSA_BLOCK)
        s_near = (_dot_nt(q_pad, ks_ref[0, 0, pl.ds(start, NEAR_TILE), :]) + tsel_ref[i]
                  + _tile_rows(near_hidden, NSA_HPG))
        o_s = _flash_finish(_flash_update(carries[i], s_near, vs_ref[0, 0, pl.ds(start, NEAR_TILE), :]))
        gt = gt_ref[i]

        def gate(h, branch):
            return jnp.sum(jnp.where(lane_g == h * 3 + branch, gt, 0.0), axis=1, keepdims=True)

        heads = []
        for h in range(NSA_HPG):
            r = slice(h * NSA_BLOCK, (h + 1) * NSA_BLOCK)
            heads.append(gate(h, 0) * o_c[r] + gate(h, 1) * o_s[r] + gate(h, 2) * o_w[r])
        for pr in range(NSA_HPG // 2):
            o_ref[:, i * wq + pr * LANES:i * wq + (pr + 1) * LANES] = jnp.where(
                lower, pltpu.roll(heads[2 * pr], HEAD_DIM, 1), heads[2 * pr + 1])


def _nsa_prompt_cmp(c, q_ref, kvc_ref, cmpq_ref, cmpt_ref, gps, n_blk):
    rows = NSA_HPG * NSA_BLOCK
    wq = NSA_HPG * HEAD_DIM
    scale = HEAD_DIM ** -0.5
    lower = lax.broadcasted_iota(i32, (NSA_BLOCK, LANES), 1) < HEAD_DIM
    blk = lax.broadcasted_iota(i32, (rows, n_blk), 1)
    blk_t = lax.broadcasted_iota(i32, (n_blk, rows), 0)

    q_parts, q_pads, kvcs, sc_raw, st_raw = [], [], [], [], []
    for i in range(gps):
        parts = []
        for h in range(NSA_HPG):
            x = q_ref[:, i * wq + (h // 2) * LANES:i * wq + (h // 2 + 1) * LANES] * scale
            if h % 2:
                x = pltpu.roll(x, HEAD_DIM, 1)
            parts.append(jnp.where(lower, x, 0.0))
        q_pad = jnp.concatenate(parts, axis=0).astype(bf16)
        kvc = kvc_ref[i].astype(bf16)
        bias_c = jnp.where(blk == c, cmpq_ref[i, 0],
                           jnp.where(blk == c - 1, cmpq_ref[i, 1],
                                     jnp.where(blk == c - 2, cmpq_ref[i, 2], 0.0)))
        cols = cmpt_ref[i]
        bias_t = jnp.where(blk_t == c, cols[0:1, :],
                           jnp.where(blk_t == c - 1, cols[1:2, :],
                                     jnp.where(blk_t == c - 2, cols[2:3, :], 0.0)))
        q_parts.append(parts)
        q_pads.append(q_pad)
        kvcs.append(kvc)
        sc_raw.append(_dot_nt(q_pad, kvc) + bias_c)
        st_raw.append(_dot_nt(kvc, q_pad) + bias_t)

    blk_a = lax.broadcasted_iota(i32, (gps * rows, n_blk), 1)
    qi_a = lax.broadcasted_iota(i32, (gps * rows, n_blk), 0) & (NSA_BLOCK - 1)
    mask_c = (blk_a < c) | ((blk_a == c) & (qi_a == NSA_BLOCK - 1))
    s_c = jnp.where(mask_c, jnp.concatenate(sc_raw, axis=0), NEG_INF)
    e_c = jnp.where(mask_c, jnp.exp(s_c - jnp.max(s_c, axis=1, keepdims=True)), 0.0)
    l_c = jnp.sum(e_c, axis=1, keepdims=True)
    p_c = (e_c / jnp.where(l_c > 0.0, l_c, 1.0)).astype(bf16)

    blk_b = lax.broadcasted_iota(i32, (n_blk, gps * rows), 0)
    qi_b = lax.broadcasted_iota(i32, (n_blk, gps * rows), 1) & (NSA_BLOCK - 1)
    mask_t = (blk_b < c) | ((blk_b == c) & (qi_b == NSA_BLOCK - 1))
    s_t = jnp.where(mask_t, jnp.concatenate(st_raw, axis=1), NEG_INF)
    e_t = jnp.where(mask_t, jnp.exp(s_t - jnp.max(s_t, axis=0, keepdims=True)), 0.0)
    l_t = jnp.sum(e_t, axis=0, keepdims=True)
    p_t = e_t / jnp.where(l_t > 0.0, l_t, 1.0)

    out = []
    for i in range(gps):
        o_c = _dot(p_c[i * rows:(i + 1) * rows], kvcs[i])
        imp_t = p_t[:, i * rows:i * rows + LANES]
        for pr in range(1, NSA_HPG // 2):
            imp_t = imp_t + p_t[:, i * rows + pr * LANES:i * rows + (pr + 1) * LANES]
        imp_t = imp_t + pltpu.roll(imp_t, NSA_BLOCK, 1)
        out.append((q_parts[i], q_pads[i], o_c, imp_t))
    return out


def _nsa_prompt_front(c, pre, sel_t, n_blk):
    q_parts, q_pad, o_c, _ = pre
    lower = lax.broadcasted_iota(i32, (NSA_BLOCK, LANES), 1) < HEAD_DIM
    eye = (lax.broadcasted_iota(i32, (NSA_BLOCK, LANES), 0)
           == lax.broadcasted_iota(i32, (NSA_BLOCK, LANES), 1)).astype(bf16)
    pieces = [jnp.zeros((HEAD_DIM, LANES), bf16), sel_t]
    if n_blk < LANES - HEAD_DIM:
        pieces.append(jnp.zeros((LANES - HEAD_DIM - n_blk, LANES), bf16))
    sel_hi = _dot_nt(eye, jnp.concatenate(pieces, axis=0))
    lane = lax.broadcasted_iota(i32, (NSA_BLOCK, LANES), 1)
    first_near = c - (NEAR_BLOCKS - 1)
    far_mask = jnp.where((sel_hi > 0.5) & (lane - HEAD_DIM < first_near), 0.0, NEG_INF)
    q_aug = jnp.concatenate([jnp.where(lower, x, far_mask) for x in q_parts], axis=0).astype(bf16)

    def sel_col(b):
        col = jnp.sum(jnp.where(lane == HEAD_DIM + b, sel_hi, 0.0), axis=1, keepdims=True)
        return jnp.broadcast_to(col, (NSA_BLOCK, NEAR_TILE))

    near_blk = lax.broadcasted_iota(i32, (NSA_BLOCK, NEAR_TILE), 1) >> 6
    has_prev = jnp.where(c >= 1, 1.0, 0.0)
    visible = jnp.where(near_blk == 0, sel_col(c - 3),
                        jnp.where(near_blk == 1, sel_col(c - 2), jnp.where(near_blk == 2, has_prev, 1.0)))
    near_hidden = jnp.where(visible > 0.5, 0.0, NEG_INF)

    return q_pad, q_aug, near_hidden, o_c


def _nsa_prompt_window(c, q_pads, slab_refs, twin_ref):
    rows = NSA_HPG * NSA_BLOCK
    start_w = pl.multiple_of((c + 1) * NSA_BLOCK + (KV_PAD - WIN_TILE), NSA_BLOCK)
    col_w = lax.broadcasted_iota(i32, (1, WIN_TILE), 1)
    before_start = jnp.where(col_w < WIN_TILE - (c + 1) * NSA_BLOCK, NEG_INF, 0.0)
    s_w = jnp.concatenate([_dot_nt(q_pad, refs[2][0, 0, pl.ds(start_w, WIN_TILE), :]) + twin_ref[i]
                           for i, (q_pad, refs) in enumerate(zip(q_pads, slab_refs))], axis=0) + before_start
    p = jnp.exp(s_w - jnp.max(s_w, axis=1, keepdims=True)).astype(bf16)
    return [_flash_finish((None, _dot(p[i * rows:(i + 1) * rows], refs[3][0, 0, pl.ds(start_w, WIN_TILE), :])))
            for i, refs in enumerate(slab_refs)]


def _nsa_prompt_attn(q, gates, kvc, slabs, tsel, twin, cmpq, cmpt, n_batch, seq, gps=4):
    m = q.shape[0]
    n_blk = seq // NSA_BLOCK
    assert n_blk <= HEAD_DIM, "the block one-hot lives in the 64 spare lanes of the key slab"
    g_ = NSA_KV_HEADS
    wq = NSA_HPG * HEAD_DIM
    slab = (1, 1) + slabs[0].shape[2:]

    def slab_spec(i):
        return pl.BlockSpec(slab, lambda b, gg, c: (gg * gps + i, b, 0, 0))

    return pl.pallas_call(
        functools.partial(_nsa_prompt_kernel, n_blk=n_blk, gps=gps),
        grid=(n_batch, g_ // gps, n_blk),
        in_specs=[
            pl.BlockSpec((NSA_BLOCK, gps * wq), lambda b, gg, c: (b * n_blk + c, gg)),
            pl.BlockSpec((gps, NSA_BLOCK, LANES), lambda b, gg, c: (gg, b * n_blk + c, 0)),
            pl.BlockSpec((gps, n_blk, LANES), lambda b, gg, c: (gg, b, 0)),
            *[slab_spec(i) for i in range(gps) for _ in slabs],
            pl.BlockSpec((gps,) + tsel.shape[1:], lambda b, gg, c: (gg, 0, 0)),
            pl.BlockSpec((gps,) + twin.shape[1:], lambda b, gg, c: (gg, 0, 0)),
            pl.BlockSpec((gps,) + cmpq.shape[1:], lambda b, gg, c: (gg, 0, 0, 0)),
            pl.BlockSpec((gps,) + cmpt.shape[1:], lambda b, gg, c: (gg, 0, 0)),
        ],
        out_specs=pl.BlockSpec((NSA_BLOCK, gps * wq), lambda b, gg, c: (b * n_blk + c, gg)),
        out_shape=jax.ShapeDtypeStruct((m, NSA_HEADS * HEAD_DIM), f32),
        compiler_params=_cparams("parallel", "parallel", "arbitrary"),
        name="nsa_prompt_attn",
    )(q, gates, kvc, *(list(slabs) * gps), tsel, twin, cmpq, cmpt)


def _group_lane_mask(shape):
    r = lax.broadcasted_iota(i32, shape, 0)
    ln = lax.broadcasted_iota(i32, shape, 1)
    return (ln >> 6) == (r & (NSA_KV_HEADS - 1))


def _fold_groups(x):
    x = jnp.where(_group_lane_mask(x.shape), x, 0.0)
    out = x[:, 0:LANES]
    for pr in range(1, x.shape[1] // LANES):
        out = out + x[:, pr * LANES:(pr + 1) * LANES]
    return out + pltpu.roll(out, HEAD_DIM, 1)


def _select_blocks_sample(imp, cur, n_slots):
    imp_t = imp.T
    blk = lax.broadcasted_iota(i32, imp_t.shape, 0)
    valid = blk <= cur
    forced = (blk == 0) | (blk == cur) | (blk == cur - 1)
    if cur + 1 <= NSA_TOPK:
        return valid.astype(f32).T
    blk_f = blk.astype(f32)
    work = jnp.where(valid & jnp.logical_not(forced), imp_t, -1.0)
    sel = forced.astype(f32)
    for _ in range(NSA_TOPK - 3):
        mx = jnp.max(work, axis=0, keepdims=True)
        first = jnp.min(jnp.where(work == mx, blk_f, float(n_slots)), axis=0, keepdims=True)
        pick = blk_f == first
        work = jnp.where(pick, -2.0, work)
        sel = jnp.where(pick, 1.0, sel)
    return sel.T


def _nsa_sample_kernel(*refs, n_pages, page, past_len):
    pt_ref = refs[0]
    page_refs = refs[1:1 + n_pages]
    (win_ref, q_ref, kvn_ref, gt_ref, wc_ref, bkey_ref, bblk_ref, emat_ref,
     o_ref, nwin_ref, kcv_scr) = refs[1 + n_pages:]
    del pt_ref
    b = pl.program_id(0)
    gd = NSA_KV_HEADS * HEAD_DIM
    bpp = page // NSA_BLOCK
    n_past_blk = past_len // NSA_BLOCK
    n_slots = kcv_scr.shape[0]
    cur = n_past_blk
    scale = HEAD_DIM ** -0.5

    kvn = kvn_ref[pl.ds(b, 1), :]
    qm = jnp.where(_group_lane_mask((NSA_HEADS, gd)), q_ref[0] * scale, 0.0)
    qm_b = qm.astype(bf16)

    wc = wc_ref[...]
    for k in range(n_pages):
        cmp_part = page_refs[k][0, 0, :, 0:2 * gd].reshape(bpp, NSA_BLOCK, 2 * gd)
        kcv_scr[k * bpp:(k + 1) * bpp, :] = jnp.sum(cmp_part * wc[None], axis=1)
    row8 = lax.broadcasted_iota(i32, (8, 2 * gd), 0)
    kcv_scr[n_past_blk:n_past_blk + 8, :] = jnp.where(row8 == 0, kvn[:, 0:2 * gd] * wc[0:1, :], 0.0)
    rest = n_slots - n_past_blk - 8
    kcv_scr[n_past_blk + 8:n_slots, :] = jnp.zeros((rest, 2 * gd), f32)
    kcv = kcv_scr[...]
    kc_b = kcv[:, 0:gd].astype(bf16)
    vc_b = kcv[:, gd:2 * gd].astype(bf16)

    blk = lax.broadcasted_iota(i32, (NSA_HEADS, n_slots), 1)
    mask_c = blk * NSA_BLOCK + (NSA_BLOCK - 1) <= past_len
    s_c = jnp.where(mask_c, _dot_nt(qm_b, kc_b) + bblk_ref[...], NEG_INF)
    m_c = jnp.max(s_c, axis=1, keepdims=True)
    e_c = jnp.where(mask_c, jnp.exp(s_c - m_c), 0.0)
    l_c = jnp.sum(e_c, axis=1, keepdims=True)
    p_c = e_c / jnp.where(l_c > 0.0, l_c, 1.0)
    o_c = _fold_groups(_dot(p_c.astype(bf16), vc_b))

    imp = p_c[0:NSA_KV_HEADS]
    for hh in range(1, NSA_HPG):
        imp = imp + p_c[hh * NSA_KV_HEADS:(hh + 1) * NSA_KV_HEADS]
    sel = _select_blocks_sample(jnp.concatenate([imp, jnp.zeros_like(imp)], axis=0), cur, n_slots)
    sel16 = jnp.concatenate([sel[0:NSA_KV_HEADS]] * NSA_HPG, axis=0).astype(bf16)

    k_new = kvn[:, 2 * gd:3 * gd]
    v_new = kvn[:, 3 * gd:4 * gd]
    hidden = (_dot(sel16, emat_ref[...]) - 1.0) * (-NEG_INF)
    b_self = bkey_ref[:, past_len:past_len + LANES][:, 0:1]
    s_parts = [_dot_nt(qm_b, page_refs[k][0, 0, :, 2 * gd:3 * gd].astype(bf16)) for k in range(n_pages)]
    s_s = jnp.concatenate(s_parts, axis=1) + bkey_ref[:, 0:past_len] + hidden
    s_self = jnp.sum(qm * k_new, axis=1, keepdims=True) + b_self
    m_s = jnp.maximum(jnp.max(s_s, axis=1, keepdims=True), s_self)
    e_s = jnp.exp(s_s - m_s)
    e_self = jnp.exp(s_self - m_s)
    l_s = jnp.sum(e_s, axis=1, keepdims=True) + e_self
    e_sb = e_s.astype(bf16)
    pv = e_self * v_new
    for k in range(n_pages):
        pv = pv + _dot(e_sb[:, k * page:(k + 1) * page], page_refs[k][0, 0, :, 3 * gd:4 * gd].astype(bf16))
    o_s = _fold_groups(pv) / l_s

    n_buf = win_ref.shape[2]
    kw_b = win_ref[0, 0, :, 0:gd].astype(bf16)
    vw_b = win_ref[0, 0, :, gd:2 * gd].astype(bf16)
    kw_new = kvn[:, 4 * gd:5 * gd]
    vw_new = kvn[:, 5 * gd:6 * gd]
    wpos = past_len - n_buf + lax.broadcasted_iota(i32, (NSA_HEADS, n_buf), 1)
    msk_w = (past_len - wpos <= NSA_WINDOW) & (wpos >= 0)
    s_w = jnp.where(msk_w, _dot_nt(qm_b, kw_b) + bkey_ref[:, past_len - n_buf:past_len], NEG_INF)
    sw_self = jnp.sum(qm * kw_new, axis=1, keepdims=True) + b_self
    m_w = jnp.maximum(jnp.max(s_w, axis=1, keepdims=True), sw_self)
    e_w = jnp.where(msk_w, jnp.exp(s_w - m_w), 0.0)
    ew_self = jnp.exp(sw_self - m_w)
    l_w = jnp.sum(e_w, axis=1, keepdims=True) + ew_self
    o_w = _fold_groups(_dot(e_w.astype(bf16), vw_b) + ew_self * vw_new) / l_w

    gt = gt_ref[0]
    lane_g = lax.broadcasted_iota(i32, gt.shape, 1)

    def gate(branch):
        return jnp.sum(jnp.where(lane_g == branch, gt, 0.0), axis=1, keepdims=True)

    o_ref[0] = gate(0) * o_c + gate(1) * o_s + gate(2) * o_w

    nwin_ref[0, 0, 0:n_buf - 1, :] = win_ref[0, 0, 1:n_buf, :]
    nwin_ref[0, 0, n_buf - 1:n_buf, :] = kvn[:, 4 * gd:6 * gd]


def _nsa_sample_attn(page_table, cache_kv, cache_win, layer, q_hg, kv_nat, gates_hg, wc_nat, bkey, bblk,
                     past_len):
    n_dec, n_pages = page_table.shape
    _, n_phys, page, _, g_, d_ = cache_kv.shape
    gd = g_ * d_
    n_buf = cache_win.shape[2]
    n_slots = bblk.shape[1]
    cache2 = cache_kv.reshape(cache_kv.shape[0], n_phys, page, 4 * gd)
    win2 = cache_win.reshape(cache_win.shape[0], n_dec, n_buf, 2 * gd)
    emat = (lax.broadcasted_iota(i32, (n_slots, past_len), 0)
            == lax.broadcasted_iota(i32, (n_slots, past_len), 1) // NSA_BLOCK).astype(bf16)

    def page_spec(k):
        return pl.BlockSpec((1, 1, page, 4 * gd), lambda b, pt: (layer, pt[b * n_pages + k], 0, 0))

    grid_spec = pltpu.PrefetchScalarGridSpec(
        num_scalar_prefetch=1,
        grid=(n_dec,),
        in_specs=[
            *[page_spec(k) for k in range(n_pages)],
            pl.BlockSpec((1, 1, n_buf, 2 * gd), lambda b, pt: (layer, b, 0, 0)),
            pl.BlockSpec((1, NSA_HEADS, gd), lambda b, pt: (b, 0, 0)),
            pl.BlockSpec(kv_nat.shape, lambda b, pt: (0, 0)),
            pl.BlockSpec((1, NSA_HEADS, LANES), lambda b, pt: (b, 0, 0)),
            pl.BlockSpec(wc_nat.shape, lambda b, pt: (0, 0)),
            pl.BlockSpec(bkey.shape, lambda b, pt: (0, 0)),
            pl.BlockSpec(bblk.shape, lambda b, pt: (0, 0)),
            pl.BlockSpec(emat.shape, lambda b, pt: (0, 0)),
        ],
        out_specs=[
            pl.BlockSpec((1, NSA_HEADS, LANES), lambda b, pt: (b, 0, 0)),
            pl.BlockSpec((1, 1, n_buf, 2 * gd), lambda b, pt: (0, b, 0, 0)),
        ],
        scratch_shapes=[pltpu.VMEM((n_slots, 2 * gd), f32)],
    )
    o, nwin = pl.pallas_call(
        functools.partial(_nsa_sample_kernel, n_pages=n_pages, page=page, past_len=past_len),
        grid_spec=grid_spec,
        out_shape=[jax.ShapeDtypeStruct((n_dec, NSA_HEADS, LANES), f32),
                   jax.ShapeDtypeStruct((1, n_dec, n_buf, 2 * gd), f32)],
        compiler_params=_cparams("parallel"),
        name="nsa_sample_attn",
    )(page_table.reshape(-1), *([cache2] * n_pages), win2, q_hg, kv_nat, gates_hg, wc_nat, bkey, bblk, emat)
    return o[:, :, :HEAD_DIM], nwin[0]


def _conv_prompt_kernel(x_ref, g_ref, win_ref, cw_ref, wout_ref, o_ref, st_ref, carry_ref, *, n_t):
    t = pl.program_id(1)
    tm, d = x_ref.shape

    @pl.when(t == 0)
    def _():
        carry_ref[...] = jnp.zeros_like(carry_ref)

    x = x_ref[...]
    xn = _rms(x, g_ref[...]).astype(bf16)
    pr = _dot(xn, win_ref[...])
    u = pr[:, d:2 * d] * pr[:, 0:d]
    bg = pr[:, 2 * d:3 * d]
    prev = carry_ref[...]
    row = lax.broadcasted_iota(i32, (tm, d), 0)
    u1 = jnp.where(row == 0, prev[7:8, :], pltpu.roll(u, 1, 0))
    u2 = jnp.where(row == 0, prev[6:7, :], jnp.where(row == 1, prev[7:8, :], pltpu.roll(u, 2, 0)))
    cw = cw_ref[...]
    v = cw[0:1, :] * u2 + cw[1:2, :] * u1 + cw[2:3, :] * u
    o_ref[...] = x + _dot((bg * v).astype(bf16), wout_ref[...])
    carry_ref[...] = u[tm - 8:tm, :]

    @pl.when(t == n_t - 1)
    def _():
        st_ref[0] = u[tm - (CONV_W - 1):tm, :]


def _conv_prompt(x, g, w_in, cw, w_out, n_batch, seq, tm_target=512):
    m, d = x.shape
    tm = _row_tile(seq, tm_target)
    n_t = seq // tm
    return pl.pallas_call(
        functools.partial(_conv_prompt_kernel, n_t=n_t),
        grid=(n_batch, n_t),
        in_specs=[pl.BlockSpec((tm, d), lambda b, t: (b * n_t + t, 0)),
                  pl.BlockSpec((1, d), lambda b, t: (0, 0)),
                  pl.BlockSpec((d, 3 * d), lambda b, t: (0, 0)),
                  pl.BlockSpec((CONV_W, d), lambda b, t: (0, 0)),
                  pl.BlockSpec((d, d), lambda b, t: (0, 0))],
        out_specs=[pl.BlockSpec((tm, d), lambda b, t: (b * n_t + t, 0)),
                   pl.BlockSpec((1, CONV_W - 1, d), lambda b, t: (b, 0, 0))],
        out_shape=[jax.ShapeDtypeStruct((m, d), f32),
                   jax.ShapeDtypeStruct((n_batch, CONV_W - 1, d), f32)],
        scratch_shapes=[pltpu.VMEM((8, d), f32)],
        compiler_params=_cparams("parallel", "arbitrary"),
        name="conv_prompt",
    )(x, g.reshape(1, d), w_in, cw, w_out)


def _conv_sample_kernel(x_ref, g_ref, st_ref, win_ref, cw_ref, wout_ref, o_ref, nst_ref):
    d = x_ref.shape[1]
    x = x_ref[...]
    xn = _rms(x, g_ref[...]).astype(bf16)
    pr = _dot(xn, win_ref[...])
    u = pr[:, d:2 * d] * pr[:, 0:d]
    bg = pr[:, 2 * d:3 * d]
    st = st_ref[...]
    cw = cw_ref[...]
    v = cw[0:1, :] * st[:, 0:d] + cw[1:2, :] * st[:, d:2 * d] + cw[2:3, :] * u
    o_ref[...] = x + _dot((bg * v).astype(bf16), wout_ref[...])
    nst_ref[:, 0:d] = st[:, d:2 * d]
    nst_ref[:, d:2 * d] = u


def _conv_sample(x, g, state, w_in, cw, w_out):
    m, d = x.shape
    o, nst = pl.pallas_call(
        _conv_sample_kernel,
        out_shape=[jax.ShapeDtypeStruct((m, d), f32),
                   jax.ShapeDtypeStruct((m, (CONV_W - 1) * d), f32)],
        compiler_params=pltpu.CompilerParams(vmem_limit_bytes=VMEM_LIMIT_BYTES),
        name="conv_sample",
    )(x, g.reshape(1, d), state.reshape(m, (CONV_W - 1) * d), w_in, cw, w_out)
    return o, nst.reshape(m, CONV_W - 1, d)


def _pool_mix(win_sum_groups, xn, cnt_groups, wgrp_ref, scale):
    gw = xn.shape[1] // len(POOL_WINDOWS)
    outs = []
    for gi in range(len(POOL_WINDOWS)):
        diff = win_sum_groups[gi] / cnt_groups[gi] - xn[:, gi * gw:(gi + 1) * gw]
        outs.append(_dot(diff.astype(bf16), wgrp_ref[gi]))
    return jnp.concatenate(outs, axis=1) * scale


def _pool_prompt_kernel(x_ref, g_ref, wgrp_ref, sc_ref, o_ref, st_ref, carry_ref, *, n_t):
    t = pl.program_id(1)
    tm, d = x_ref.shape
    gw = d // len(POOL_WINDOWS)
    hist = carry_ref.shape[0]

    @pl.when(t == 0)
    def _():
        carry_ref[...] = jnp.zeros_like(carry_ref)

    x = x_ref[...]
    xn = _rms(x, g_ref[...])
    a = jnp.concatenate([carry_ref[...], xn], axis=0)
    sums = []
    s = a
    shift = 1
    for gi, w in enumerate(POOL_WINDOWS):
        while shift < w:
            s = s + pltpu.roll(s, shift, 0)
            shift *= 2
        sums.append(s[hist:, gi * gw:(gi + 1) * gw])
    pos = t * tm + lax.broadcasted_iota(i32, (tm, gw), 0)
    cnts = [jnp.minimum(pos + 1, w).astype(f32) for w in POOL_WINDOWS]
    o_ref[...] = x + _pool_mix(sums, xn, cnts, wgrp_ref, sc_ref[...])
    carry_ref[...] = xn[tm - hist:tm, :]

    @pl.when(t == n_t - 1)
    def _():
        st_ref[0] = xn[tm - POOL_HIST:tm, :]


def _pool_prompt(x, g, w_grp, scale, n_batch, seq, tm_target=512):
    m, d = x.shape
    tm = _row_tile(seq, tm_target)
    n_t = seq // tm
    return pl.pallas_call(
        functools.partial(_pool_prompt_kernel, n_t=n_t),
        grid=(n_batch, n_t),
        in_specs=[pl.BlockSpec((tm, d), lambda b, t: (b * n_t + t, 0)),
                  pl.BlockSpec((1, d), lambda b, t: (0, 0)),
                  pl.BlockSpec(w_grp.shape, lambda b, t: (0, 0, 0)),
                  pl.BlockSpec((1, d), lambda b, t: (0, 0))],
        out_specs=[pl.BlockSpec((tm, d), lambda b, t: (b * n_t + t, 0)),
                   pl.BlockSpec((1, POOL_HIST, d), lambda b, t: (b, 0, 0))],
        out_shape=[jax.ShapeDtypeStruct((m, d), f32),
                   jax.ShapeDtypeStruct((n_batch, POOL_HIST, d), f32)],
        scratch_shapes=[pltpu.VMEM((POOL_HIST + 1, d), f32)],
        compiler_params=_cparams("parallel", "arbitrary"),
        name="pool_prompt",
    )(x, g.reshape(1, d), w_grp, scale.reshape(1, d))


def _pool_sample_kernel(x_ref, g_ref, st_ref, wgrp_ref, sc_ref, o_ref, nst_ref, *, pos):
    m, d = x_ref.shape
    gw = d // len(POOL_WINDOWS)
    x = x_ref[...]
    xn = _rms(x, g_ref[...])
    st = st_ref[...]
    sums, cnts = [], []
    s = xn
    back = 1
    for gi, w in enumerate(POOL_WINDOWS):
        while back < w:
            s = s + st[:, (POOL_HIST - back) * d:(POOL_HIST - back + 1) * d]
            back += 1
        sums.append(s[:, gi * gw:(gi + 1) * gw])
        cnts.append(jnp.full((m, gw), float(min(pos + 1, w)), f32))
    o_ref[...] = x + _pool_mix(sums, xn, cnts, wgrp_ref, sc_ref[...])
    nst_ref[:, 0:(POOL_HIST - 1) * d] = st[:, d:POOL_HIST * d]
    nst_ref[:, (POOL_HIST - 1) * d:POOL_HIST * d] = xn


def _pool_sample(x, g, state, w_grp, scale, pos):
    m, d = x.shape
    o, nst = pl.pallas_call(
        functools.partial(_pool_sample_kernel, pos=pos),
        out_shape=[jax.ShapeDtypeStruct((m, d), f32),
                   jax.ShapeDtypeStruct((m, POOL_HIST * d), f32)],
        compiler_params=pltpu.CompilerParams(vmem_limit_bytes=VMEM_LIMIT_BYTES),
        name="pool_sample",
    )(x, g.reshape(1, d), state.reshape(m, POOL_HIST * d), w_grp, scale.reshape(1, d))
    return o, nst.reshape(m, POOL_HIST, d)


def _ca_prompt_kernel(x_ref, g_ref, wq_ref, kv_ref, wo_ref, o_ref):
    d = x_ref.shape[1]
    hd = d // CA_HEADS
    x = x_ref[...]
    xn = _rms(x, g_ref[...]).astype(bf16)
    q = (_dot(xn, wq_ref[...]) * (hd ** -0.5)).astype(bf16)
    outs = []
    for h in range(CA_HEADS):
        k = kv_ref[:, h * hd:(h + 1) * hd].astype(bf16)
        v = kv_ref[:, d + h * hd:d + (h + 1) * hd].astype(bf16)
        s = _dot_nt(q[:, h * hd:(h + 1) * hd], k)
        e = jnp.exp(s - jnp.max(s, axis=1, keepdims=True))
        p = e / jnp.sum(e, axis=1, keepdims=True)
        outs.append(_dot(p.astype(bf16), v))
    o = jnp.concatenate(outs, axis=1).astype(bf16)
    o_ref[...] = x + _dot(o, wo_ref[...])


def _ca_prompt(x, g, wq, mkv, wo, n_batch, seq, mem_len, tm_target=512):
    m, d = x.shape
    tm = _row_tile(seq, tm_target)
    n_t = seq // tm
    return pl.pallas_call(
        _ca_prompt_kernel,
        grid=(n_batch, n_t),
        in_specs=[pl.BlockSpec((tm, d), lambda b, t: (b * n_t + t, 0)),
                  pl.BlockSpec((1, d), lambda b, t: (0, 0)),
                  pl.BlockSpec((d, d), lambda b, t: (0, 0)),
                  pl.BlockSpec((mem_len, 2 * d), lambda b, t: (b, 0)),
                  pl.BlockSpec((d, d), lambda b, t: (0, 0))],
        out_specs=pl.BlockSpec((tm, d), lambda b, t: (b * n_t + t, 0)),
        out_shape=jax.ShapeDtypeStruct((m, d), f32),
        compiler_params=_cparams("parallel", "parallel"),
        name="ca_prompt",
    )(x, g.reshape(1, d), wq, mkv, wo)


def _ca_sample_kernel(q_ref, kv_ref, o_ref):
    bb, _, hd = q_ref.shape
    scale = hd ** -0.5
    for bi in range(bb):
        q = q_ref[bi] * scale
        k = kv_ref[0, bi, :, 0]
        v = kv_ref[0, bi, :, 1]
        s = jnp.sum(k * q[None], axis=2, keepdims=True)
        e = jnp.exp(s - jnp.max(s, axis=0, keepdims=True))
        p = e / jnp.sum(e, axis=0, keepdims=True)
        o_ref[bi] = jnp.sum(p * v, axis=0)


def _ca_sample(q, cache_mem_kv, layer, bb=4):
    n_dec, d = q.shape
    _, _, mem, _, heads, hd = cache_mem_kv.shape
    o = pl.pallas_call(
        _ca_sample_kernel,
        grid=(n_dec // bb,),
        in_specs=[pl.BlockSpec((bb, heads, hd), lambda i: (i, 0, 0)),
                  pl.BlockSpec((1, bb, mem, 2, heads, hd), lambda i: (layer, i, 0, 0, 0, 0))],
        out_specs=pl.BlockSpec((bb, heads, hd), lambda i: (i, 0, 0)),
        out_shape=jax.ShapeDtypeStruct((n_dec, heads, hd), f32),
        compiler_params=_cparams("parallel"),
        name="ca_sample",
    )(q.reshape(n_dec, heads, hd), cache_mem_kv)
    return o.reshape(n_dec, d)


def _nsa_weights(w_in, w_cmp):
    d_model = w_in.shape[0]
    nq = NSA_HEADS * HEAD_DIM
    nkv = 6 * NSA_KV_HEADS * HEAD_DIM
    wq = w_in[:, :nq].astype(bf16)
    wkv = w_in[:, nq:nq + nkv].reshape(d_model, 3, 2, NSA_KV_HEADS, HEAD_DIM)
    wkv = wkv.transpose(0, 1, 3, 2, 4).reshape(d_model, nkv).astype(bf16)
    wg = w_in[:, nq + nkv:].reshape(d_model, NSA_KV_HEADS, NSA_HPG * 3)
    wg = jnp.pad(wg, ((0, 0), (0, 0), (0, LANES - NSA_HPG * 3))).reshape(d_model, NSA_KV_HEADS * LANES)
    w_pack = jnp.concatenate([w_cmp[0], w_cmp[1]], axis=1)
    wc_nat = jnp.concatenate([jnp.tile(w_cmp[0], (1, NSA_KV_HEADS)),
                              jnp.tile(w_cmp[1], (1, NSA_KV_HEADS))], axis=1)
    return wq, wkv, wg.astype(bf16), w_pack, wc_nat


def _kv_natural(kvp):
    m = kvp.shape[1]
    x = kvp.reshape(3, NSA_KV_HEADS, m, 2, HEAD_DIM)
    return x.transpose(2, 0, 3, 1, 4).reshape(m, 6 * NSA_KV_HEADS * HEAD_DIM)


def kernel(x_prompt, x_sample, cache_nsa_kv, cache_nsa_win, state_conv, state_pool, cache_mem_kv,
           page_table, mem_prompt, rel_bias, norm_g, norm_mem_g, final_g, w_ffn_in, w_ffn_out,
           w_nsa_in, w_nsa_cmp, w_nsa_out, w_conv_in, conv_w, w_conv_out, w_pool, pool_scale,
           w_ca_q, w_ca_kv, w_ca_out):
    n_b, seq, d = x_prompt.shape
    n_dec = x_sample.shape[0]
    depth = norm_g.shape[0]
    mem_len = mem_prompt.shape[1]
    page = cache_nsa_kv.shape[2]
    past_len = page_table.shape[1] * page
    g_, hd = NSA_KV_HEADS, HEAD_DIM
    gd = g_ * hd

    xp = x_prompt.reshape(n_b * seq, d)
    xs = x_sample.reshape(n_dec, d)
    mem = mem_prompt.reshape(n_b * mem_len, d)

    tsel, twin, cmpq, cmpt = _t5_prompt_tables(rel_bias, seq // NSA_BLOCK)
    n_blk_pad = -(-(past_len // NSA_BLOCK + 8) // LANES) * LANES
    bkey, bblk = _t5_sample_tables(rel_bias, past_len, past_len + LANES, n_blk_pad)

    nsa_kv_p, nsa_win_p, conv_p, pool_p, mem_p = [], [], [], [], []
    nsa_kv_s, nsa_win_s, conv_s, pool_s = [], [], [], []
    for i in range(depth):
        kind, j = i % 3, i // 3
        wf_in = w_ffn_in[i].astype(bf16)
        wf_out = w_ffn_out[i].astype(bf16)
        xp = _ffn(xp, norm_g[i, 0], wf_in[0], wf_out[0])
        xs = _ffn(xs, norm_g[i, 0], wf_in[0], wf_out[0])
        if kind == 0:
            wq, wkv, wg, w_pack, wc_nat = _nsa_weights(w_nsa_in[j], w_nsa_cmp[j])
            w_out = w_nsa_out[j].astype(bf16)
            q, kvp, gates, *slabs = _nsa_proj(xp, norm_g[i, 1], wq, wkv, wg, n_b, slab_pad=KV_PAD)
            kvc = _compress(kvp, w_pack, g_)
            o = _nsa_prompt_attn(q, gates, kvc, slabs, tsel, twin, cmpq, cmpt, n_b, seq)
            xp = _proj_res(o, w_out, xp)
            kv_nat = _kv_natural(kvp)
            nsa_kv_p.append(kv_nat[:, :4 * gd].reshape(n_b * seq // page, page, 4, g_, hd))
            n_keep = min(NSA_WINDOW, seq)
            nsa_win_p.append(kv_nat.reshape(n_b, seq, 6 * gd)[:, seq - n_keep:, 4 * gd:]
                             .reshape(n_b, n_keep, 2, g_, hd))
            q_s, kvp_s, gates_s = _nsa_proj(xs, norm_g[i, 1], wq, wkv, wg, 1)
            kv_nat_s = _kv_natural(kvp_s)
            q_hg = q_s.reshape(n_dec, g_, NSA_HPG, hd).transpose(0, 2, 1, 3).reshape(n_dec, NSA_HEADS, hd)
            q_hg = jnp.tile(q_hg, (1, 1, g_))
            gates_hg = gates_s.transpose(1, 0, 2)[:, :, :NSA_HPG * 3].reshape(n_dec, g_, NSA_HPG, 3)
            gates_hg = jnp.pad(gates_hg.transpose(0, 2, 1, 3).reshape(n_dec, NSA_HEADS, 3),
                               ((0, 0), (0, 0), (0, LANES - 3)))
            o_s, nwin = _nsa_sample_attn(page_table, cache_nsa_kv, cache_nsa_win, j, q_hg, kv_nat_s,
                                         gates_hg, wc_nat, bkey, bblk, past_len)
            o_s = o_s.reshape(n_dec, NSA_HPG, g_, hd).transpose(0, 2, 1, 3).reshape(n_dec, NSA_HEADS * hd)
            xs = _proj_res(o_s, w_out, xs)
            nsa_kv_s.append(kv_nat_s[:, :4 * gd].reshape(n_dec, 1, 4, g_, hd))
            nsa_win_s.append(nwin.reshape(n_dec, -1, 2, g_, hd))
        elif kind == 1:
            wc_in = w_conv_in[j].astype(bf16)
            wc_out = w_conv_out[j].astype(bf16)
            xp, cst = _conv_prompt(xp, norm_g[i, 1], wc_in, conv_w[j], wc_out, n_b, seq)
            xs, csts = _conv_sample(xs, norm_g[i, 1], state_conv[j], wc_in, conv_w[j], wc_out)
            conv_p.append(cst)
            conv_s.append(csts)
        else:
            wp = w_pool[j].astype(bf16)
            xp, pst = _pool_prompt(xp, norm_g[i, 1], wp, pool_scale[j], n_b, seq)
            xs, psts = _pool_sample(xs, norm_g[i, 1], state_pool[j], wp, pool_scale[j], past_len)
            pool_p.append(pst)
            pool_s.append(psts)
        w_q = w_ca_q[i].astype(bf16)
        w_o = w_ca_out[i].astype(bf16)
        mkv = _normproj(mem, norm_mem_g[i], w_ca_kv[i].astype(bf16))
        mem_p.append(mkv.reshape(n_b, mem_len, 2, CA_HEADS, d // CA_HEADS))
        xp = _ca_prompt(xp, norm_g[i, 2], w_q, mkv, w_o, n_b, seq, mem_len)
        q_ca = _normproj(xs, norm_g[i, 2], w_q)
        o_ca = _ca_sample(q_ca, cache_mem_kv, i)
        xs = _proj_res(o_ca, w_o, xs)
        fg = final_g if i == depth - 1 else None
        xp = _ffn(xp, norm_g[i, 3], wf_in[1], wf_out[1], final_g=fg)
        xs = _ffn(xs, norm_g[i, 3], wf_in[1], wf_out[1], final_g=fg)

    return (xp.reshape(n_b, seq, d), xs.reshape(n_dec, 1, d),
            jnp.stack(nsa_kv_p), jnp.stack(nsa_win_p), jnp.stack(conv_p), jnp.stack(pool_p),
            jnp.stack(mem_p),
            jnp.stack(nsa_kv_s), jnp.stack(nsa_win_s), jnp.stack(conv_s), jnp.stack(pool_s))
```

```python
import functools
import math

import jax
import jax.numpy as jnp
from jax import lax
from jax.experimental import pallas as pl
from jax.experimental.pallas import tpu as pltpu

f32 = jnp.float32
bf16 = jnp.bfloat16
i32 = jnp.int32

NORM_EPS = 1e-6
NEG_INF = -1e30

NSA_HEADS = 16
NSA_KV_HEADS = 4
NSA_HPG = NSA_HEADS // NSA_KV_HEADS
HEAD_DIM = 64
NSA_BLOCK = 64
NSA_TOPK = 16
NSA_WINDOW = 512
REL_BUCKETS = 32
REL_MAX_DIST = 128
CONV_W = 3
POOL_WINDOWS = (2, 4, 8, 16)
POOL_HIST = max(POOL_WINDOWS) - 1
CA_HEADS = 4

LANES = 128
VMEM_LIMIT_BYTES = 56 * 1024 * 1024

N_NEAR = 3
SEL_CHUNK_BLOCKS = 16
SEL_CHUNK = SEL_CHUNK_BLOCKS * NSA_BLOCK
NEAR_BLOCKS = 4
NEAR_TILE = NEAR_BLOCKS * NSA_BLOCK
WIN_TILE = 640
KV_PAD = 1024


def _cparams(*sem):
    return pltpu.CompilerParams(dimension_semantics=sem, vmem_limit_bytes=VMEM_LIMIT_BYTES)


def _rms(x, g):
    return x * lax.rsqrt(jnp.mean(x * x, axis=-1, keepdims=True) + NORM_EPS) * g


def _dot(a, b):
    return jnp.dot(a, b, preferred_element_type=f32)


def _dot_nt(a, b):
    return lax.dot_general(a, b, (((1,), (1,)), ((), ())), preferred_element_type=f32)


def _row_tile(m, target):
    t = min(m, target)
    while m % t:
        t //= 2
    return t


def _ffn_kernel(x_ref, g_ref, wg_ref, wu_ref, wo_ref, *rest, n_f, final_norm):
    if final_norm:
        gf_ref, o_ref, xn_ref, acc_ref = rest
    else:
        o_ref, xn_ref, acc_ref = rest
    j = pl.program_id(1)

    @pl.when(j == 0)
    def _():
        xn_ref[...] = _rms(x_ref[...], g_ref[...]).astype(bf16)
        acc_ref[...] = jnp.zeros_like(acc_ref)

    xn = xn_ref[...]
    gate = _dot(xn, wg_ref[...])
    up = _dot(xn, wu_ref[...])
    act = (gate * jax.nn.sigmoid(gate) * up).astype(bf16)
    acc_ref[...] += _dot(act, wo_ref[...])

    @pl.when(j == n_f - 1)
    def _():
        y = x_ref[...] + 0.5 * acc_ref[...]
        if final_norm:
            y = _rms(y, gf_ref[...])
        o_ref[...] = y


def _ffn(x, g, w_in, w_out, final_g=None, tm_target=512):
    m, d = x.shape
    f = w_out.shape[0]
    tm = _row_tile(m, tm_target)
    tf = f
    for cand in (1408, 1024, 512, 256, 128):
        if f % cand == 0:
            tf = cand
            break
    n_f = f // tf
    in_specs = [
        pl.BlockSpec((tm, d), lambda i, j: (i, 0)),
        pl.BlockSpec((1, d), lambda i, j: (0, 0)),
        pl.BlockSpec((d, tf), lambda i, j: (0, j)),
        pl.BlockSpec((d, tf), lambda i, j: (0, n_f + j)),
        pl.BlockSpec((tf, d), lambda i, j: (j, 0)),
    ]
    args = [x, g.reshape(1, d), w_in, w_in, w_out]
    if final_g is not None:
        in_specs.append(pl.BlockSpec((1, d), lambda i, j: (0, 0)))
        args.append(final_g.reshape(1, d))
    return pl.pallas_call(
        functools.partial(_ffn_kernel, n_f=n_f, final_norm=final_g is not None),
        grid=(m // tm, n_f),
        in_specs=in_specs,
        out_specs=pl.BlockSpec((tm, d), lambda i, j: (i, 0)),
        out_shape=jax.ShapeDtypeStruct((m, d), f32),
        scratch_shapes=[pltpu.VMEM((tm, d), bf16), pltpu.VMEM((tm, d), f32)],
        compiler_params=_cparams("parallel", "arbitrary"),
        name="ffn",
    )(*args)


def _normproj_kernel(x_ref, g_ref, w_ref, o_ref):
    xn = _rms(x_ref[...], g_ref[...]).astype(bf16)
    o_ref[...] = _dot(xn, w_ref[...])


def _normproj(x, g, w, tm_target=512):
    m, d = x.shape
    n = w.shape[1]
    tm = _row_tile(m, tm_target)
    return pl.pallas_call(
        _normproj_kernel,
        grid=(m // tm,),
        in_specs=[pl.BlockSpec((tm, d), lambda i: (i, 0)),
                  pl.BlockSpec((1, d), lambda i: (0, 0)),
                  pl.BlockSpec((d, n), lambda i: (0, 0))],
        out_specs=pl.BlockSpec((tm, n), lambda i: (i, 0)),
        out_shape=jax.ShapeDtypeStruct((m, n), f32),
        compiler_params=_cparams("parallel"),
        name="normproj",
    )(x, g.reshape(1, d), w)


def _proj_res_kernel(a_ref, w_ref, x_ref, o_ref):
    o_ref[...] = x_ref[...] + _dot(a_ref[...].astype(bf16), w_ref[...])


def _proj_res(a, w, x, tm_target=512):
    m, k = a.shape
    d = w.shape[1]
    tm = _row_tile(m, tm_target)
    return pl.pallas_call(
        _proj_res_kernel,
        grid=(m // tm,),
        in_specs=[pl.BlockSpec((tm, k), lambda i: (i, 0)),
                  pl.BlockSpec((k, d), lambda i: (0, 0)),
                  pl.BlockSpec((tm, d), lambda i: (i, 0))],
        out_specs=pl.BlockSpec((tm, d), lambda i: (i, 0)),
        out_shape=jax.ShapeDtypeStruct((m, d), f32),
        compiler_params=_cparams("parallel"),
        name="proj_res",
    )(a, w, x)


def _nsa_proj_kernel(x_ref, g_ref, wq_ref, wkv_ref, wg_ref, q_ref, kv_ref, gt_ref, *slab_refs, n_pad):
    t = pl.program_id(1) - n_pad

    @pl.when(t >= 0)
    def _():
        xn = _rms(x_ref[...], g_ref[...]).astype(bf16)
        q_ref[...] = _dot(xn, wq_ref[...])
        kv = _dot(xn, wkv_ref[...])
        n_g = kv_ref.shape[0] // 3
        tm = kv.shape[0]
        for s in range(kv_ref.shape[0]):
            kv_ref[s] = kv[:, s * LANES:(s + 1) * LANES]
        gt = jax.nn.sigmoid(_dot(xn, wg_ref[...]))
        for s in range(gt_ref.shape[0]):
            gt_ref[s] = gt[:, s * LANES:(s + 1) * LANES]
        if slab_refs:
            ks_ref, vs_ref, kw_ref, vw_ref = slab_refs
            lane = lax.broadcasted_iota(i32, (tm, LANES), 1)
            lower = lane < HEAD_DIM
            blk = (t * tm + lax.broadcasted_iota(i32, (tm, LANES), 0)) // NSA_BLOCK
            onehot = jnp.where(lane - HEAD_DIM == blk, 1.0, 0.0)
            for gi in range(n_g):
                sel = kv[:, (n_g + gi) * LANES:(n_g + gi + 1) * LANES]
                win = kv[:, (2 * n_g + gi) * LANES:(2 * n_g + gi + 1) * LANES]
                ks_ref[gi, 0] = jnp.where(lower, sel, onehot).astype(bf16)
                vs_ref[gi, 0] = jnp.where(lower, 1.0, sel).astype(bf16)
                kw_ref[gi, 0] = jnp.where(lower, win, 0.0).astype(bf16)
                vw_ref[gi, 0] = jnp.where(lower, 1.0, win).astype(bf16)

    if slab_refs:
        @pl.when(t < 0)
        def _():
            for ref in slab_refs:
                ref[...] = jnp.zeros(ref.shape, ref.dtype)


def _nsa_proj(x, g, wq, wkv, wg, n_batch, slab_pad=None, tm_target=512):
    m, d = x.shape
    seq = m // n_batch
    nq, nkv, ng = wq.shape[1], wkv.shape[1], wg.shape[1]
    n_slab, n_gs = nkv // LANES, ng // LANES
    tm = _row_tile(seq, tm_target)
    n_t = seq // tm
    n_pad = 0 if slab_pad is None else slab_pad // tm

    def row(b, t):
        return b * n_t + jnp.maximum(t - n_pad, 0)

    out_specs = [pl.BlockSpec((tm, nq), lambda b, t: (row(b, t), 0)),
                 pl.BlockSpec((n_slab, tm, LANES), lambda b, t: (0, row(b, t), 0)),
                 pl.BlockSpec((n_gs, tm, LANES), lambda b, t: (0, row(b, t), 0))]
    out_shape = [jax.ShapeDtypeStruct((m, nq), f32),
                 jax.ShapeDtypeStruct((n_slab, m, LANES), f32),
                 jax.ShapeDtypeStruct((n_gs, m, LANES), f32)]
    if slab_pad is not None:
        assert slab_pad % tm == 0
        n_g = n_slab // 3
        out_specs += [pl.BlockSpec((n_g, 1, tm, LANES), lambda b, t: (0, b, t, 0))] * 4
        out_shape += [jax.ShapeDtypeStruct((n_g, n_batch, slab_pad + seq, LANES), bf16)] * 4
    return pl.pallas_call(
        functools.partial(_nsa_proj_kernel, n_pad=n_pad),
        grid=(n_batch, n_t + n_pad),
        in_specs=[pl.BlockSpec((tm, d), lambda b, t: (row(b, t), 0)),
                  pl.BlockSpec((1, d), lambda b, t: (0, 0)),
                  pl.BlockSpec((d, nq), lambda b, t: (0, 0)),
                  pl.BlockSpec((d, nkv), lambda b, t: (0, 0)),
                  pl.BlockSpec((d, ng), lambda b, t: (0, 0))],
        out_specs=out_specs,
        out_shape=out_shape,
        compiler_params=_cparams("parallel", "arbitrary"),
        name="nsa_proj",
    )(x, g.reshape(1, d), wq, wkv, wg)


def _compress_kernel(kv_ref, w_ref, o_ref):
    tm = kv_ref.shape[1]
    nb = tm // NSA_BLOCK
    x = kv_ref[0].reshape(nb, NSA_BLOCK, LANES)
    o_ref[0] = jnp.sum(x * w_ref[...][None], axis=1)


def _compress(kvp, w_pack, n_groups, tm_target=512):
    m = kvp.shape[1]
    tm = _row_tile(m, tm_target)
    nb = tm // NSA_BLOCK
    return pl.pallas_call(
        _compress_kernel,
        grid=(n_groups, m // tm),
        in_specs=[pl.BlockSpec((1, tm, LANES), lambda g, i: (g, i, 0)),
                  pl.BlockSpec((NSA_BLOCK, LANES), lambda g, i: (0, 0))],
        out_specs=pl.BlockSpec((1, nb, LANES), lambda g, i: (g, i, 0)),
        out_shape=jax.ShapeDtypeStruct((n_groups, m // NSA_BLOCK, LANES), f32),
        compiler_params=_cparams("parallel", "parallel"),
        name="nsa_compress",
    )(kvp, w_pack)


def _t5_bucket(rel):
    n = jnp.maximum(rel, 0)
    max_exact = REL_BUCKETS // 2
    nf = jnp.maximum(n, 1).astype(f32)
    large = max_exact + (jnp.log(nf / max_exact) / math.log(REL_MAX_DIST / max_exact)
                         * (REL_BUCKETS - max_exact)).astype(i32)
    large = jnp.minimum(large, REL_BUCKETS - 1)
    return jnp.where(n < max_exact, n, large)


def _bias_lookup(table_ref, bucket, head):
    out = jnp.zeros(bucket.shape, f32)
    for k in range(REL_BUCKETS):
        out = jnp.where(bucket == k, table_ref[k, head], out)
    return out


def _t5_prompt_kernel(table_ref, tsel_ref, twin_ref, cmpq_ref, cmpt_ref):
    g = pl.program_id(0)
    n_bc = cmpq_ref.shape[3]
    rel_s = (lax.broadcasted_iota(i32, (NSA_BLOCK, NEAR_TILE), 0) + (NEAR_TILE - NSA_BLOCK)
             - lax.broadcasted_iota(i32, (NSA_BLOCK, NEAR_TILE), 1))
    rel_w = (lax.broadcasted_iota(i32, (NSA_BLOCK, WIN_TILE), 0) + (WIN_TILE - NSA_BLOCK)
             - lax.broadcasted_iota(i32, (NSA_BLOCK, WIN_TILE), 1))
    bucket_s = _t5_bucket(rel_s)
    bucket_w = _t5_bucket(rel_w)
    qi_b = lax.broadcasted_iota(i32, (NSA_BLOCK, n_bc), 0)
    shp_t = (8, NSA_BLOCK)
    bucket_t = _t5_bucket(lax.broadcasted_iota(i32, shp_t, 1) - (NSA_BLOCK - 1)
                          + NSA_BLOCK * lax.broadcasted_iota(i32, shp_t, 0))
    for hh in range(NSA_HPG):
        head = g * NSA_HPG + hh
        far = table_ref[REL_BUCKETS - 1, head]
        rows = slice(hh * NSA_BLOCK, (hh + 1) * NSA_BLOCK)
        tsel_ref[0, rows, :] = jnp.where(rel_s >= 0, _bias_lookup(table_ref, bucket_s, head) - far, NEG_INF)
        twin_ref[0, rows, :] = jnp.where((rel_w >= 0) & (rel_w <= NSA_WINDOW),
                                         _bias_lookup(table_ref, bucket_w, head) - far, NEG_INF)
        for d in range(N_NEAR):
            cmpq_ref[0, d, rows, :] = _bias_lookup(
                table_ref, _t5_bucket(qi_b - (NSA_BLOCK - 1) + NSA_BLOCK * d), head) - far
        cmpt_ref[0, :, rows] = _bias_lookup(table_ref, bucket_t, head) - far


def _t5_prompt_tables(rel_bias, n_blk):
    rows = NSA_HPG * NSA_BLOCK
    return pl.pallas_call(
        _t5_prompt_kernel,
        grid=(NSA_KV_HEADS,),
        in_specs=[pl.BlockSpec(memory_space=pltpu.SMEM)],
        out_specs=[pl.BlockSpec((1, rows, NEAR_TILE), lambda g: (g, 0, 0)),
                   pl.BlockSpec((1, rows, WIN_TILE), lambda g: (g, 0, 0)),
                   pl.BlockSpec((1, N_NEAR, rows, n_blk), lambda g: (g, 0, 0, 0)),
                   pl.BlockSpec((1, 8, rows), lambda g: (g, 0, 0))],
        out_shape=[jax.ShapeDtypeStruct((NSA_KV_HEADS, rows, NEAR_TILE), f32),
                   jax.ShapeDtypeStruct((NSA_KV_HEADS, rows, WIN_TILE), f32),
                   jax.ShapeDtypeStruct((NSA_KV_HEADS, N_NEAR, rows, n_blk), f32),
                   jax.ShapeDtypeStruct((NSA_KV_HEADS, 8, rows), f32)],
        compiler_params=_cparams("parallel"),
        name="t5_prompt_tables",
    )(rel_bias)


def _t5_sample_kernel(table_ref, key_ref, blk_ref, *, qpos):
    kpos = lax.broadcasted_iota(i32, (1, key_ref.shape[1]), 1)
    bucket_k = _t5_bucket(qpos - kpos)
    blk = lax.broadcasted_iota(i32, (1, blk_ref.shape[1]), 1)
    bucket_b = _t5_bucket(qpos - (blk * NSA_BLOCK + NSA_BLOCK - 1))
    for r in range(NSA_HEADS):
        head = (r % NSA_KV_HEADS) * NSA_HPG + r // NSA_KV_HEADS
        key_ref[r:r + 1, :] = _bias_lookup(table_ref, bucket_k, head)
        blk_ref[r:r + 1, :] = _bias_lookup(table_ref, bucket_b, head)


def _t5_sample_tables(rel_bias, qpos, n_key, n_blk):
    return pl.pallas_call(
        functools.partial(_t5_sample_kernel, qpos=qpos),
        in_specs=[pl.BlockSpec(memory_space=pltpu.SMEM)],
        out_shape=[jax.ShapeDtypeStruct((NSA_HEADS, n_key), f32),
                   jax.ShapeDtypeStruct((NSA_HEADS, n_blk), f32)],
        name="t5_sample_tables",
    )(rel_bias)


def _flash_update(carry, s, v_aug):
    m, acc = carry
    m_new = jnp.maximum(m, jnp.max(s, axis=1, keepdims=True))
    p = jnp.exp(s - m_new).astype(bf16)
    return m_new, jnp.exp(m - m_new) * acc + _dot(p, v_aug)


def _flash_first(s, v_aug):
    m = jnp.max(s, axis=1, keepdims=True)
    return m, _dot(jnp.exp(s - m).astype(bf16), v_aug)


def _flash_finish(carry):
    _, acc = carry
    return acc / pltpu.roll(acc, HEAD_DIM, 1)


def _tile_rows(x, n):
    return jnp.concatenate([x] * n, axis=0)


def _select_blocks(imp_t, c, n_blk):
    blk = lax.broadcasted_iota(i32, imp_t.shape, 0)
    valid = blk <= c
    forced = (blk == 0) | (blk == c) | (blk == c - 1)

    blk_f = blk.astype(f32)
    work = jnp.where(valid & jnp.logical_not(forced), imp_t, -1.0)
    sel = forced.astype(f32)
    for _ in range(NSA_TOPK - 3):
        mx = jnp.max(work, axis=0, keepdims=True)
        first = jnp.min(jnp.where(work == mx, blk_f, float(n_blk)), axis=0, keepdims=True)
        pick = blk_f == first
        work = jnp.where(pick, -2.0, work)
        sel = jnp.where(pick, 1.0, sel)
    return jnp.where(c + 1 <= NSA_TOPK, valid.astype(f32), sel)


def _nsa_prompt_kernel(*refs, n_blk, gps):
    q_ref, gt_ref, kvc_ref = refs[0:3]
    slab_refs = [refs[3 + 4 * i:3 + 4 * (i + 1)] for i in range(gps)]
    tsel_ref, twin_ref, cmpq_ref, cmpt_ref, o_ref = refs[3 + 4 * gps:]
    c = pl.program_id(2)
    rows = NSA_HPG * NSA_BLOCK
    wq = NSA_HPG * HEAD_DIM
    pre = _nsa_prompt_cmp(c, q_ref, kvc_ref, cmpq_ref, cmpt_ref, gps, n_blk)
    sel_all = _select_blocks(jnp.concatenate([p[3] for p in pre], axis=1), c, n_blk).astype(bf16)
    fronts = [_nsa_prompt_front(c, pre[i], sel_all[:, i * LANES:(i + 1) * LANES], n_blk) for i in range(gps)]
    o_win = _nsa_prompt_window(c, [p[1] for p in pre], slab_refs, twin_ref)

    def far_body(jj, carry):
        m, accs = carry
        start = pl.multiple_of(KV_PAD + jj * SEL_CHUNK, SEL_CHUNK // 8)
        s = jnp.concatenate([_dot_nt(f[1], refs[0][0, 0, pl.ds(start, SEL_CHUNK), :])
                             for f, refs in zip(fronts, slab_refs)], axis=0)
        m_new = jnp.maximum(m, jnp.max(s, axis=1, keepdims=True))
        p = jnp.exp(s - m_new).astype(bf16)
        alpha = jnp.exp(m - m_new)
        accs = tuple(alpha[i * rows:(i + 1) * rows] * accs[i]
                     + _dot(p[i * rows:(i + 1) * rows], slab_refs[i][1][0, 0, pl.ds(start, SEL_CHUNK), :])
                     for i in range(gps))
        return m_new, accs

    n_far = jnp.maximum(c - (NEAR_BLOCKS - 1), 0)
    init = (jnp.full((gps * rows, 1), NEG_INF, f32), tuple(jnp.zeros((rows, LANES), f32) for _ in range(gps)))
    m_far, acc_far = lax.fori_loop(0, (n_far + SEL_CHUNK_BLOCKS - 1) // SEL_CHUNK_BLOCKS, far_body, init)
    carries = [(m_far[i * rows:(i + 1) * rows], acc_far[i]) for i in range(gps)]

    lower = lax.broadcasted_iota(i32, (NSA_BLOCK, LANES), 1) < HEAD_DIM
    lane_g = lax.broadcasted_iota(i32, (NSA_BLOCK, LANES), 1)
    for i in range(gps):
        q_pad, _, near_hidden, o_c = fronts[i]
        o_w = o_win[i]
        ks_ref, vs_ref = slab_refs[i][0:2]
        start = pl.multiple_of((c + 1) * NSA_BLOCK + (KV_PAD - NEAR_TILE), NSA_BLOCK)
        s_near = (_dot_nt(q_pad, ks_ref[0, 0, pl.ds(start, NEAR_TILE), :]) + tsel_ref[i]
                  + _tile_rows(near_hidden, NSA_HPG))
        o_s = _flash_finish(_flash_update(carries[i], s_near, vs_ref[0, 0, pl.ds(start, NEAR_TILE), :]))
        gt = gt_ref[i]

        def gate(h, branch):
            return jnp.sum(jnp.where(lane_g == h * 3 + branch, gt, 0.0), axis=1, keepdims=True)

        heads = []
        for h in range(NSA_HPG):
            r = slice(h * NSA_BLOCK, (h + 1) * NSA_BLOCK)
            heads.append(gate(h, 0) * o_c[r] + gate(h, 1) * o_s[r] + gate(h, 2) * o_w[r])
        for pr in range(NSA_HPG // 2):
            o_ref[:, i * wq + pr * LANES:i * wq + (pr + 1) * LANES] = jnp.where(
                lower, pltpu.roll(heads[2 * pr], HEAD_DIM, 1), heads[2 * pr + 1])


def _nsa_prompt_cmp(c, q_ref, kvc_ref, cmpq_ref, cmpt_ref, gps, n_blk):
    rows = NSA_HPG * NSA_BLOCK
    wq = NSA_HPG * HEAD_DIM
    scale = HEAD_DIM ** -0.5
    lower = lax.broadcasted_iota(i32, (NSA_BLOCK, LANES), 1) < HEAD_DIM
    blk = lax.broadcasted_iota(i32, (rows, n_blk), 1)
    blk_t = lax.broadcasted_iota(i32, (n_blk, rows), 0)

    q_parts, q_pads, kvcs, sc_raw, st_raw = [], [], [], [], []
    for i in range(gps):
        parts = []
        for h in range(NSA_HPG):
            x = q_ref[:, i * wq + (h // 2) * LANES:i * wq + (h // 2 + 1) * LANES] * scale
            if h % 2:
                x = pltpu.roll(x, HEAD_DIM, 1)
            parts.append(jnp.where(lower, x, 0.0))
        q_pad = jnp.concatenate(parts, axis=0).astype(bf16)
        kvc = kvc_ref[i].astype(bf16)
        bias_c = jnp.where(blk == c, cmpq_ref[i, 0],
                           jnp.where(blk == c - 1, cmpq_ref[i, 1],
                                     jnp.where(blk == c - 2, cmpq_ref[i, 2], 0.0)))
        cols = cmpt_ref[i]
        bias_t = jnp.where(blk_t == c, cols[0:1, :],
                           jnp.where(blk_t == c - 1, cols[1:2, :],
                                     jnp.where(blk_t == c - 2, cols[2:3, :], 0.0)))
        q_parts.append(parts)
        q_pads.append(q_pad)
        kvcs.append(kvc)
        sc_raw.append(_dot_nt(q_pad, kvc) + bias_c)
        st_raw.append(_dot_nt(kvc, q_pad) + bias_t)

    blk_a = lax.broadcasted_iota(i32, (gps * rows, n_blk), 1)
    qi_a = lax.broadcasted_iota(i32, (gps * rows, n_blk), 0) & (NSA_BLOCK - 1)
    mask_c = (blk_a < c) | ((blk_a == c) & (qi_a == NSA_BLOCK - 1))
    s_c = jnp.where(mask_c, jnp.concatenate(sc_raw, axis=0), NEG_INF)
    e_c = jnp.where(mask_c, jnp.exp(s_c - jnp.max(s_c, axis=1, keepdims=True)), 0.0)
    l_c = jnp.sum(e_c, axis=1, keepdims=True)
    p_c = (e_c / jnp.where(l_c > 0.0, l_c, 1.0)).astype(bf16)

    blk_b = lax.broadcasted_iota(i32, (n_blk, gps * rows), 0)
    qi_b = lax.broadcasted_iota(i32, (n_blk, gps * rows), 1) & (NSA_BLOCK - 1)
    mask_t = (blk_b < c) | ((blk_b == c) & (qi_b == NSA_BLOCK - 1))
    s_t = jnp.where(mask_t, jnp.concatenate(st_raw, axis=1), NEG_INF)
    e_t = jnp.where(mask_t, jnp.exp(s_t - jnp.max(s_t, axis=0, keepdims=True)), 0.0)
    l_t = jnp.sum(e_t, axis=0, keepdims=True)
    p_t = e_t / jnp.where(l_t > 0.0, l_t, 1.0)

    out = []
    for i in range(gps):
        o_c = _dot(p_c[i * rows:(i + 1) * rows], kvcs[i])
        imp_t = p_t[:, i * rows:i * rows + LANES]
        for pr in range(1, NSA_HPG // 2):
            imp_t = imp_t + p_t[:, i * rows + pr * LANES:i * rows + (pr + 1) * LANES]
        imp_t = imp_t + pltpu.roll(imp_t, NSA_BLOCK, 1)
        out.append((q_parts[i], q_pads[i], o_c, imp_t))
    return out


def _nsa_prompt_front(c, pre, sel_t, n_blk):
    q_parts, q_pad, o_c, _ = pre
    lower = lax.broadcasted_iota(i32, (NSA_BLOCK, LANES), 1) < HEAD_DIM
    eye = (lax.broadcasted_iota(i32, (NSA_BLOCK, LANES), 0)
           == lax.broadcasted_iota(i32, (NSA_BLOCK, LANES), 1)).astype(bf16)
    pieces = [jnp.zeros((HEAD_DIM, LANES), bf16), sel_t]
    if n_blk < LANES - HEAD_DIM:
        pieces.append(jnp.zeros((LANES - HEAD_DIM - n_blk, LANES), bf16))
    sel_hi = _dot_nt(eye, jnp.concatenate(pieces, axis=0))
    lane = lax.broadcasted_iota(i32, (NSA_BLOCK, LANES), 1)
    first_near = c - (NEAR_BLOCKS - 1)
    far_mask = jnp.where((sel_hi > 0.5) & (lane - HEAD_DIM < first_near), 0.0, NEG_INF)
    q_aug = jnp.concatenate([jnp.where(lower, x, far_mask) for x in q_parts], axis=0).astype(bf16)

    def sel_col(b):
        col = jnp.sum(jnp.where(lane == HEAD_DIM + b, sel_hi, 0.0), axis=1, keepdims=True)
        return jnp.broadcast_to(col, (NSA_BLOCK, NEAR_TILE))

    near_blk = lax.broadcasted_iota(i32, (NSA_BLOCK, NEAR_TILE), 1) >> 6
    has_prev = jnp.where(c >= 1, 1.0, 0.0)
    visible = jnp.where(near_blk == 0, sel_col(c - 3),
                        jnp.where(near_blk == 1, sel_col(c - 2), jnp.where(near_blk == 2, has_prev, 1.0)))
    near_hidden = jnp.where(visible > 0.5, 0.0, NEG_INF)

    return q_pad, q_aug, near_hidden, o_c


def _nsa_prompt_window(c, q_pads, slab_refs, twin_ref):
    rows = NSA_HPG * NSA_BLOCK
    start_w = pl.multiple_of((c + 1) * NSA_BLOCK + (KV_PAD - WIN_TILE), NSA_BLOCK)
    col_w = lax.broadcasted_iota(i32, (1, WIN_TILE), 1)
    before_start = jnp.where(col_w < WIN_TILE - (c + 1) * NSA_BLOCK, NEG_INF, 0.0)
    s_w = jnp.concatenate([_dot_nt(q_pad, refs[2][0, 0, pl.ds(start_w, WIN_TILE), :]) + twin_ref[i]
                           for i, (q_pad, refs) in enumerate(zip(q_pads, slab_refs))], axis=0) + before_start
    p = jnp.exp(s_w - jnp.max(s_w, axis=1, keepdims=True)).astype(bf16)
    return [_flash_finish((None, _dot(p[i * rows:(i + 1) * rows], refs[3][0, 0, pl.ds(start_w, WIN_TILE), :])))
            for i, refs in enumerate(slab_refs)]


def _nsa_prompt_attn(q, gates, kvc, slabs, tsel, twin, cmpq, cmpt, n_batch, seq, gps=4):
    m = q.shape[0]
    n_blk = seq // NSA_BLOCK
    assert n_blk <= HEAD_DIM, "the block one-hot lives in the 64 spare lanes of the key slab"
    g_ = NSA_KV_HEADS
    wq = NSA_HPG * HEAD_DIM
    slab = (1, 1) + slabs[0].shape[2:]

    def slab_spec(i):
        return pl.BlockSpec(slab, lambda b, gg, c: (gg * gps + i, b, 0, 0))

    return pl.pallas_call(
        functools.partial(_nsa_prompt_kernel, n_blk=n_blk, gps=gps),
        grid=(n_batch, g_ // gps, n_blk),
        in_specs=[
            pl.BlockSpec((NSA_BLOCK, gps * wq), lambda b, gg, c: (b * n_blk + c, gg)),
            pl.BlockSpec((gps, NSA_BLOCK, LANES), lambda b, gg, c: (gg, b * n_blk + c, 0)),
            pl.BlockSpec((gps, n_blk, LANES), lambda b, gg, c: (gg, b, 0)),
            *[slab_spec(i) for i in range(gps) for _ in slabs],
            pl.BlockSpec((gps,) + tsel.shape[1:], lambda b, gg, c: (gg, 0, 0)),
            pl.BlockSpec((gps,) + twin.shape[1:], lambda b, gg, c: (gg, 0, 0)),
            pl.BlockSpec((gps,) + cmpq.shape[1:], lambda b, gg, c: (gg, 0, 0, 0)),
            pl.BlockSpec((gps,) + cmpt.shape[1:], lambda b, gg, c: (gg, 0, 0)),
        ],
        out_specs=pl.BlockSpec((NSA_BLOCK, gps * wq), lambda b, gg, c: (b * n_blk + c, gg)),
        out_shape=jax.ShapeDtypeStruct((m, NSA_HEADS * HEAD_DIM), f32),
        compiler_params=_cparams("parallel", "parallel", "arbitrary"),
        name="nsa_prompt_attn",
    )(q, gates, kvc, *(list(slabs) * gps), tsel, twin, cmpq, cmpt)


def _group_lane_mask(shape):
    r = lax.broadcasted_iota(i32, shape, 0)
    ln = lax.broadcasted_iota(i32, shape, 1)
    return (ln >> 6) == (r & (NSA_KV_HEADS - 1))


def _fold_groups(x):
    x = jnp.where(_group_lane_mask(x.shape), x, 0.0)
    out = x[:, 0:LANES]
    for pr in range(1, x.shape[1] // LANES):
        out = out + x[:, pr * LANES:(pr + 1) * LANES]
    return out + pltpu.roll(out, HEAD_DIM, 1)


def _select_blocks_sample(imp, cur, n_slots):
    imp_t = imp.T
    blk = lax.broadcasted_iota(i32, imp_t.shape, 0)
    valid = blk <= cur
    forced = (blk == 0) | (blk == cur) | (blk == cur - 1)
    if cur + 1 <= NSA_TOPK:
        return valid.astype(f32).T
    blk_f = blk.astype(f32)
    work = jnp.where(valid & jnp.logical_not(forced), imp_t, -1.0)
    sel = forced.astype(f32)
    for _ in range(NSA_TOPK - 3):
        mx = jnp.max(work, axis=0, keepdims=True)
        first = jnp.min(jnp.where(work == mx, blk_f, float(n_slots)), axis=0, keepdims=True)
        pick = blk_f == first
        work = jnp.where(pick, -2.0, work)
        sel = jnp.where(pick, 1.0, sel)
    return sel.T


def _nsa_sample_kernel(*refs, n_pages, page, past_len):
    pt_ref = refs[0]
    page_refs = refs[1:1 + n_pages]
    (win_ref, q_ref, kvn_ref, gt_ref, wc_ref, bkey_ref, bblk_ref, emat_ref,
     o_ref, nwin_ref, kcv_scr) = refs[1 + n_pages:]
    del pt_ref
    b = pl.program_id(0)
    gd = NSA_KV_HEADS * HEAD_DIM
    bpp = page // NSA_BLOCK
    n_past_blk = past_len // NSA_BLOCK
    n_slots = kcv_scr.shape[0]
    cur = n_past_blk
    scale = HEAD_DIM ** -0.5

    kvn = kvn_ref[pl.ds(b, 1), :]
    qm = jnp.where(_group_lane_mask((NSA_HEADS, gd)), q_ref[0] * scale, 0.0)
    qm_b = qm.astype(bf16)

    wc = wc_ref[...]
    for k in range(n_pages):
        cmp_part = page_refs[k][0, 0, :, 0:2 * gd].reshape(bpp, NSA_BLOCK, 2 * gd)
        kcv_scr[k * bpp:(k + 1) * bpp, :] = jnp.sum(cmp_part * wc[None], axis=1)
    row8 = lax.broadcasted_iota(i32, (8, 2 * gd), 0)
    kcv_scr[n_past_blk:n_past_blk + 8, :] = jnp.where(row8 == 0, kvn[:, 0:2 * gd] * wc[0:1, :], 0.0)
    rest = n_slots - n_past_blk - 8
    kcv_scr[n_past_blk + 8:n_slots, :] = jnp.zeros((rest, 2 * gd), f32)
    kcv = kcv_scr[...]
    kc_b = kcv[:, 0:gd].astype(bf16)
    vc_b = kcv[:, gd:2 * gd].astype(bf16)

    blk = lax.broadcasted_iota(i32, (NSA_HEADS, n_slots), 1)
    mask_c = blk * NSA_BLOCK + (NSA_BLOCK - 1) <= past_len
    s_c = jnp.where(mask_c, _dot_nt(qm_b, kc_b) + bblk_ref[...], NEG_INF)
    m_c = jnp.max(s_c, axis=1, keepdims=True)
    e_c = jnp.where(mask_c, jnp.exp(s_c - m_c), 0.0)
    l_c = jnp.sum(e_c, axis=1, keepdims=True)
    p_c = e_c / jnp.where(l_c > 0.0, l_c, 1.0)
    o_c = _fold_groups(_dot(p_c.astype(bf16), vc_b))

    imp = p_c[0:NSA_KV_HEADS]
    for hh in range(1, NSA_HPG):
        imp = imp + p_c[hh * NSA_KV_HEADS:(hh + 1) * NSA_KV_HEADS]
    sel = _select_blocks_sample(jnp.concatenate([imp, jnp.zeros_like(imp)], axis=0), cur, n_slots)
    sel16 = jnp.concatenate([sel[0:NSA_KV_HEADS]] * NSA_HPG, axis=0).astype(bf16)

    k_new = kvn[:, 2 * gd:3 * gd]
    v_new = kvn[:, 3 * gd:4 * gd]
    hidden = (_dot(sel16, emat_ref[...]) - 1.0) * (-NEG_INF)
    b_self = bkey_ref[:, past_len:past_len + LANES][:, 0:1]
    s_parts = [_dot_nt(qm_b, page_refs[k][0, 0, :, 2 * gd:3 * gd].astype(bf16)) for k in range(n_pages)]
    s_s = jnp.concatenate(s_parts, axis=1) + bkey_ref[:, 0:past_len] + hidden
    s_self = jnp.sum(qm * k_new, axis=1, keepdims=True) + b_self
    m_s = jnp.maximum(jnp.max(s_s, axis=1, keepdims=True), s_self)
    e_s = jnp.exp(s_s - m_s)
    e_self = jnp.exp(s_self - m_s)
    l_s = jnp.sum(e_s, axis=1, keepdims=True) + e_self
    e_sb = e_s.astype(bf16)
    pv = e_self * v_new
    for k in range(n_pages):
        pv = pv + _dot(e_sb[:, k * page:(k + 1) * page], page_refs[k][0, 0, :, 3 * gd:4 * gd].astype(bf16))
    o_s = _fold_groups(pv) / l_s

    n_buf = win_ref.shape[2]
    kw_b = win_ref[0, 0, :, 0:gd].astype(bf16)
    vw_b = win_ref[0, 0, :, gd:2 * gd].astype(bf16)
    kw_new = kvn[:, 4 * gd:5 * gd]
    vw_new = kvn[:, 5 * gd:6 * gd]
    wpos = past_len - n_buf + lax.broadcasted_iota(i32, (NSA_HEADS, n_buf), 1)
    msk_w = (past_len - wpos <= NSA_WINDOW) & (wpos >= 0)
    s_w = jnp.where(msk_w, _dot_nt(qm_b, kw_b) + bkey_ref[:, past_len - n_buf:past_len], NEG_INF)
    sw_self = jnp.sum(qm * kw_new, axis=1, keepdims=True) + b_self
    m_w = jnp.maximum(jnp.max(s_w, axis=1, keepdims=True), sw_self)
    e_w = jnp.where(msk_w, jnp.exp(s_w - m_w), 0.0)
    ew_self = jnp.exp(sw_self - m_w)
    l_w = jnp.sum(e_w, axis=1, keepdims=True) + ew_self
    o_w = _fold_groups(_dot(e_w.astype(bf16), vw_b) + ew_self * vw_new) / l_w

    gt = gt_ref[0]
    lane_g = lax.broadcasted_iota(i32, gt.shape, 1)

    def gate(branch):
        return jnp.sum(jnp.where(lane_g == branch, gt, 0.0), axis=1, keepdims=True)

    o_ref[0] = gate(0) * o_c + gate(1) * o_s + gate(2) * o_w

    nwin_ref[0, 0, 0:n_buf - 1, :] = win_ref[0, 0, 1:n_buf, :]
    nwin_ref[0, 0, n_buf - 1:n_buf, :] = kvn[:, 4 * gd:6 * gd]


def _nsa_sample_attn(page_table, cache_kv, cache_win, layer, q_hg, kv_nat, gates_hg, wc_nat, bkey, bblk,
                     past_len):
    n_dec, n_pages = page_table.shape
    _, n_phys, page, _, g_, d_ = cache_kv.shape
    gd = g_ * d_
    n_buf = cache_win.shape[2]
    n_slots = bblk.shape[1]
    cache2 = cache_kv.reshape(cache_kv.shape[0], n_phys, page, 4 * gd)
    win2 = cache_win.reshape(cache_win.shape[0], n_dec, n_buf, 2 * gd)
    emat = (lax.broadcasted_iota(i32, (n_slots, past_len), 0)
            == lax.broadcasted_iota(i32, (n_slots, past_len), 1) // NSA_BLOCK).astype(bf16)

    def page_spec(k):
        return pl.BlockSpec((1, 1, page, 4 * gd), lambda b, pt: (layer, pt[b * n_pages + k], 0, 0))

    grid_spec = pltpu.PrefetchScalarGridSpec(
        num_scalar_prefetch=1,
        grid=(n_dec,),
        in_specs=[
            *[page_spec(k) for k in range(n_pages)],
            pl.BlockSpec((1, 1, n_buf, 2 * gd), lambda b, pt: (layer, b, 0, 0)),
            pl.BlockSpec((1, NSA_HEADS, gd), lambda b, pt: (b, 0, 0)),
            pl.BlockSpec(kv_nat.shape, lambda b, pt: (0, 0)),
            pl.BlockSpec((1, NSA_HEADS, LANES), lambda b, pt: (b, 0, 0)),
            pl.BlockSpec(wc_nat.shape, lambda b, pt: (0, 0)),
            pl.BlockSpec(bkey.shape, lambda b, pt: (0, 0)),
            pl.BlockSpec(bblk.shape, lambda b, pt: (0, 0)),
            pl.BlockSpec(emat.shape, lambda b, pt: (0, 0)),
        ],
        out_specs=[
            pl.BlockSpec((1, NSA_HEADS, LANES), lambda b, pt: (b, 0, 0)),
            pl.BlockSpec((1, 1, n_buf, 2 * gd), lambda b, pt: (0, b, 0, 0)),
        ],
        scratch_shapes=[pltpu.VMEM((n_slots, 2 * gd), f32)],
    )
    o, nwin = pl.pallas_call(
        functools.partial(_nsa_sample_kernel, n_pages=n_pages, page=page, past_len=past_len),
        grid_spec=grid_spec,
        out_shape=[jax.ShapeDtypeStruct((n_dec, NSA_HEADS, LANES), f32),
                   jax.ShapeDtypeStruct((1, n_dec, n_buf, 2 * gd), f32)],
        compiler_params=_cparams("parallel"),
        name="nsa_sample_attn",
    )(page_table.reshape(-1), *([cache2] * n_pages), win2, q_hg, kv_nat, gates_hg, wc_nat, bkey, bblk, emat)
    return o[:, :, :HEAD_DIM], nwin[0]


def _conv_prompt_kernel(x_ref, g_ref, win_ref, cw_ref, wout_ref, o_ref, st_ref, carry_ref, *, n_t):
    t = pl.program_id(1)
    tm, d = x_ref.shape

    @pl.when(t == 0)
    def _():
        carry_ref[...] = jnp.zeros_like(carry_ref)

    x = x_ref[...]
    xn = _rms(x, g_ref[...]).astype(bf16)
    pr = _dot(xn, win_ref[...])
    u = pr[:, d:2 * d] * pr[:, 0:d]
    bg = pr[:, 2 * d:3 * d]
    prev = carry_ref[...]
    row = lax.broadcasted_iota(i32, (tm, d), 0)
    u1 = jnp.where(row == 0, prev[7:8, :], pltpu.roll(u, 1, 0))
    u2 = jnp.where(row == 0, prev[6:7, :], jnp.where(row == 1, prev[7:8, :], pltpu.roll(u, 2, 0)))
    cw = cw_ref[...]
    v = cw[0:1, :] * u2 + cw[1:2, :] * u1 + cw[2:3, :] * u
    o_ref[...] = x + _dot((bg * v).astype(bf16), wout_ref[...])
    carry_ref[...] = u[tm - 8:tm, :]

    @pl.when(t == n_t - 1)
    def _():
        st_ref[0] = u[tm - (CONV_W - 1):tm, :]


def _conv_prompt(x, g, w_in, cw, w_out, n_batch, seq, tm_target=512):
    m, d = x.shape
    tm = _row_tile(seq, tm_target)
    n_t = seq // tm
    return pl.pallas_call(
        functools.partial(_conv_prompt_kernel, n_t=n_t),
        grid=(n_batch, n_t),
        in_specs=[pl.BlockSpec((tm, d), lambda b, t: (b * n_t + t, 0)),
                  pl.BlockSpec((1, d), lambda b, t: (0, 0)),
                  pl.BlockSpec((d, 3 * d), lambda b, t: (0, 0)),
                  pl.BlockSpec((CONV_W, d), lambda b, t: (0, 0)),
                  pl.BlockSpec((d, d), lambda b, t: (0, 0))],
        out_specs=[pl.BlockSpec((tm, d), lambda b, t: (b * n_t + t, 0)),
                   pl.BlockSpec((1, CONV_W - 1, d), lambda b, t: (b, 0, 0))],
        out_shape=[jax.ShapeDtypeStruct((m, d), f32),
                   jax.ShapeDtypeStruct((n_batch, CONV_W - 1, d), f32)],
        scratch_shapes=[pltpu.VMEM((8, d), f32)],
        compiler_params=_cparams("parallel", "arbitrary"),
        name="conv_prompt",
    )(x, g.reshape(1, d), w_in, cw, w_out)


def _conv_sample_kernel(x_ref, g_ref, st_ref, win_ref, cw_ref, wout_ref, o_ref, nst_ref):
    d = x_ref.shape[1]
    x = x_ref[...]
    xn = _rms(x, g_ref[...]).astype(bf16)
    pr = _dot(xn, win_ref[...])
    u = pr[:, d:2 * d] * pr[:, 0:d]
    bg = pr[:, 2 * d:3 * d]
    st = st_ref[...]
    cw = cw_ref[...]
    v = cw[0:1, :] * st[:, 0:d] + cw[1:2, :] * st[:, d:2 * d] + cw[2:3, :] * u
    o_ref[...] = x + _dot((bg * v).astype(bf16), wout_ref[...])
    nst_ref[:, 0:d] = st[:, d:2 * d]
    nst_ref[:, d:2 * d] = u


def _conv_sample(x, g, state, w_in, cw, w_out):
    m, d = x.shape
    o, nst = pl.pallas_call(
        _conv_sample_kernel,
        out_shape=[jax.ShapeDtypeStruct((m, d), f32),
                   jax.ShapeDtypeStruct((m, (CONV_W - 1) * d), f32)],
        compiler_params=pltpu.CompilerParams(vmem_limit_bytes=VMEM_LIMIT_BYTES),
        name="conv_sample",
    )(x, g.reshape(1, d), state.reshape(m, (CONV_W - 1) * d), w_in, cw, w_out)
    return o, nst.reshape(m, CONV_W - 1, d)


def _pool_mix(win_sum_groups, xn, cnt_groups, wgrp_ref, scale):
    gw = xn.shape[1] // len(POOL_WINDOWS)
    outs = []
    for gi in range(len(POOL_WINDOWS)):
        diff = win_sum_groups[gi] / cnt_groups[gi] - xn[:, gi * gw:(gi + 1) * gw]
        outs.append(_dot(diff.astype(bf16), wgrp_ref[gi]))
    return jnp.concatenate(outs, axis=1) * scale


def _pool_prompt_kernel(x_ref, g_ref, wgrp_ref, sc_ref, o_ref, st_ref, carry_ref, *, n_t):
    t = pl.program_id(1)
    tm, d = x_ref.shape
    gw = d // len(POOL_WINDOWS)
    hist = carry_ref.shape[0]

    @pl.when(t == 0)
    def _():
        carry_ref[...] = jnp.zeros_like(carry_ref)

    x = x_ref[...]
    xn = _rms(x, g_ref[...])
    a = jnp.concatenate([carry_ref[...], xn], axis=0)
    sums = []
    s = a
    shift = 1
    for gi, w in enumerate(POOL_WINDOWS):
        while shift < w:
            s = s + pltpu.roll(s, shift, 0)
            shift *= 2
        sums.append(s[hist:, gi * gw:(gi + 1) * gw])
    pos = t * tm + lax.broadcasted_iota(i32, (tm, gw), 0)
    cnts = [jnp.minimum(pos + 1, w).astype(f32) for w in POOL_WINDOWS]
    o_ref[...] = x + _pool_mix(sums, xn, cnts, wgrp_ref, sc_ref[...])
    carry_ref[...] = xn[tm - hist:tm, :]

    @pl.when(t == n_t - 1)
    def _():
        st_ref[0] = xn[tm - POOL_HIST:tm, :]


def _pool_prompt(x, g, w_grp, scale, n_batch, seq, tm_target=512):
    m, d = x.shape
    tm = _row_tile(seq, tm_target)
    n_t = seq // tm
    return pl.pallas_call(
        functools.partial(_pool_prompt_kernel, n_t=n_t),
        grid=(n_batch, n_t),
        in_specs=[pl.BlockSpec((tm, d), lambda b, t: (b * n_t + t, 0)),
                  pl.BlockSpec((1, d), lambda b, t: (0, 0)),
                  pl.BlockSpec(w_grp.shape, lambda b, t: (0, 0, 0)),
                  pl.BlockSpec((1, d), lambda b, t: (0, 0))],
        out_specs=[pl.BlockSpec((tm, d), lambda b, t: (b * n_t + t, 0)),
                   pl.BlockSpec((1, POOL_HIST, d), lambda b, t: (b, 0, 0))],
        out_shape=[jax.ShapeDtypeStruct((m, d), f32),
                   jax.ShapeDtypeStruct((n_batch, POOL_HIST, d), f32)],
        scratch_shapes=[pltpu.VMEM((POOL_HIST + 1, d), f32)],
        compiler_params=_cparams("parallel", "arbitrary"),
        name="pool_prompt",
    )(x, g.reshape(1, d), w_grp, scale.reshape(1, d))


def _pool_sample_kernel(x_ref, g_ref, st_ref, wgrp_ref, sc_ref, o_ref, nst_ref, *, pos):
    m, d = x_ref.shape
    gw = d // len(POOL_WINDOWS)
    x = x_ref[...]
    xn = _rms(x, g_ref[...])
    st = st_ref[...]
    sums, cnts = [], []
    s = xn
    back = 1
    for gi, w in enumerate(POOL_WINDOWS):
        while back < w:
            s = s + st[:, (POOL_HIST - back) * d:(POOL_HIST - back + 1) * d]
            back += 1
        sums.append(s[:, gi * gw:(gi + 1) * gw])
        cnts.append(jnp.full((m, gw), float(min(pos + 1, w)), f32))
    o_ref[...] = x + _pool_mix(sums, xn, cnts, wgrp_ref, sc_ref[...])
    nst_ref[:, 0:(POOL_HIST - 1) * d] = st[:, d:POOL_HIST * d]
    nst_ref[:, (POOL_HIST - 1) * d:POOL_HIST * d] = xn


def _pool_sample(x, g, state, w_grp, scale, pos):
    m, d = x.shape
    o, nst = pl.pallas_call(
        functools.partial(_pool_sample_kernel, pos=pos),
        out_shape=[jax.ShapeDtypeStruct((m, d), f32),
                   jax.ShapeDtypeStruct((m, POOL_HIST * d), f32)],
        compiler_params=pltpu.CompilerParams(vmem_limit_bytes=VMEM_LIMIT_BYTES),
        name="pool_sample",
    )(x, g.reshape(1, d), state.reshape(m, POOL_HIST * d), w_grp, scale.reshape(1, d))
    return o, nst.reshape(m, POOL_HIST, d)


def _ca_prompt_kernel(x_ref, g_ref, wq_ref, kv_ref, wo_ref, o_ref):
    d = x_ref.shape[1]
    hd = d // CA_HEADS
    x = x_ref[...]
    xn = _rms(x, g_ref[...]).astype(bf16)
    q = (_dot(xn, wq_ref[...]) * (hd ** -0.5)).astype(bf16)
    outs = []
    for h in range(CA_HEADS):
        k = kv_ref[:, h * hd:(h + 1) * hd].astype(bf16)
        v = kv_ref[:, d + h * hd:d + (h + 1) * hd].astype(bf16)
        s = _dot_nt(q[:, h * hd:(h + 1) * hd], k)
        e = jnp.exp(s - jnp.max(s, axis=1, keepdims=True))
        p = e / jnp.sum(e, axis=1, keepdims=True)
        outs.append(_dot(p.astype(bf16), v))
    o = jnp.concatenate(outs, axis=1).astype(bf16)
    o_ref[...] = x + _dot(o, wo_ref[...])


def _ca_prompt(x, g, wq, mkv, wo, n_batch, seq, mem_len, tm_target=512):
    m, d = x.shape
    tm = _row_tile(seq, tm_target)
    n_t = seq // tm
    return pl.pallas_call(
        _ca_prompt_kernel,
        grid=(n_batch, n_t),
        in_specs=[pl.BlockSpec((tm, d), lambda b, t: (b * n_t + t, 0)),
                  pl.BlockSpec((1, d), lambda b, t: (0, 0)),
                  pl.BlockSpec((d, d), lambda b, t: (0, 0)),
                  pl.BlockSpec((mem_len, 2 * d), lambda b, t: (b, 0)),
                  pl.BlockSpec((d, d), lambda b, t: (0, 0))],
        out_specs=pl.BlockSpec((tm, d), lambda b, t: (b * n_t + t, 0)),
        out_shape=jax.ShapeDtypeStruct((m, d), f32),
        compiler_params=_cparams("parallel", "parallel"),
        name="ca_prompt",
    )(x, g.reshape(1, d), wq, mkv, wo)


def _ca_sample_kernel(q_ref, kv_ref, o_ref):
    bb, _, hd = q_ref.shape
    scale = hd ** -0.5
    for bi in range(bb):
        q = q_ref[bi] * scale
        k = kv_ref[0, bi, :, 0]
        v = kv_ref[0, bi, :, 1]
        s = jnp.sum(k * q[None], axis=2, keepdims=True)
        e = jnp.exp(s - jnp.max(s, axis=0, keepdims=True))
        p = e / jnp.sum(e, axis=0, keepdims=True)
        o_ref[bi] = jnp.sum(p * v, axis=0)


def _ca_sample(q, cache_mem_kv, layer, bb=4):
    n_dec, d = q.shape
    _, _, mem, _, heads, hd = cache_mem_kv.shape
    o = pl.pallas_call(
        _ca_sample_kernel,
        grid=(n_dec // bb,),
        in_specs=[pl.BlockSpec((bb, heads, hd), lambda i: (i, 0, 0)),
                  pl.BlockSpec((1, bb, mem, 2, heads, hd), lambda i: (layer, i, 0, 0, 0, 0))],
        out_specs=pl.BlockSpec((bb, heads, hd), lambda i: (i, 0, 0)),
        out_shape=jax.ShapeDtypeStruct((n_dec, heads, hd), f32),
        compiler_params=_cparams("parallel"),
        name="ca_sample",
    )(q.reshape(n_dec, heads, hd), cache_mem_kv)
    return o.reshape(n_dec, d)


def _nsa_weights(w_in, w_cmp):
    d_model = w_in.shape[0]
    nq = NSA_HEADS * HEAD_DIM
    nkv = 6 * NSA_KV_HEADS * HEAD_DIM
    wq = w_in[:, :nq].astype(bf16)
    wkv = w_in[:, nq:nq + nkv].reshape(d_model, 3, 2, NSA_KV_HEADS, HEAD_DIM)
    wkv = wkv.transpose(0, 1, 3, 2, 4).reshape(d_model, nkv).astype(bf16)
    wg = w_in[:, nq + nkv:].reshape(d_model, NSA_KV_HEADS, NSA_HPG * 3)
    wg = jnp.pad(wg, ((0, 0), (0, 0), (0, LANES - NSA_HPG * 3))).reshape(d_model, NSA_KV_HEADS * LANES)
    w_pack = jnp.concatenate([w_cmp[0], w_cmp[1]], axis=1)
    wc_nat = jnp.concatenate([jnp.tile(w_cmp[0], (1, NSA_KV_HEADS)),
                              jnp.tile(w_cmp[1], (1, NSA_KV_HEADS))], axis=1)
    return wq, wkv, wg.astype(bf16), w_pack, wc_nat


def _kv_natural(kvp):
    m = kvp.shape[1]
    x = kvp.reshape(3, NSA_KV_HEADS, m, 2, HEAD_DIM)
    return x.transpose(2, 0, 3, 1, 4).reshape(m, 6 * NSA_KV_HEADS * HEAD_DIM)


def kernel(x_prompt, x_sample, cache_nsa_kv, cache_nsa_win, state_conv, state_pool, cache_mem_kv,
           page_table, mem_prompt, rel_bias, norm_g, norm_mem_g, final_g, w_ffn_in, w_ffn_out,
           w_nsa_in, w_nsa_cmp, w_nsa_out, w_conv_in, conv_w, w_conv_out, w_pool, pool_scale,
           w_ca_q, w_ca_kv, w_ca_out):
    n_b, seq, d = x_prompt.shape
    n_dec = x_sample.shape[0]
    depth = norm_g.shape[0]
    mem_len = mem_prompt.shape[1]
    page = cache_nsa_kv.shape[2]
    past_len = page_table.shape[1] * page
    g_, hd = NSA_KV_HEADS, HEAD_DIM
    gd = g_ * hd

    xp = x_prompt.reshape(n_b * seq, d)
    xs = x_sample.reshape(n_dec, d)
    mem = mem_prompt.reshape(n_b * mem_len, d)

    tsel, twin, cmpq, cmpt = _t5_prompt_tables(rel_bias, seq // NSA_BLOCK)
    n_blk_pad = -(-(past_len // NSA_BLOCK + 8) // LANES) * LANES
    bkey, bblk = _t5_sample_tables(rel_bias, past_len, past_len + LANES, n_blk_pad)

    nsa_kv_p, nsa_win_p, conv_p, pool_p, mem_p = [], [], [], [], []
    nsa_kv_s, nsa_win_s, conv_s, pool_s = [], [], [], []
    for i in range(depth):
        kind, j = i % 3, i // 3
        wf_in = w_ffn_in[i].astype(bf16)
        wf_out = w_ffn_out[i].astype(bf16)
        xp = _ffn(xp, norm_g[i, 0], wf_in[0], wf_out[0])
        xs = _ffn(xs, norm_g[i, 0], wf_in[0], wf_out[0])
        if kind == 0:
            wq, wkv, wg, w_pack, wc_nat = _nsa_weights(w_nsa_in[j], w_nsa_cmp[j])
            w_out = w_nsa_out[j].astype(bf16)
            q, kvp, gates, *slabs = _nsa_proj(xp, norm_g[i, 1], wq, wkv, wg, n_b, slab_pad=KV_PAD)
            kvc = _compress(kvp, w_pack, g_)
            o = _nsa_prompt_attn(q, gates, kvc, slabs, tsel, twin, cmpq, cmpt, n_b, seq)
            xp = _proj_res(o, w_out, xp)
            kv_nat = _kv_natural(kvp)
            nsa_kv_p.append(kv_nat[:, :4 * gd].reshape(n_b * seq // page, page, 4, g_, hd))
            n_keep = min(NSA_WINDOW, seq)
            nsa_win_p.append(kv_nat.reshape(n_b, seq, 6 * gd)[:, seq - n_keep:, 4 * gd:]
                             .reshape(n_b, n_keep, 2, g_, hd))
            q_s, kvp_s, gates_s = _nsa_proj(xs, norm_g[i, 1], wq, wkv, wg, 1)
            kv_nat_s = _kv_natural(kvp_s)
            q_hg = q_s.reshape(n_dec, g_, NSA_HPG, hd).transpose(0, 2, 1, 3).reshape(n_dec, NSA_HEADS, hd)
            q_hg = jnp.tile(q_hg, (1, 1, g_))
            gates_hg = gates_s.transpose(1, 0, 2)[:, :, :NSA_HPG * 3].reshape(n_dec, g_, NSA_HPG, 3)
            gates_hg = jnp.pad(gates_hg.transpose(0, 2, 1, 3).reshape(n_dec, NSA_HEADS, 3),
                               ((0, 0), (0, 0), (0, LANES - 3)))
            o_s, nwin = _nsa_sample_attn(page_table, cache_nsa_kv, cache_nsa_win, j, q_hg, kv_nat_s,
                                         gates_hg, wc_nat, bkey, bblk, past_len)
            o_s = o_s.reshape(n_dec, NSA_HPG, g_, hd).transpose(0, 2, 1, 3).reshape(n_dec, NSA_HEADS * hd)
            xs = _proj_res(o_s, w_out, xs)
            nsa_kv_s.append(kv_nat_s[:, :4 * gd].reshape(n_dec, 1, 4, g_, hd))
            nsa_win_s.append(nwin.reshape(n_dec, -1, 2, g_, hd))
        elif kind == 1:
            wc_in = w_conv_in[j].astype(bf16)
            wc_out = w_conv_out[j].astype(bf16)
            xp, cst = _conv_prompt(xp, norm_g[i, 1], wc_in, conv_w[j], wc_out, n_b, seq)
            xs, csts = _conv_sample(xs, norm_g[i, 1], state_conv[j], wc_in, conv_w[j], wc_out)
            conv_p.append(cst)
            conv_s.append(csts)
        else:
            wp = w_pool[j].astype(bf16)
            xp, pst = _pool_prompt(xp, norm_g[i, 1], wp, pool_scale[j], n_b, seq)
            xs, psts = _pool_sample(xs, norm_g[i, 1], state_pool[j], wp, pool_scale[j], past_len)
            pool_p.append(pst)
            pool_s.append(psts)
        w_q = w_ca_q[i].astype(bf16)
        w_o = w_ca_out[i].astype(bf16)
        mkv = _normproj(mem, norm_mem_g[i], w_ca_kv[i].astype(bf16))
        mem_p.append(mkv.reshape(n_b, mem_len, 2, CA_HEADS, d // CA_HEADS))
        xp = _ca_prompt(xp, norm_g[i, 2], w_q, mkv, w_o, n_b, seq, mem_len)
        q_ca = _normproj(xs, norm_g[i, 2], w_q)
        o_ca = _ca_sample(q_ca, cache_mem_kv, i)
        xs = _proj_res(o_ca, w_o, xs)
        fg = final_g if i == depth - 1 else None
        xp = _ffn(xp, norm_g[i, 3], wf_in[1], wf_out[1], final_g=fg)
        xs = _ffn(xs, norm_g[i, 3], wf_in[1], wf_out[1], final_g=fg)

    return (xp.reshape(n_b, seq, d), xs.reshape(n_dec, 1, d),
            jnp.stack(nsa_kv_p), jnp.stack(nsa_win_p), jnp.stack(conv_p), jnp.stack(pool_p),
            jnp.stack(mem_p),
            jnp.stack(nsa_kv_s), jnp.stack(nsa_win_s), jnp.stack(conv_s), jnp.stack(pool_s))
```

```python
import functools
import math

import jax
import jax.numpy as jnp
from jax import lax
from jax.experimental import pallas as pl
from jax.experimental.pallas import tpu as pltpu

f32 = jnp.float32
bf16 = jnp.bfloat16
i32 = jnp.int32

NORM_EPS = 1e-6
NEG_INF = -1e30

NSA_HEADS = 16
NSA_KV_HEADS = 4
NSA_HPG = NSA_HEADS // NSA_KV_HEADS
HEAD_DIM = 64
NSA_BLOCK = 64
NSA_TOPK = 16
NSA_WINDOW = 512
REL_BUCKETS = 32
REL_MAX_DIST = 128
CONV_W = 3
POOL_WINDOWS = (2, 4, 8, 16)
POOL_HIST = max(POOL_WINDOWS) - 1
CA_HEADS = 4

LANES = 128
VMEM_LIMIT_BYTES = 56 * 1024 * 1024

N_NEAR = 3
SEL_CHUNK_BLOCKS = 16
SEL_CHUNK = SEL_CHUNK_BLOCKS * NSA_BLOCK
NEAR_BLOCKS = 4
NEAR_TILE = NEAR_BLOCKS * NSA_BLOCK
WIN_TILE = 640
KV_PAD = 1024


def _cparams(*sem):
    return pltpu.CompilerParams(dimension_semantics=sem, vmem_limit_bytes=VMEM_LIMIT_BYTES)


def _rms(x, g):
    return x * lax.rsqrt(jnp.mean(x * x, axis=-1, keepdims=True) + NORM_EPS) * g


def _dot(a, b):
    return jnp.dot(a, b, preferred_element_type=f32)


def _dot_nt(a, b):
    return lax.dot_general(a, b, (((1,), (1,)), ((), ())), preferred_element_type=f32)


def _row_tile(m, target):
    t = min(m, target)
    while m % t:
        t //= 2
    return t


def _ffn_kernel(x_ref, g_ref, wg_ref, wu_ref, wo_ref, *rest, n_f, final_norm):
    if final_norm:
        gf_ref, o_ref, xn_ref, acc_ref = rest
    else:
        o_ref, xn_ref, acc_ref = rest
    j = pl.program_id(1)

    @pl.when(j == 0)
    def _():
        xn_ref[...] = _rms(x_ref[...], g_ref[...]).astype(bf16)
        acc_ref[...] = jnp.zeros_like(acc_ref)

    xn = xn_ref[...]
    gate = _dot(xn, wg_ref[...])
    up = _dot(xn, wu_ref[...])
    act = (gate * jax.nn.sigmoid(gate) * up).astype(bf16)
    acc_ref[...] += _dot(act, wo_ref[...])

    @pl.when(j == n_f - 1)
    def _():
        y = x_ref[...] + 0.5 * acc_ref[...]
        if final_norm:
            y = _rms(y, gf_ref[...])
        o_ref[...] = y


def _ffn(x, g, w_in, w_out, final_g=None, tm_target=512):
    m, d = x.shape
    f = w_out.shape[0]
    tm = _row_tile(m, tm_target)
    tf = f
    for cand in (1408, 1024, 512, 256, 128):
        if f % cand == 0:
            tf = cand
            break
    n_f = f // tf
    in_specs = [
        pl.BlockSpec((tm, d), lambda i, j: (i, 0)),
        pl.BlockSpec((1, d), lambda i, j: (0, 0)),
        pl.BlockSpec((d, tf), lambda i, j: (0, j)),
        pl.BlockSpec((d, tf), lambda i, j: (0, n_f + j)),
        pl.BlockSpec((tf, d), lambda i, j: (j, 0)),
    ]
    args = [x, g.reshape(1, d), w_in, w_in, w_out]
    if final_g is not None:
        in_specs.append(pl.BlockSpec((1, d), lambda i, j: (0, 0)))
        args.append(final_g.reshape(1, d))
    return pl.pallas_call(
        functools.partial(_ffn_kernel, n_f=n_f, final_norm=final_g is not None),
        grid=(m // tm, n_f),
        in_specs=in_specs,
        out_specs=pl.BlockSpec((tm, d), lambda i, j: (i, 0)),
        out_shape=jax.ShapeDtypeStruct((m, d), f32),
        scratch_shapes=[pltpu.VMEM((tm, d), bf16), pltpu.VMEM((tm, d), f32)],
        compiler_params=_cparams("parallel", "arbitrary"),
        name="ffn",
    )(*args)


def _normproj_kernel(x_ref, g_ref, w_ref, o_ref):
    xn = _rms(x_ref[...], g_ref[...]).astype(bf16)
    o_ref[...] = _dot(xn, w_ref[...])


def _normproj(x, g, w, tm_target=512):
    m, d = x.shape
    n = w.shape[1]
    tm = _row_tile(m, tm_target)
    return pl.pallas_call(
        _normproj_kernel,
        grid=(m // tm,),
        in_specs=[pl.BlockSpec((tm, d), lambda i: (i, 0)),
                  pl.BlockSpec((1, d), lambda i: (0, 0)),
                  pl.BlockSpec((d, n), lambda i: (0, 0))],
        out_specs=pl.BlockSpec((tm, n), lambda i: (i, 0)),
        out_shape=jax.ShapeDtypeStruct((m, n), f32),
        compiler_params=_cparams("parallel"),
        name="normproj",
    )(x, g.reshape(1, d), w)


def _proj_res_kernel(a_ref, w_ref, x_ref, o_ref):
    o_ref[...] = x_ref[...] + _dot(a_ref[...].astype(bf16), w_ref[...])


def _proj_res(a, w, x, tm_target=512):
    m, k = a.shape
    d = w.shape[1]
    tm = _row_tile(m, tm_target)
    return pl.pallas_call(
        _proj_res_kernel,
        grid=(m // tm,),
        in_specs=[pl.BlockSpec((tm, k), lambda i: (i, 0)),
                  pl.BlockSpec((k, d), lambda i: (0, 0)),
                  pl.BlockSpec((tm, d), lambda i: (i, 0))],
        out_specs=pl.BlockSpec((tm, d), lambda i: (i, 0)),
        out_shape=jax.ShapeDtypeStruct((m, d), f32),
        compiler_params=_cparams("parallel"),
        name="proj_res",
    )(a, w, x)


def _nsa_proj_kernel(x_ref, g_ref, wq_ref, wkv_ref, wg_ref, *rest, n_pad, with_slabs):
    if with_slabs:
        wp_ref, q_ref, kv_ref, gt_ref, kvc_ref, *slab_refs = rest
    else:
        q_ref, kv_ref, gt_ref = rest
        slab_refs = ()
    t = pl.program_id(1) - n_pad

    @pl.when(t >= 0)
    def _():
        xn = _rms(x_ref[...], g_ref[...]).astype(bf16)
        q_ref[...] = _dot(xn, wq_ref[...])
        kv = _dot(xn, wkv_ref[...])
        n_g = kv_ref.shape[0] // 3
        tm = kv.shape[0]
        for s in range(kv_ref.shape[0]):
            kv_ref[s] = kv[:, s * LANES:(s + 1) * LANES]
        gt = jax.nn.sigmoid(_dot(xn, wg_ref[...]))
        for s in range(gt_ref.shape[0]):
            gt_ref[s] = gt[:, s * LANES:(s + 1) * LANES]
        if slab_refs:
            ks_ref, vs_ref, kw_ref, vw_ref = slab_refs
            lane = lax.broadcasted_iota(i32, (tm, LANES), 1)
            lower = lane < HEAD_DIM
            blk = (t * tm + lax.broadcasted_iota(i32, (tm, LANES), 0)) // NSA_BLOCK
            onehot = jnp.where(lane - HEAD_DIM == blk, 1.0, 0.0)
            w_blk = wp_ref[...][None]
            for gi in range(n_g):
                cmp = kv[:, gi * LANES:(gi + 1) * LANES].reshape(tm // NSA_BLOCK, NSA_BLOCK, LANES)
                kvc_ref[gi] = jnp.sum(cmp * w_blk, axis=1)
                sel = kv[:, (n_g + gi) * LANES:(n_g + gi + 1) * LANES]
                win = kv[:, (2 * n_g + gi) * LANES:(2 * n_g + gi + 1) * LANES]
                ks_ref[gi, 0] = jnp.where(lower, sel, onehot).astype(bf16)
                vs_ref[gi, 0] = jnp.where(lower, 1.0, sel).astype(bf16)
                kw_ref[gi, 0] = jnp.where(lower, win, 0.0).astype(bf16)
                vw_ref[gi, 0] = jnp.where(lower, 1.0, win).astype(bf16)

    if slab_refs:
        @pl.when(t < 0)
        def _():
            for ref in slab_refs:
                ref[...] = jnp.zeros(ref.shape, ref.dtype)


def _nsa_proj(x, g, wq, wkv, wg, n_batch, slab_pad=None, w_pack=None, tm_target=512):
    m, d = x.shape
    seq = m // n_batch
    nq, nkv, ng = wq.shape[1], wkv.shape[1], wg.shape[1]
    n_slab, n_gs = nkv // LANES, ng // LANES
    tm = _row_tile(seq, tm_target)
    n_t = seq // tm
    n_pad = 0 if slab_pad is None else slab_pad // tm

    def row(b, t):
        return b * n_t + jnp.maximum(t - n_pad, 0)

    out_specs = [pl.BlockSpec((tm, nq), lambda b, t: (row(b, t), 0)),
                 pl.BlockSpec((n_slab, tm, LANES), lambda b, t: (0, row(b, t), 0)),
                 pl.BlockSpec((n_gs, tm, LANES), lambda b, t: (0, row(b, t), 0))]
    out_shape = [jax.ShapeDtypeStruct((m, nq), f32),
                 jax.ShapeDtypeStruct((n_slab, m, LANES), f32),
                 jax.ShapeDtypeStruct((n_gs, m, LANES), f32)]
    in_specs = [pl.BlockSpec((tm, d), lambda b, t: (row(b, t), 0)),
                pl.BlockSpec((1, d), lambda b, t: (0, 0)),
                pl.BlockSpec((d, nq), lambda b, t: (0, 0)),
                pl.BlockSpec((d, nkv), lambda b, t: (0, 0)),
                pl.BlockSpec((d, ng), lambda b, t: (0, 0))]
    args = [x, g.reshape(1, d), wq, wkv, wg]
    if slab_pad is not None:
        assert slab_pad % tm == 0
        n_g = n_slab // 3
        in_specs.append(pl.BlockSpec(w_pack.shape, lambda b, t: (0, 0)))
        args.append(w_pack)
        out_specs.append(pl.BlockSpec((n_g, tm // NSA_BLOCK, LANES), lambda b, t: (0, row(b, t), 0)))
        out_shape.append(jax.ShapeDtypeStruct((n_g, m // NSA_BLOCK, LANES), f32))
        out_specs += [pl.BlockSpec((n_g, 1, tm, LANES), lambda b, t: (0, b, t, 0))] * 4
        out_shape += [jax.ShapeDtypeStruct((n_g, n_batch, slab_pad + seq, LANES), bf16)] * 4
    return pl.pallas_call(
        functools.partial(_nsa_proj_kernel, n_pad=n_pad, with_slabs=slab_pad is not None),
        grid=(n_batch, n_t + n_pad),
        in_specs=in_specs,
        out_specs=out_specs,
        out_shape=out_shape,
        compiler_params=_cparams("parallel", "arbitrary"),
        name="nsa_proj",
    )(*args)


def _t5_bucket(rel):
    n = jnp.maximum(rel, 0)
    max_exact = REL_BUCKETS // 2
    nf = jnp.maximum(n, 1).astype(f32)
    large = max_exact + (jnp.log(nf / max_exact) / math.log(REL_MAX_DIST / max_exact)
                         * (REL_BUCKETS - max_exact)).astype(i32)
    large = jnp.minimum(large, REL_BUCKETS - 1)
    return jnp.where(n < max_exact, n, large)


def _bias_lookup(table_ref, bucket, head):
    out = jnp.zeros(bucket.shape, f32)
    for k in range(REL_BUCKETS):
        out = jnp.where(bucket == k, table_ref[k, head], out)
    return out


def _t5_prompt_kernel(table_ref, tsel_ref, twin_ref, cmpq_ref, cmpt_ref):
    g = pl.program_id(0)
    n_bc = cmpq_ref.shape[3]
    rel_s = (lax.broadcasted_iota(i32, (NSA_BLOCK, NEAR_TILE), 0) + (NEAR_TILE - NSA_BLOCK)
             - lax.broadcasted_iota(i32, (NSA_BLOCK, NEAR_TILE), 1))
    rel_w = (lax.broadcasted_iota(i32, (NSA_BLOCK, WIN_TILE), 0) + (WIN_TILE - NSA_BLOCK)
             - lax.broadcasted_iota(i32, (NSA_BLOCK, WIN_TILE), 1))
    bucket_s = _t5_bucket(rel_s)
    bucket_w = _t5_bucket(rel_w)
    qi_b = lax.broadcasted_iota(i32, (NSA_BLOCK, n_bc), 0)
    shp_t = (8, NSA_BLOCK)
    bucket_t = _t5_bucket(lax.broadcasted_iota(i32, shp_t, 1) - (NSA_BLOCK - 1)
                          + NSA_BLOCK * lax.broadcasted_iota(i32, shp_t, 0))
    for hh in range(NSA_HPG):
        head = g * NSA_HPG + hh
        far = table_ref[REL_BUCKETS - 1, head]
        rows = slice(hh * NSA_BLOCK, (hh + 1) * NSA_BLOCK)
        tsel_ref[0, rows, :] = jnp.where(rel_s >= 0, _bias_lookup(table_ref, bucket_s, head) - far, NEG_INF)
        twin_ref[0, rows, :] = jnp.where((rel_w >= 0) & (rel_w <= NSA_WINDOW),
                                         _bias_lookup(table_ref, bucket_w, head) - far, NEG_INF)
        for d in range(N_NEAR):
            cmpq_ref[0, d, rows, :] = _bias_lookup(
                table_ref, _t5_bucket(qi_b - (NSA_BLOCK - 1) + NSA_BLOCK * d), head) - far
        cmpt_ref[0, :, rows] = _bias_lookup(table_ref, bucket_t, head) - far


def _t5_prompt_tables(rel_bias, n_blk):
    rows = NSA_HPG * NSA_BLOCK
    return pl.pallas_call(
        _t5_prompt_kernel,
        grid=(NSA_KV_HEADS,),
        in_specs=[pl.BlockSpec(memory_space=pltpu.SMEM)],
        out_specs=[pl.BlockSpec((1, rows, NEAR_TILE), lambda g: (g, 0, 0)),
                   pl.BlockSpec((1, rows, WIN_TILE), lambda g: (g, 0, 0)),
                   pl.BlockSpec((1, N_NEAR, rows, n_blk), lambda g: (g, 0, 0, 0)),
                   pl.BlockSpec((1, 8, rows), lambda g: (g, 0, 0))],
        out_shape=[jax.ShapeDtypeStruct((NSA_KV_HEADS, rows, NEAR_TILE), f32),
                   jax.ShapeDtypeStruct((NSA_KV_HEADS, rows, WIN_TILE), f32),
                   jax.ShapeDtypeStruct((NSA_KV_HEADS, N_NEAR, rows, n_blk), f32),
                   jax.ShapeDtypeStruct((NSA_KV_HEADS, 8, rows), f32)],
        compiler_params=_cparams("parallel"),
        name="t5_prompt_tables",
    )(rel_bias)


def _t5_sample_kernel(table_ref, key_ref, blk_ref, *, qpos):
    kpos = lax.broadcasted_iota(i32, (1, key_ref.shape[1]), 1)
    bucket_k = _t5_bucket(qpos - kpos)
    blk = lax.broadcasted_iota(i32, (1, blk_ref.shape[1]), 1)
    bucket_b = _t5_bucket(qpos - (blk * NSA_BLOCK + NSA_BLOCK - 1))
    for r in range(NSA_HEADS):
        head = (r % NSA_KV_HEADS) * NSA_HPG + r // NSA_KV_HEADS
        key_ref[r:r + 1, :] = _bias_lookup(table_ref, bucket_k, head)
        blk_ref[r:r + 1, :] = _bias_lookup(table_ref, bucket_b, head)


def _t5_sample_tables(rel_bias, qpos, n_key, n_blk):
    return pl.pallas_call(
        functools.partial(_t5_sample_kernel, qpos=qpos),
        in_specs=[pl.BlockSpec(memory_space=pltpu.SMEM)],
        out_shape=[jax.ShapeDtypeStruct((NSA_HEADS, n_key), f32),
                   jax.ShapeDtypeStruct((NSA_HEADS, n_blk), f32)],
        name="t5_sample_tables",
    )(rel_bias)


def _flash_update(carry, s, v_aug):
    m, acc = carry
    m_new = jnp.maximum(m, jnp.max(s, axis=1, keepdims=True))
    p = jnp.exp(s - m_new).astype(bf16)
    return m_new, jnp.exp(m - m_new) * acc + _dot(p, v_aug)


def _flash_first(s, v_aug):
    m = jnp.max(s, axis=1, keepdims=True)
    return m, _dot(jnp.exp(s - m).astype(bf16), v_aug)


def _flash_finish(carry):
    _, acc = carry
    return acc / pltpu.roll(acc, HEAD_DIM, 1)


def _tile_rows(x, n):
    return jnp.concatenate([x] * n, axis=0)


def _select_blocks(imp_t, c, n_blk):
    blk = lax.broadcasted_iota(i32, imp_t.shape, 0)
    valid = blk <= c
    forced = (blk == 0) | (blk == c) | (blk == c - 1)

    blk_f = blk.astype(f32)
    work = jnp.where(valid & jnp.logical_not(forced), imp_t, -1.0)
    sel = forced.astype(f32)
    for _ in range(NSA_TOPK - 3):
        mx = jnp.max(work, axis=0, keepdims=True)
        first = jnp.min(jnp.where(work == mx, blk_f, float(n_blk)), axis=0, keepdims=True)
        pick = blk_f == first
        work = jnp.where(pick, -2.0, work)
        sel = jnp.where(pick, 1.0, sel)
    return jnp.where(c + 1 <= NSA_TOPK, valid.astype(f32), sel)


def _nsa_prompt_kernel(*refs, n_blk, gps):
    q_ref, gt_ref, kvc_ref = refs[0:3]
    slab_refs = [refs[3 + 4 * i:3 + 4 * (i + 1)] for i in range(gps)]
    tsel_ref, twin_ref, cmpq_ref, cmpt_ref, o_ref = refs[3 + 4 * gps:]
    c = pl.program_id(2)
    rows = NSA_HPG * NSA_BLOCK
    wq = NSA_HPG * HEAD_DIM
    pre = _nsa_prompt_cmp(c, q_ref, kvc_ref, cmpq_ref, cmpt_ref, gps, n_blk)
    sel_all = _select_blocks(jnp.concatenate([p[3] for p in pre], axis=1), c, n_blk).astype(bf16)
    fronts = [_nsa_prompt_front(c, pre[i], sel_all[:, i * LANES:(i + 1) * LANES], n_blk) for i in range(gps)]
    o_win = _nsa_prompt_window(c, [p[1] for p in pre], slab_refs, twin_ref)

    def far_body(jj, carry):
        m, accs = carry
        start = pl.multiple_of(KV_PAD + jj * SEL_CHUNK, SEL_CHUNK // 8)
        s = jnp.concatenate([_dot_nt(f[1], refs[0][0, 0, pl.ds(start, SEL_CHUNK), :])
                             for f, refs in zip(fronts, slab_refs)], axis=0)
        m_new = jnp.maximum(m, jnp.max(s, axis=1, keepdims=True))
        p = jnp.exp(s - m_new).astype(bf16)
        alpha = jnp.exp(m - m_new)
        accs = tuple(alpha[i * rows:(i + 1) * rows] * accs[i]
                     + _dot(p[i * rows:(i + 1) * rows], slab_refs[i][1][0, 0, pl.ds(start, SEL_CHUNK), :])
                     for i in range(gps))
        return m_new, accs

    n_far = jnp.maximum(c - (NEAR_BLOCKS - 1), 0)
    init = (jnp.full((gps * rows, 1), NEG_INF, f32), tuple(jnp.zeros((rows, LANES), f32) for _ in range(gps)))
    m_far, acc_far = lax.fori_loop(0, (n_far + SEL_CHUNK_BLOCKS - 1) // SEL_CHUNK_BLOCKS, far_body, init)
    carries = [(m_far[i * rows:(i + 1) * rows], acc_far[i]) for i in range(gps)]

    lower = lax.broadcasted_iota(i32, (NSA_BLOCK, LANES), 1) < HEAD_DIM
    lane_g = lax.broadcasted_iota(i32, (NSA_BLOCK, LANES), 1)
    for i in range(gps):
        q_pad, _, near_hidden, o_c = fronts[i]
        o_w = o_win[i]
        ks_ref, vs_ref = slab_refs[i][0:2]
        start = pl.multiple_of((c + 1) * NSA_BLOCK + (KV_PAD - NEAR_TILE), NSA_BLOCK)
        s_near = (_dot_nt(q_pad, ks_ref[0, 0, pl.ds(start, NEAR_TILE), :]) + tsel_ref[i]
                  + _tile_rows(near_hidden, NSA_HPG))
        o_s = _flash_finish(_flash_update(carries[i], s_near, vs_ref[0, 0, pl.ds(start, NEAR_TILE), :]))
        gt = gt_ref[i]

        def gate(h, branch):
            return jnp.sum(jnp.where(lane_g == h * 3 + branch, gt, 0.0), axis=1, keepdims=True)

        heads = []
        for h in range(NSA_HPG):
            r = slice(h * NSA_BLOCK, (h + 1) * NSA_BLOCK)
            heads.append(gate(h, 0) * o_c[r] + gate(h, 1) * o_s[r] + gate(h, 2) * o_w[r])
        for pr in range(NSA_HPG // 2):
            o_ref[:, i * wq + pr * LANES:i * wq + (pr + 1) * LANES] = jnp.where(
                lower, pltpu.roll(heads[2 * pr], HEAD_DIM, 1), heads[2 * pr + 1])


def _nsa_prompt_cmp(c, q_ref, kvc_ref, cmpq_ref, cmpt_ref, gps, n_blk):
    rows = NSA_HPG * NSA_BLOCK
    wq = NSA_HPG * HEAD_DIM
    scale = HEAD_DIM ** -0.5
    lower = lax.broadcasted_iota(i32, (NSA_BLOCK, LANES), 1) < HEAD_DIM
    blk = lax.broadcasted_iota(i32, (rows, n_blk), 1)
    blk_t = lax.broadcasted_iota(i32, (n_blk, rows), 0)

    q_parts, q_pads, kvcs, sc_raw, st_raw = [], [], [], [], []
    for i in range(gps):
        parts = []
        for h in range(NSA_HPG):
            x = q_ref[:, i * wq + (h // 2) * LANES:i * wq + (h // 2 + 1) * LANES] * scale
            if h % 2:
                x = pltpu.roll(x, HEAD_DIM, 1)
            parts.append(jnp.where(lower, x, 0.0))
        q_pad = jnp.concatenate(parts, axis=0).astype(bf16)
        kvc = kvc_ref[i].astype(bf16)
        bias_c = jnp.where(blk == c, cmpq_ref[i, 0],
                           jnp.where(blk == c - 1, cmpq_ref[i, 1],
                                     jnp.where(blk == c - 2, cmpq_ref[i, 2], 0.0)))
        cols = cmpt_ref[i]
        bias_t = jnp.where(blk_t == c, cols[0:1, :],
                           jnp.where(blk_t == c - 1, cols[1:2, :],
                                     jnp.where(blk_t == c - 2, cols[2:3, :], 0.0)))
        q_parts.append(parts)
        q_pads.append(q_pad)
        kvcs.append(kvc)
        sc_raw.append(_dot_nt(q_pad, kvc) + bias_c)
        st_raw.append(_dot_nt(kvc, q_pad) + bias_t)

    blk_a = lax.broadcasted_iota(i32, (gps * rows, n_blk), 1)
    qi_a = lax.broadcasted_iota(i32, (gps * rows, n_blk), 0) & (NSA_BLOCK - 1)
    mask_c = (blk_a < c) | ((blk_a == c) & (qi_a == NSA_BLOCK - 1))
    s_c = jnp.where(mask_c, jnp.concatenate(sc_raw, axis=0), NEG_INF)
    e_c = jnp.where(mask_c, jnp.exp(s_c - jnp.max(s_c, axis=1, keepdims=True)), 0.0)
    l_c = jnp.sum(e_c, axis=1, keepdims=True)
    p_c = (e_c / jnp.where(l_c > 0.0, l_c, 1.0)).astype(bf16)

    blk_b = lax.broadcasted_iota(i32, (n_blk, gps * rows), 0)
    qi_b = lax.broadcasted_iota(i32, (n_blk, gps * rows), 1) & (NSA_BLOCK - 1)
    mask_t = (blk_b < c) | ((blk_b == c) & (qi_b == NSA_BLOCK - 1))
    s_t = jnp.where(mask_t, jnp.concatenate(st_raw, axis=1), NEG_INF)
    e_t = jnp.where(mask_t, jnp.exp(s_t - jnp.max(s_t, axis=0, keepdims=True)), 0.0)
    l_t = jnp.sum(e_t, axis=0, keepdims=True)
    p_t = e_t / jnp.where(l_t > 0.0, l_t, 1.0)

    out = []
    for i in range(gps):
        o_c = _dot(p_c[i * rows:(i + 1) * rows], kvcs[i])
        imp_t = p_t[:, i * rows:i * rows + LANES]
        for pr in range(1, NSA_HPG // 2):
            imp_t = imp_t + p_t[:, i * rows + pr * LANES:i * rows + (pr + 1) * LANES]
        imp_t = imp_t + pltpu.roll(imp_t, NSA_BLOCK, 1)
        out.append((q_parts[i], q_pads[i], o_c, imp_t))
    return out


def _nsa_prompt_front(c, pre, sel_t, n_blk):
    q_parts, q_pad, o_c, _ = pre
    lower = lax.broadcasted_iota(i32, (NSA_BLOCK, LANES), 1) < HEAD_DIM
    eye = (lax.broadcasted_iota(i32, (NSA_BLOCK, LANES), 0)
           == lax.broadcasted_iota(i32, (NSA_BLOCK, LANES), 1)).astype(bf16)
    pieces = [jnp.zeros((HEAD_DIM, LANES), bf16), sel_t]
    if n_blk < LANES - HEAD_DIM:
        pieces.append(jnp.zeros((LANES - HEAD_DIM - n_blk, LANES), bf16))
    sel_hi = _dot_nt(eye, jnp.concatenate(pieces, axis=0))
    lane = lax.broadcasted_iota(i32, (NSA_BLOCK, LANES), 1)
    first_near = c - (NEAR_BLOCKS - 1)
    far_mask = jnp.where((sel_hi > 0.5) & (lane - HEAD_DIM < first_near), 0.0, NEG_INF)
    q_aug = jnp.concatenate([jnp.where(lower, x, far_mask) for x in q_parts], axis=0).astype(bf16)

    def sel_col(b):
        col = jnp.sum(jnp.where(lane == HEAD_DIM + b, sel_hi, 0.0), axis=1, keepdims=True)
        return jnp.broadcast_to(col, (NSA_BLOCK, NEAR_TILE))

    near_blk = lax.broadcasted_iota(i32, (NSA_BLOCK, NEAR_TILE), 1) >> 6
    has_prev = jnp.where(c >= 1, 1.0, 0.0)
    visible = jnp.where(near_blk == 0, sel_col(c - 3),
                        jnp.where(near_blk == 1, sel_col(c - 2), jnp.where(near_blk == 2, has_prev, 1.0)))
    near_hidden = jnp.where(visible > 0.5, 0.0, NEG_INF)

    return q_pad, q_aug, near_hidden, o_c


def _nsa_prompt_window(c, q_pads, slab_refs, twin_ref):
    rows = NSA_HPG * NSA_BLOCK
    start_w = pl.multiple_of((c + 1) * NSA_BLOCK + (KV_PAD - WIN_TILE), NSA_BLOCK)
    col_w = lax.broadcasted_iota(i32, (1, WIN_TILE), 1)
    before_start = jnp.where(col_w < WIN_TILE - (c + 1) * NSA_BLOCK, NEG_INF, 0.0)
    s_w = jnp.concatenate([_dot_nt(q_pad, refs[2][0, 0, pl.ds(start_w, WIN_TILE), :]) + twin_ref[i]
                           for i, (q_pad, refs) in enumerate(zip(q_pads, slab_refs))], axis=0) + before_start
    p = jnp.exp(s_w - jnp.max(s_w, axis=1, keepdims=True)).astype(bf16)
    return [_flash_finish((None, _dot(p[i * rows:(i + 1) * rows], refs[3][0, 0, pl.ds(start_w, WIN_TILE), :])))
            for i, refs in enumerate(slab_refs)]


def _nsa_prompt_attn(q, gates, kvc, slabs, tsel, twin, cmpq, cmpt, n_batch, seq, gps=4):
    m = q.shape[0]
    n_blk = seq // NSA_BLOCK
    assert n_blk <= HEAD_DIM, "the block one-hot lives in the 64 spare lanes of the key slab"
    g_ = NSA_KV_HEADS
    wq = NSA_HPG * HEAD_DIM
    slab = (1, 1) + slabs[0].shape[2:]

    def slab_spec(i):
        return pl.BlockSpec(slab, lambda b, gg, c: (gg * gps + i, b, 0, 0))

    return pl.pallas_call(
        functools.partial(_nsa_prompt_kernel, n_blk=n_blk, gps=gps),
        grid=(n_batch, g_ // gps, n_blk),
        in_specs=[
            pl.BlockSpec((NSA_BLOCK, gps * wq), lambda b, gg, c: (b * n_blk + c, gg)),
            pl.BlockSpec((gps, NSA_BLOCK, LANES), lambda b, gg, c: (gg, b * n_blk + c, 0)),
            pl.BlockSpec((gps, n_blk, LANES), lambda b, gg, c: (gg, b, 0)),
            *[slab_spec(i) for i in range(gps) for _ in slabs],
            pl.BlockSpec((gps,) + tsel.shape[1:], lambda b, gg, c: (gg, 0, 0)),
            pl.BlockSpec((gps,) + twin.shape[1:], lambda b, gg, c: (gg, 0, 0)),
            pl.BlockSpec((gps,) + cmpq.shape[1:], lambda b, gg, c: (gg, 0, 0, 0)),
            pl.BlockSpec((gps,) + cmpt.shape[1:], lambda b, gg, c: (gg, 0, 0)),
        ],
        out_specs=pl.BlockSpec((NSA_BLOCK, gps * wq), lambda b, gg, c: (b * n_blk + c, gg)),
        out_shape=jax.ShapeDtypeStruct((m, NSA_HEADS * HEAD_DIM), f32),
        compiler_params=_cparams("parallel", "parallel", "arbitrary"),
        name="nsa_prompt_attn",
    )(q, gates, kvc, *(list(slabs) * gps), tsel, twin, cmpq, cmpt)


def _group_lane_mask(shape):
    r = lax.broadcasted_iota(i32, shape, 0)
    ln = lax.broadcasted_iota(i32, shape, 1)
    return (ln >> 6) == (r & (NSA_KV_HEADS - 1))


def _fold_groups(x):
    x = jnp.where(_group_lane_mask(x.shape), x, 0.0)
    out = x[:, 0:LANES]
    for pr in range(1, x.shape[1] // LANES):
        out = out + x[:, pr * LANES:(pr + 1) * LANES]
    return out + pltpu.roll(out, HEAD_DIM, 1)


def _select_blocks_sample(imp, cur, n_slots):
    imp_t = imp.T
    blk = lax.broadcasted_iota(i32, imp_t.shape, 0)
    valid = blk <= cur
    forced = (blk == 0) | (blk == cur) | (blk == cur - 1)
    if cur + 1 <= NSA_TOPK:
        return valid.astype(f32).T
    blk_f = blk.astype(f32)
    work = jnp.where(valid & jnp.logical_not(forced), imp_t, -1.0)
    sel = forced.astype(f32)
    for _ in range(NSA_TOPK - 3):
        mx = jnp.max(work, axis=0, keepdims=True)
        first = jnp.min(jnp.where(work == mx, blk_f, float(n_slots)), axis=0, keepdims=True)
        pick = blk_f == first
        work = jnp.where(pick, -2.0, work)
        sel = jnp.where(pick, 1.0, sel)
    return sel.T


def _nsa_sample_kernel(*refs, n_pages, page, past_len):
    pt_ref = refs[0]
    page_refs = refs[1:1 + n_pages]
    (win_ref, q_ref, kvn_ref, gt_ref, wc_ref, bkey_ref, bblk_ref, emat_ref,
     o_ref, nwin_ref, kcv_scr) = refs[1 + n_pages:]
    del pt_ref
    b = pl.program_id(0)
    gd = NSA_KV_HEADS * HEAD_DIM
    bpp = page // NSA_BLOCK
    n_past_blk = past_len // NSA_BLOCK
    n_slots = kcv_scr.shape[0]
    cur = n_past_blk
    scale = HEAD_DIM ** -0.5

    kvn = kvn_ref[pl.ds(b, 1), :]
    qm = jnp.where(_group_lane_mask((NSA_HEADS, gd)), q_ref[0] * scale, 0.0)
    qm_b = qm.astype(bf16)

    wc = wc_ref[...]
    for k in range(n_pages):
        cmp_part = page_refs[k][0, 0, :, 0:2 * gd].reshape(bpp, NSA_BLOCK, 2 * gd)
        kcv_scr[k * bpp:(k + 1) * bpp, :] = jnp.sum(cmp_part * wc[None], axis=1)
    row8 = lax.broadcasted_iota(i32, (8, 2 * gd), 0)
    kcv_scr[n_past_blk:n_past_blk + 8, :] = jnp.where(row8 == 0, kvn[:, 0:2 * gd] * wc[0:1, :], 0.0)
    rest = n_slots - n_past_blk - 8
    kcv_scr[n_past_blk + 8:n_slots, :] = jnp.zeros((rest, 2 * gd), f32)
    kcv = kcv_scr[...]
    kc_b = kcv[:, 0:gd].astype(bf16)
    vc_b = kcv[:, gd:2 * gd].astype(bf16)

    blk = lax.broadcasted_iota(i32, (NSA_HEADS, n_slots), 1)
    mask_c = blk * NSA_BLOCK + (NSA_BLOCK - 1) <= past_len
    s_c = jnp.where(mask_c, _dot_nt(qm_b, kc_b) + bblk_ref[...], NEG_INF)
    m_c = jnp.max(s_c, axis=1, keepdims=True)
    e_c = jnp.where(mask_c, jnp.exp(s_c - m_c), 0.0)
    l_c = jnp.sum(e_c, axis=1, keepdims=True)
    p_c = e_c / jnp.where(l_c > 0.0, l_c, 1.0)
    o_c = _fold_groups(_dot(p_c.astype(bf16), vc_b))

    imp = p_c[0:NSA_KV_HEADS]
    for hh in range(1, NSA_HPG):
        imp = imp + p_c[hh * NSA_KV_HEADS:(hh + 1) * NSA_KV_HEADS]
    sel = _select_blocks_sample(jnp.concatenate([imp, jnp.zeros_like(imp)], axis=0), cur, n_slots)
    sel16 = jnp.concatenate([sel[0:NSA_KV_HEADS]] * NSA_HPG, axis=0).astype(bf16)

    k_new = kvn[:, 2 * gd:3 * gd]
    v_new = kvn[:, 3 * gd:4 * gd]
    hidden = (_dot(sel16, emat_ref[...]) - 1.0) * (-NEG_INF)
    b_self = bkey_ref[:, past_len:past_len + LANES][:, 0:1]
    s_parts = [_dot_nt(qm_b, page_refs[k][0, 0, :, 2 * gd:3 * gd].astype(bf16)) for k in range(n_pages)]
    s_s = jnp.concatenate(s_parts, axis=1) + bkey_ref[:, 0:past_len] + hidden
    s_self = jnp.sum(qm * k_new, axis=1, keepdims=True) + b_self
    m_s = jnp.maximum(jnp.max(s_s, axis=1, keepdims=True), s_self)
    e_s = jnp.exp(s_s - m_s)
    e_self = jnp.exp(s_self - m_s)
    l_s = jnp.sum(e_s, axis=1, keepdims=True) + e_self
    e_sb = e_s.astype(bf16)
    pv = e_self * v_new
    for k in range(n_pages):
        pv = pv + _dot(e_sb[:, k * page:(k + 1) * page], page_refs[k][0, 0, :, 3 * gd:4 * gd].astype(bf16))
    o_s = _fold_groups(pv) / l_s

    n_buf = win_ref.shape[2]
    kw_b = win_ref[0, 0, :, 0:gd].astype(bf16)
    vw_b = win_ref[0, 0, :, gd:2 * gd].astype(bf16)
    kw_new = kvn[:, 4 * gd:5 * gd]
    vw_new = kvn[:, 5 * gd:6 * gd]
    wpos = past_len - n_buf + lax.broadcasted_iota(i32, (NSA_HEADS, n_buf), 1)
    msk_w = (past_len - wpos <= NSA_WINDOW) & (wpos >= 0)
    s_w = jnp.where(msk_w, _dot_nt(qm_b, kw_b) + bkey_ref[:, past_len - n_buf:past_len], NEG_INF)
    sw_self = jnp.sum(qm * kw_new, axis=1, keepdims=True) + b_self
    m_w = jnp.maximum(jnp.max(s_w, axis=1, keepdims=True), sw_self)
    e_w = jnp.where(msk_w, jnp.exp(s_w - m_w), 0.0)
    ew_self = jnp.exp(sw_self - m_w)
    l_w = jnp.sum(e_w, axis=1, keepdims=True) + ew_self
    o_w = _fold_groups(_dot(e_w.astype(bf16), vw_b) + ew_self * vw_new) / l_w

    gt = gt_ref[0]
    lane_g = lax.broadcasted_iota(i32, gt.shape, 1)

    def gate(branch):
        return jnp.sum(jnp.where(lane_g == branch, gt, 0.0), axis=1, keepdims=True)

    o_ref[0] = gate(0) * o_c + gate(1) * o_s + gate(2) * o_w

    nwin_ref[0, 0, 0:n_buf - 1, :] = win_ref[0, 0, 1:n_buf, :]
    nwin_ref[0, 0, n_buf - 1:n_buf, :] = kvn[:, 4 * gd:6 * gd]


def _nsa_sample_attn(page_table, cache_kv, cache_win, layer, q_hg, kv_nat, gates_hg, wc_nat, bkey, bblk,
                     past_len):
    n_dec, n_pages = page_table.shape
    _, n_phys, page, _, g_, d_ = cache_kv.shape
    gd = g_ * d_
    n_buf = cache_win.shape[2]
    n_slots = bblk.shape[1]
    cache2 = cache_kv.reshape(cache_kv.shape[0], n_phys, page, 4 * gd)
    win2 = cache_win.reshape(cache_win.shape[0], n_dec, n_buf, 2 * gd)
    emat = (lax.broadcasted_iota(i32, (n_slots, past_len), 0)
            == lax.broadcasted_iota(i32, (n_slots, past_len), 1) // NSA_BLOCK).astype(bf16)

    def page_spec(k):
        return pl.BlockSpec((1, 1, page, 4 * gd), lambda b, pt: (layer, pt[b * n_pages + k], 0, 0))

    grid_spec = pltpu.PrefetchScalarGridSpec(
        num_scalar_prefetch=1,
        grid=(n_dec,),
        in_specs=[
            *[page_spec(k) for k in range(n_pages)],
            pl.BlockSpec((1, 1, n_buf, 2 * gd), lambda b, pt: (layer, b, 0, 0)),
            pl.BlockSpec((1, NSA_HEADS, gd), lambda b, pt: (b, 0, 0)),
            pl.BlockSpec(kv_nat.shape, lambda b, pt: (0, 0)),
            pl.BlockSpec((1, NSA_HEADS, LANES), lambda b, pt: (b, 0, 0)),
            pl.BlockSpec(wc_nat.shape, lambda b, pt: (0, 0)),
            pl.BlockSpec(bkey.shape, lambda b, pt: (0, 0)),
            pl.BlockSpec(bblk.shape, lambda b, pt: (0, 0)),
            pl.BlockSpec(emat.shape, lambda b, pt: (0, 0)),
        ],
        out_specs=[
            pl.BlockSpec((1, NSA_HEADS, LANES), lambda b, pt: (b, 0, 0)),
            pl.BlockSpec((1, 1, n_buf, 2 * gd), lambda b, pt: (0, b, 0, 0)),
        ],
        scratch_shapes=[pltpu.VMEM((n_slots, 2 * gd), f32)],
    )
    o, nwin = pl.pallas_call(
        functools.partial(_nsa_sample_kernel, n_pages=n_pages, page=page, past_len=past_len),
        grid_spec=grid_spec,
        out_shape=[jax.ShapeDtypeStruct((n_dec, NSA_HEADS, LANES), f32),
                   jax.ShapeDtypeStruct((1, n_dec, n_buf, 2 * gd), f32)],
        compiler_params=_cparams("parallel"),
        name="nsa_sample_attn",
    )(page_table.reshape(-1), *([cache2] * n_pages), win2, q_hg, kv_nat, gates_hg, wc_nat, bkey, bblk, emat)
    return o[:, :, :HEAD_DIM], nwin[0]


def _conv_prompt_kernel(x_ref, g_ref, win_ref, cw_ref, wout_ref, o_ref, st_ref, carry_ref, *, n_t):
    t = pl.program_id(1)
    tm, d = x_ref.shape

    @pl.when(t == 0)
    def _():
        carry_ref[...] = jnp.zeros_like(carry_ref)

    x = x_ref[...]
    xn = _rms(x, g_ref[...]).astype(bf16)
    pr = _dot(xn, win_ref[...])
    u = pr[:, d:2 * d] * pr[:, 0:d]
    bg = pr[:, 2 * d:3 * d]
    prev = carry_ref[...]
    row = lax.broadcasted_iota(i32, (tm, d), 0)
    u1 = jnp.where(row == 0, prev[7:8, :], pltpu.roll(u, 1, 0))
    u2 = jnp.where(row == 0, prev[6:7, :], jnp.where(row == 1, prev[7:8, :], pltpu.roll(u, 2, 0)))
    cw = cw_ref[...]
    v = cw[0:1, :] * u2 + cw[1:2, :] * u1 + cw[2:3, :] * u
    o_ref[...] = x + _dot((bg * v).astype(bf16), wout_ref[...])
    carry_ref[...] = u[tm - 8:tm, :]

    @pl.when(t == n_t - 1)
    def _():
        st_ref[0] = u[tm - (CONV_W - 1):tm, :]


def _conv_prompt(x, g, w_in, cw, w_out, n_batch, seq, tm_target=512):
    m, d = x.shape
    tm = _row_tile(seq, tm_target)
    n_t = seq // tm
    return pl.pallas_call(
        functools.partial(_conv_prompt_kernel, n_t=n_t),
        grid=(n_batch, n_t),
        in_specs=[pl.BlockSpec((tm, d), lambda b, t: (b * n_t + t, 0)),
                  pl.BlockSpec((1, d), lambda b, t: (0, 0)),
                  pl.BlockSpec((d, 3 * d), lambda b, t: (0, 0)),
                  pl.BlockSpec((CONV_W, d), lambda b, t: (0, 0)),
                  pl.BlockSpec((d, d), lambda b, t: (0, 0))],
        out_specs=[pl.BlockSpec((tm, d), lambda b, t: (b * n_t + t, 0)),
                   pl.BlockSpec((1, CONV_W - 1, d), lambda b, t: (b, 0, 0))],
        out_shape=[jax.ShapeDtypeStruct((m, d), f32),
                   jax.ShapeDtypeStruct((n_batch, CONV_W - 1, d), f32)],
        scratch_shapes=[pltpu.VMEM((8, d), f32)],
        compiler_params=_cparams("parallel", "arbitrary"),
        name="conv_prompt",
    )(x, g.reshape(1, d), w_in, cw, w_out)


def _conv_sample_kernel(x_ref, g_ref, st_ref, win_ref, cw_ref, wout_ref, o_ref, nst_ref):
    d = x_ref.shape[1]
    x = x_ref[...]
    xn = _rms(x, g_ref[...]).astype(bf16)
    pr = _dot(xn, win_ref[...])
    u = pr[:, d:2 * d] * pr[:, 0:d]
    bg = pr[:, 2 * d:3 * d]
    st = st_ref[...]
    cw = cw_ref[...]
    v = cw[0:1, :] * st[:, 0:d] + cw[1:2, :] * st[:, d:2 * d] + cw[2:3, :] * u
    o_ref[...] = x + _dot((bg * v).astype(bf16), wout_ref[...])
    nst_ref[:, 0:d] = st[:, d:2 * d]
    nst_ref[:, d:2 * d] = u


def _conv_sample(x, g, state, w_in, cw, w_out):
    m, d = x.shape
    o, nst = pl.pallas_call(
        _conv_sample_kernel,
        out_shape=[jax.ShapeDtypeStruct((m, d), f32),
                   jax.ShapeDtypeStruct((m, (CONV_W - 1) * d), f32)],
        compiler_params=pltpu.CompilerParams(vmem_limit_bytes=VMEM_LIMIT_BYTES),
        name="conv_sample",
    )(x, g.reshape(1, d), state.reshape(m, (CONV_W - 1) * d), w_in, cw, w_out)
    return o, nst.reshape(m, CONV_W - 1, d)


def _pool_mix(win_sum_groups, xn, cnt_groups, wgrp_ref, scale):
    gw = xn.shape[1] // len(POOL_WINDOWS)
    outs = []
    for gi in range(len(POOL_WINDOWS)):
        diff = win_sum_groups[gi] / cnt_groups[gi] - xn[:, gi * gw:(gi + 1) * gw]
        outs.append(_dot(diff.astype(bf16), wgrp_ref[gi]))
    return jnp.concatenate(outs, axis=1) * scale


def _pool_prompt_kernel(x_ref, g_ref, wgrp_ref, sc_ref, o_ref, st_ref, carry_ref, *, n_t):
    t = pl.program_id(1)
    tm, d = x_ref.shape
    gw = d // len(POOL_WINDOWS)
    hist = carry_ref.shape[0]

    @pl.when(t == 0)
    def _():
        carry_ref[...] = jnp.zeros_like(carry_ref)

    x = x_ref[...]
    xn = _rms(x, g_ref[...])
    a = jnp.concatenate([carry_ref[...], xn], axis=0)
    sums = []
    s = a
    shift = 1
    for gi, w in enumerate(POOL_WINDOWS):
        while shift < w:
            s = s + pltpu.roll(s, shift, 0)
            shift *= 2
        sums.append(s[hist:, gi * gw:(gi + 1) * gw])
    pos = t * tm + lax.broadcasted_iota(i32, (tm, gw), 0)
    cnts = [jnp.minimum(pos + 1, w).astype(f32) for w in POOL_WINDOWS]
    o_ref[...] = x + _pool_mix(sums, xn, cnts, wgrp_ref, sc_ref[...])
    carry_ref[...] = xn[tm - hist:tm, :]

    @pl.when(t == n_t - 1)
    def _():
        st_ref[0] = xn[tm - POOL_HIST:tm, :]


def _pool_prompt(x, g, w_grp, scale, n_batch, seq, tm_target=512):
    m, d = x.shape
    tm = _row_tile(seq, tm_target)
    n_t = seq // tm
    return pl.pallas_call(
        functools.partial(_pool_prompt_kernel, n_t=n_t),
        grid=(n_batch, n_t),
        in_specs=[pl.BlockSpec((tm, d), lambda b, t: (b * n_t + t, 0)),
                  pl.BlockSpec((1, d), lambda b, t: (0, 0)),
                  pl.BlockSpec(w_grp.shape, lambda b, t: (0, 0, 0)),
                  pl.BlockSpec((1, d), lambda b, t: (0, 0))],
        out_specs=[pl.BlockSpec((tm, d), lambda b, t: (b * n_t + t, 0)),
                   pl.BlockSpec((1, POOL_HIST, d), lambda b, t: (b, 0, 0))],
        out_shape=[jax.ShapeDtypeStruct((m, d), f32),
                   jax.ShapeDtypeStruct((n_batch, POOL_HIST, d), f32)],
        scratch_shapes=[pltpu.VMEM((POOL_HIST + 1, d), f32)],
        compiler_params=_cparams("parallel", "arbitrary"),
        name="pool_prompt",
    )(x, g.reshape(1, d), w_grp, scale.reshape(1, d))


def _pool_sample_kernel(x_ref, g_ref, st_ref, wgrp_ref, sc_ref, o_ref, nst_ref, *, pos):
    m, d = x_ref.shape
    gw = d // len(POOL_WINDOWS)
    x = x_ref[...]
    xn = _rms(x, g_ref[...])
    st = st_ref[...]
    sums, cnts = [], []
    s = xn
    back = 1
    for gi, w in enumerate(POOL_WINDOWS):
        while back < w:
            s = s + st[:, (POOL_HIST - back) * d:(POOL_HIST - back + 1) * d]
            back += 1
        sums.append(s[:, gi * gw:(gi + 1) * gw])
        cnts.append(jnp.full((m, gw), float(min(pos + 1, w)), f32))
    o_ref[...] = x + _pool_mix(sums, xn, cnts, wgrp_ref, sc_ref[...])
    nst_ref[:, 0:(POOL_HIST - 1) * d] = st[:, d:POOL_HIST * d]
    nst_ref[:, (POOL_HIST - 1) * d:POOL_HIST * d] = xn


def _pool_sample(x, g, state, w_grp, scale, pos):
    m, d = x.shape
    o, nst = pl.pallas_call(
        functools.partial(_pool_sample_kernel, pos=pos),
        out_shape=[jax.ShapeDtypeStruct((m, d), f32),
                   jax.ShapeDtypeStruct((m, POOL_HIST * d), f32)],
        compiler_params=pltpu.CompilerParams(vmem_limit_bytes=VMEM_LIMIT_BYTES),
        name="pool_sample",
    )(x, g.reshape(1, d), state.reshape(m, POOL_HIST * d), w_grp, scale.reshape(1, d))
    return o, nst.reshape(m, POOL_HIST, d)


def _ca_prompt_kernel(x_ref, g_ref, wq_ref, kv_ref, wo_ref, o_ref):
    d = x_ref.shape[1]
    hd = d // CA_HEADS
    x = x_ref[...]
    xn = _rms(x, g_ref[...]).astype(bf16)
    q = (_dot(xn, wq_ref[...]) * (hd ** -0.5)).astype(bf16)
    outs = []
    for h in range(CA_HEADS):
        k = kv_ref[:, h * hd:(h + 1) * hd].astype(bf16)
        v = kv_ref[:, d + h * hd:d + (h + 1) * hd].astype(bf16)
        s = _dot_nt(q[:, h * hd:(h + 1) * hd], k)
        e = jnp.exp(s - jnp.max(s, axis=1, keepdims=True))
        p = e / jnp.sum(e, axis=1, keepdims=True)
        outs.append(_dot(p.astype(bf16), v))
    o = jnp.concatenate(outs, axis=1).astype(bf16)
    o_ref[...] = x + _dot(o, wo_ref[...])


def _ca_prompt(x, g, wq, mkv, wo, n_batch, seq, mem_len, tm_target=512):
    m, d = x.shape
    tm = _row_tile(seq, tm_target)
    n_t = seq // tm
    return pl.pallas_call(
        _ca_prompt_kernel,
        grid=(n_batch, n_t),
        in_specs=[pl.BlockSpec((tm, d), lambda b, t: (b * n_t + t, 0)),
                  pl.BlockSpec((1, d), lambda b, t: (0, 0)),
                  pl.BlockSpec((d, d), lambda b, t: (0, 0)),
                  pl.BlockSpec((mem_len, 2 * d), lambda b, t: (b, 0)),
                  pl.BlockSpec((d, d), lambda b, t: (0, 0))],
        out_specs=pl.BlockSpec((tm, d), lambda b, t: (b * n_t + t, 0)),
        out_shape=jax.ShapeDtypeStruct((m, d), f32),
        compiler_params=_cparams("parallel", "parallel"),
        name="ca_prompt",
    )(x, g.reshape(1, d), wq, mkv, wo)


def _ca_sample_kernel(q_ref, kv_ref, o_ref):
    bb, _, hd = q_ref.shape
    scale = hd ** -0.5
    for bi in range(bb):
        q = q_ref[bi] * scale
        k = kv_ref[0, bi, :, 0]
        v = kv_ref[0, bi, :, 1]
        s = jnp.sum(k * q[None], axis=2, keepdims=True)
        e = jnp.exp(s - jnp.max(s, axis=0, keepdims=True))
        p = e / jnp.sum(e, axis=0, keepdims=True)
        o_ref[bi] = jnp.sum(p * v, axis=0)


def _ca_sample(q, cache_mem_kv, layer, bb=4):
    n_dec, d = q.shape
    _, _, mem, _, heads, hd = cache_mem_kv.shape
    o = pl.pallas_call(
        _ca_sample_kernel,
        grid=(n_dec // bb,),
        in_specs=[pl.BlockSpec((bb, heads, hd), lambda i: (i, 0, 0)),
                  pl.BlockSpec((1, bb, mem, 2, heads, hd), lambda i: (layer, i, 0, 0, 0, 0))],
        out_specs=pl.BlockSpec((bb, heads, hd), lambda i: (i, 0, 0)),
        out_shape=jax.ShapeDtypeStruct((n_dec, heads, hd), f32),
        compiler_params=_cparams("parallel"),
        name="ca_sample",
    )(q.reshape(n_dec, heads, hd), cache_mem_kv)
    return o.reshape(n_dec, d)


def _nsa_weights(w_in, w_cmp):
    d_model = w_in.shape[0]
    nq = NSA_HEADS * HEAD_DIM
    nkv = 6 * NSA_KV_HEADS * HEAD_DIM
    wq = w_in[:, :nq].astype(bf16)
    wkv = w_in[:, nq:nq + nkv].reshape(d_model, 3, 2, NSA_KV_HEADS, HEAD_DIM)
    wkv = wkv.transpose(0, 1, 3, 2, 4).reshape(d_model, nkv).astype(bf16)
    wg = w_in[:, nq + nkv:].reshape(d_model, NSA_KV_HEADS, NSA_HPG * 3)
    wg = jnp.pad(wg, ((0, 0), (0, 0), (0, LANES - NSA_HPG * 3))).reshape(d_model, NSA_KV_HEADS * LANES)
    w_pack = jnp.concatenate([w_cmp[0], w_cmp[1]], axis=1)
    wc_nat = jnp.concatenate([jnp.tile(w_cmp[0], (1, NSA_KV_HEADS)),
                              jnp.tile(w_cmp[1], (1, NSA_KV_HEADS))], axis=1)
    return wq, wkv, wg.astype(bf16), w_pack, wc_nat


def _kv_natural(kvp):
    m = kvp.shape[1]
    x = kvp.reshape(3, NSA_KV_HEADS, m, 2, HEAD_DIM)
    return x.transpose(2, 0, 3, 1, 4).reshape(m, 6 * NSA_KV_HEADS * HEAD_DIM)


def kernel(x_prompt, x_sample, cache_nsa_kv, cache_nsa_win, state_conv, state_pool, cache_mem_kv,
           page_table, mem_prompt, rel_bias, norm_g, norm_mem_g, final_g, w_ffn_in, w_ffn_out,
           w_nsa_in, w_nsa_cmp, w_nsa_out, w_conv_in, conv_w, w_conv_out, w_pool, pool_scale,
           w_ca_q, w_ca_kv, w_ca_out):
    n_b, seq, d = x_prompt.shape
    n_dec = x_sample.shape[0]
    depth = norm_g.shape[0]
    mem_len = mem_prompt.shape[1]
    page = cache_nsa_kv.shape[2]
    past_len = page_table.shape[1] * page
    g_, hd = NSA_KV_HEADS, HEAD_DIM
    gd = g_ * hd

    xp = x_prompt.reshape(n_b * seq, d)
    xs = x_sample.reshape(n_dec, d)
    mem = mem_prompt.reshape(n_b * mem_len, d)

    tsel, twin, cmpq, cmpt = _t5_prompt_tables(rel_bias, seq // NSA_BLOCK)
    n_blk_pad = -(-(past_len // NSA_BLOCK + 8) // LANES) * LANES
    bkey, bblk = _t5_sample_tables(rel_bias, past_len, past_len + LANES, n_blk_pad)

    nsa_kv_p, nsa_win_p, conv_p, pool_p, mem_p = [], [], [], [], []
    nsa_kv_s, nsa_win_s, conv_s, pool_s = [], [], [], []
    for i in range(depth):
        kind, j = i % 3, i // 3
        wf_in = w_ffn_in[i].astype(bf16)
        wf_out = w_ffn_out[i].astype(bf16)
        xp = _ffn(xp, norm_g[i, 0], wf_in[0], wf_out[0])
        xs = _ffn(xs, norm_g[i, 0], wf_in[0], wf_out[0])
        if kind == 0:
            wq, wkv, wg, w_pack, wc_nat = _nsa_weights(w_nsa_in[j], w_nsa_cmp[j])
            w_out = w_nsa_out[j].astype(bf16)
            q, kvp, gates, kvc, *slabs = _nsa_proj(xp, norm_g[i, 1], wq, wkv, wg, n_b, slab_pad=KV_PAD,
                                                   w_pack=w_pack)
            o = _nsa_prompt_attn(q, gates, kvc, slabs, tsel, twin, cmpq, cmpt, n_b, seq)
            xp = _proj_res(o, w_out, xp)
            kv_nat = _kv_natural(kvp)
            nsa_kv_p.append(kv_nat[:, :4 * gd].reshape(n_b * seq // page, page, 4, g_, hd))
            n_keep = min(NSA_WINDOW, seq)
            nsa_win_p.append(kv_nat.reshape(n_b, seq, 6 * gd)[:, seq - n_keep:, 4 * gd:]
                             .reshape(n_b, n_keep, 2, g_, hd))
            q_s, kvp_s, gates_s = _nsa_proj(xs, norm_g[i, 1], wq, wkv, wg, 1)
            kv_nat_s = _kv_natural(kvp_s)
            q_hg = q_s.reshape(n_dec, g_, NSA_HPG, hd).transpose(0, 2, 1, 3).reshape(n_dec, NSA_HEADS, hd)
            q_hg = jnp.tile(q_hg, (1, 1, g_))
            gates_hg = gates_s.transpose(1, 0, 2)[:, :, :NSA_HPG * 3].reshape(n_dec, g_, NSA_HPG, 3)
            gates_hg = jnp.pad(gates_hg.transpose(0, 2, 1, 3).reshape(n_dec, NSA_HEADS, 3),
                               ((0, 0), (0, 0), (0, LANES - 3)))
            o_s, nwin = _nsa_sample_attn(page_table, cache_nsa_kv, cache_nsa_win, j, q_hg, kv_nat_s,
                                         gates_hg, wc_nat, bkey, bblk, past_len)
            o_s = o_s.reshape(n_dec, NSA_HPG, g_, hd).transpose(0, 2, 1, 3).reshape(n_dec, NSA_HEADS * hd)
            xs = _proj_res(o_s, w_out, xs)
            nsa_kv_s.append(kv_nat_s[:, :4 * gd].reshape(n_dec, 1, 4, g_, hd))
            nsa_win_s.append(nwin.reshape(n_dec, -1, 2, g_, hd))
        elif kind == 1:
            wc_in = w_conv_in[j].astype(bf16)
            wc_out = w_conv_out[j].astype(bf16)
            xp, cst = _conv_prompt(xp, norm_g[i, 1], wc_in, conv_w[j], wc_out, n_b, seq)
            xs, csts = _conv_sample(xs, norm_g[i, 1], state_conv[j], wc_in, conv_w[j], wc_out)
            conv_p.append(cst)
            conv_s.append(csts)
        else:
            wp = w_pool[j].astype(bf16)
            xp, pst = _pool_prompt(xp, norm_g[i, 1], wp, pool_scale[j], n_b, seq)
            xs, psts = _pool_sample(xs, norm_g[i, 1], state_pool[j], wp, pool_scale[j], past_len)
            pool_p.append(pst)
            pool_s.append(psts)
        w_q = w_ca_q[i].astype(bf16)
        w_o = w_ca_out[i].astype(bf16)
        mkv = _normproj(mem, norm_mem_g[i], w_ca_kv[i].astype(bf16))
        mem_p.append(mkv.reshape(n_b, mem_len, 2, CA_HEADS, d // CA_HEADS))
        xp = _ca_prompt(xp, norm_g[i, 2], w_q, mkv, w_o, n_b, seq, mem_len)
        q_ca = _normproj(xs, norm_g[i, 2], w_q)
        o_ca = _ca_sample(q_ca, cache_mem_kv, i)
        xs = _proj_res(o_ca, w_o, xs)
        fg = final_g if i == depth - 1 else None
        xp = _ffn(xp, norm_g[i, 3], wf_in[1], wf_out[1], final_g=fg)
        xs = _ffn(xs, norm_g[i, 3], wf_in[1], wf_out[1], final_g=fg)

    return (xp.reshape(n_b, seq, d), xs.reshape(n_dec, 1, d),
            jnp.stack(nsa_kv_p), jnp.stack(nsa_win_p), jnp.stack(conv_p), jnp.stack(pool_p),
            jnp.stack(mem_p),
            jnp.stack(nsa_kv_s), jnp.stack(nsa_win_s), jnp.stack(conv_s), jnp.stack(pool_s))
```
